```python
import jax, jax.numpy as jnp
from jax import lax
import numpy as np

D_MODEL = 1024
BATCH = 8
SEQ = 4096
DEPTH = 2

N_A = DEPTH // 2
N_B = DEPTH - N_A
CHUNK = 128
GMLP_WIDTH = 2 * D_MODEL
GMLP_GROUPS = 8
GMLP_GROUP_DIM = GMLP_WIDTH // GMLP_GROUPS
HEAD_DIM = 64
N_HEADS = D_MODEL // HEAD_DIM
Q_BLOCK = 128
N_EXPERTS = 32
TOP_K = 4
D_EXPERT = D_MODEL
SWIGLU_LIMIT = 7.0
SWIGLU_ALPHA = 1.702
EXPERT_BLOCK = 256
EPS = 1e-6
NEG_INF = -1e30

kernel_name = "yoco_gmlp_fox_moe_adaln"


def rms_norm(x, g):
    xf = x.astype(jnp.float32)
    y = xf * lax.rsqrt(jnp.mean(xf * xf, axis=-1, keepdims=True) + EPS)
    return (y * g.astype(jnp.float32)).astype(x.dtype)


def layer_norm(x, g, b):
    xf = x.astype(jnp.float32)
    mu = jnp.mean(xf, axis=-1, keepdims=True)
    var = jnp.mean(jnp.square(xf - mu), axis=-1, keepdims=True)
    y = (xf - mu) * lax.rsqrt(var + EPS)
    return (y * g.astype(jnp.float32) + b.astype(jnp.float32)).astype(x.dtype)


def modulate(h, shift, scale):
    return h * (1 + scale) + shift


def ada_chunks(c, w, b, n):
    mod = jax.nn.silu(c) @ w + b
    return [m[:, None, :] for m in jnp.split(mod, n, axis=-1)]


def gmlp_mixer(h, w_in, v_g, v_b, w_s, b_s, w_out):
    bsz, s, _ = h.shape
    z = jax.nn.gelu(h @ w_in, approximate=False)
    u, v = jnp.split(z, 2, axis=-1)
    v = layer_norm(v, v_g, v_b)
    v = v.reshape(bsz, s // CHUNK, CHUNK, GMLP_GROUPS, GMLP_GROUP_DIM)
    causal = jnp.tril(jnp.ones((CHUNK, CHUNK), dtype=bool))
    w_causal = jnp.where(causal[None], w_s, 0).astype(v.dtype)
    mixed = jnp.einsum('gts,bnsgd->bntgd', w_causal, v) + b_s.T[:, :, None]
    gated = u * mixed.reshape(bsz, s, GMLP_WIDTH)
    return gated @ w_out


def shared_kv(x, c, kv_ada_w, kv_ada_b, kv_norm_g, kv_w, k_norm_g, fgate_b):
    bsz, s, _ = x.shape
    shift, scale = ada_chunks(c, kv_ada_w, kv_ada_b, 2)
    h = modulate(rms_norm(x, kv_norm_g), shift, scale)
    proj = h @ kv_w
    k = proj[..., :D_MODEL].reshape(bsz, s, N_HEADS, HEAD_DIM)
    v = proj[..., D_MODEL:2 * D_MODEL].reshape(bsz, s, N_HEADS, HEAD_DIM)
    k = rms_norm(k, k_norm_g)
    log_f = jax.nn.log_sigmoid((proj[..., 2 * D_MODEL:] + fgate_b).astype(jnp.float32))
    log_f_cum = jnp.cumsum(log_f, axis=1)
    return k, v, log_f_cum


def forgetting_attention(q, k, v, log_f_cum):
    bsz, s, nh, dh = q.shape
    scale = dh ** -0.5
    lf = jnp.transpose(log_f_cum, (0, 2, 1))
    outs = []
    for i in range(s // Q_BLOCK):
        q0, end = i * Q_BLOCK, (i + 1) * Q_BLOCK
        qb, kb, vb = q[:, q0:end], k[:, :end], v[:, :end]
        logits = jnp.einsum('bqhd,bkhd->bhqk', qb, kb).astype(jnp.float32) * scale
        logits = logits + (lf[:, :, q0:end, None] - lf[:, :, None, :end])
        q_pos = q0 + jnp.arange(Q_BLOCK)
        k_pos = jnp.arange(end)
        logits = jnp.where(k_pos[None, :] <= q_pos[:, None], logits, NEG_INF)
        p = jax.nn.softmax(logits, axis=-1).astype(vb.dtype)
        outs.append(jnp.einsum('bhqk,bkhd->bqhd', p, vb))
    return jnp.concatenate(outs, axis=1)


def fox_mixer(h, k, v, log_f_cum, w_qg, q_norm_g, w_o):
    bsz, s, _ = h.shape
    qg = h @ w_qg
    q = rms_norm(qg[..., :D_MODEL].reshape(bsz, s, N_HEADS, HEAD_DIM), q_norm_g)
    o = forgetting_attention(q, k, v, log_f_cum).reshape(bsz, s, D_MODEL)
    o = o * jax.nn.sigmoid(qg[..., D_MODEL:])
    return o @ w_o


def moe(h, router_w, router_b, w1, b1, w2, b2):
    bsz, s, d = h.shape
    t = bsz * s
    tk = t * TOP_K
    xt = h.reshape(t, d)
    logits = (xt @ router_w + router_b).astype(jnp.float32)
    top_val, top_idx = lax.top_k(logits, TOP_K)
    gates = jax.nn.softmax(top_val, axis=-1)
    e_flat = top_idx.reshape(-1)
    tok_flat = jnp.arange(tk) // TOP_K
    order = jnp.argsort(e_flat)
    e_sorted = e_flat[order]
    counts = jnp.bincount(e_flat, length=N_EXPERTS)
    starts = jnp.cumsum(counts) - counts
    padded = (counts + EXPERT_BLOCK - 1) // EXPERT_BLOCK * EXPERT_BLOCK
    pends = jnp.cumsum(padded)
    pstarts = pends - padded
    dest_sorted = pstarts[e_sorted] + (jnp.arange(tk) - starts[e_sorted])
    dest = jnp.zeros((tk,), dtype=dest_sorted.dtype).at[order].set(dest_sorted)
    n_blocks = -(-(tk + N_EXPERTS * (EXPERT_BLOCK - 1)) // EXPERT_BLOCK)
    buf = jnp.zeros((n_blocks * EXPERT_BLOCK, d), dtype=xt.dtype).at[dest].set(xt[tok_flat])
    block_expert = jnp.minimum(
        jnp.searchsorted(pends, jnp.arange(n_blocks) * EXPERT_BLOCK, side='right'),
        N_EXPERTS - 1)

    def run_block(args):
        xb, e = args
        gu = xb @ w1[e] + b1[e]
        glu_in, lin = jnp.split(gu, 2, axis=-1)
        glu_in = jnp.minimum(glu_in, SWIGLU_LIMIT)
        lin = jnp.clip(lin, -SWIGLU_LIMIT, SWIGLU_LIMIT)
        act = (lin + 1) * (glu_in * jax.nn.sigmoid(SWIGLU_ALPHA * glu_in))
        return act @ w2[e] + b2[e]

    y_buf = lax.map(run_block, (buf.reshape(n_blocks, EXPERT_BLOCK, d), block_expert))
    y = y_buf.reshape(-1, d)[dest].reshape(t, TOP_K, d)
    out = jnp.einsum('tk,tkd->td', gates.astype(y.dtype), y)
    return out.reshape(bsz, s, d)


def setup_inputs(seed: int = 0) -> dict:
    key = jax.random.key(seed)
    ks = jax.random.split(key, 32)
    D = D_MODEL

    def nrm(k, shape, s):
        return jax.random.normal(k, shape, jnp.float32) * s

    return {
        "x": nrm(ks[0], (BATCH, SEQ, D), 1.0),
        "c": nrm(ks[1], (BATCH, D), 1.0),
        "ada_w": nrm(ks[2], (DEPTH, D, 6 * D), 0.5 * D ** -0.5),
        "ada_b": nrm(ks[3], (DEPTH, 6 * D), 0.02),
        "norm_mix_g": 1.0 + nrm(ks[4], (DEPTH, D), 0.05),
        "norm_ffn_g": 1.0 + nrm(ks[5], (DEPTH, D), 0.05),
        "gmlp_w_in": nrm(ks[6], (N_A, D, 2 * GMLP_WIDTH), D ** -0.5),
        "gmlp_v_g": 1.0 + nrm(ks[7], (N_A, GMLP_WIDTH), 0.05),
        "gmlp_v_b": nrm(ks[8], (N_A, GMLP_WIDTH), 0.02),
        "gmlp_w_s": nrm(ks[9], (N_A, GMLP_GROUPS, CHUNK, CHUNK), 0.5 * CHUNK ** -0.5),
        "gmlp_b_s": 1.0 + nrm(ks[10], (N_A, GMLP_GROUPS, CHUNK), 0.1),
        "gmlp_w_out": nrm(ks[11], (N_A, GMLP_WIDTH, D), GMLP_WIDTH ** -0.5),
        "kv_ada_w": nrm(ks[12], (D, 2 * D), 0.5 * D ** -0.5),
        "kv_ada_b": nrm(ks[13], (2 * D,), 0.02),
        "kv_norm_g": 1.0 + nrm(ks[14], (D,), 0.05),
        "kv_w": nrm(ks[15], (D, 2 * D + N_HEADS), D ** -0.5),
        "k_norm_g": 1.0 + nrm(ks[16], (HEAD_DIM,), 0.05),
        "fgate_b": 2.0 + nrm(ks[17], (N_HEADS,), 0.1),
        "fox_w_qg": nrm(ks[18], (N_B, D, 2 * D), D ** -0.5),
        "q_norm_g": 1.0 + nrm(ks[19], (N_B, HEAD_DIM), 0.05),
        "fox_w_o": nrm(ks[20], (N_B, D, D), D ** -0.5),
        "router_w": nrm(ks[21], (DEPTH, D, N_EXPERTS), D ** -0.5),
        "router_b": nrm(ks[22], (DEPTH, N_EXPERTS), 0.01),
        "exp_w1": nrm(ks[23], (DEPTH, N_EXPERTS, D, 2 * D_EXPERT), D ** -0.5),
        "exp_b1": nrm(ks[24], (DEPTH, N_EXPERTS, 2 * D_EXPERT), 0.01),
        "exp_w2": nrm(ks[25], (DEPTH, N_EXPERTS, D_EXPERT, D), D_EXPERT ** -0.5),
        "exp_b2": nrm(ks[26], (DEPTH, N_EXPERTS, D), 0.01),
        "final_g": 1.0 + nrm(ks[27], (D,), 0.05),
    }


def reference(x, c, ada_w, ada_b, norm_mix_g, norm_ffn_g, gmlp_w_in, gmlp_v_g, gmlp_v_b,
              gmlp_w_s, gmlp_b_s, gmlp_w_out, kv_ada_w, kv_ada_b, kv_norm_g, kv_w, k_norm_g,
              fgate_b, fox_w_qg, q_norm_g, fox_w_o, router_w, router_b, exp_w1, exp_b1,
              exp_w2, exp_b2, final_g):
    k_sh = v_sh = lf_sh = None
    for l in range(DEPTH):
        sh_m, sc_m, gt_m, sh_f, sc_f, gt_f = ada_chunks(c, ada_w[l], ada_b[l], 6)
        if l == N_A:
            k_sh, v_sh, lf_sh = shared_kv(x, c, kv_ada_w, kv_ada_b, kv_norm_g, kv_w,
                                          k_norm_g, fgate_b)
        h = modulate(rms_norm(x, norm_mix_g[l]), sh_m, sc_m)
        if l < N_A:
            y = gmlp_mixer(h, gmlp_w_in[l], gmlp_v_g[l], gmlp_v_b[l], gmlp_w_s[l],
                           gmlp_b_s[l], gmlp_w_out[l])
        else:
            j = l - N_A
            y = fox_mixer(h, k_sh, v_sh, lf_sh, fox_w_qg[j], q_norm_g[j], fox_w_o[j])
        x = x + gt_m * y
        h = modulate(rms_norm(x, norm_ffn_g[l]), sh_f, sc_f)
        x = x + gt_f * moe(h, router_w[l], router_b[l], exp_w1[l], exp_b1[l],
                           exp_w2[l], exp_b2[l])
    return rms_norm(x, final_g)
```

```python
import functools

import jax
import jax.numpy as jnp
from jax import lax
from jax.experimental import pallas as pl
from jax.experimental.pallas import tpu as pltpu

F32 = jnp.float32
BF16 = jnp.bfloat16

CHUNK = 128
GMLP_GROUPS = 8
HEAD_DIM = 64
TOP_K = 4
SWIGLU_LIMIT = 7.0
SWIGLU_ALPHA = 1.702
EPS = 1e-6
NEG_INF = -1e30

LANES = 128
EXPERT_ROWS = 256
VMEM_LIMIT = 56 * 1024 * 1024


def _cparams(sem, vmem=VMEM_LIMIT):
    return pltpu.CompilerParams(dimension_semantics=sem, vmem_limit_bytes=vmem)


def _divisor_tile(n, pref):
    if n <= pref:
        return n
    return max(t for t in range(LANES, pref + 1, LANES) if n % t == 0)


def _dot(a, b):
    return jnp.dot(a, b, preferred_element_type=F32)


def _dot_nt(a, b):
    return lax.dot_general(a, b, (((1,), (1,)), ((), ())), preferred_element_type=F32)


def _split3(x):
    hi = x.astype(BF16)
    r = x - hi.astype(F32)
    mid = r.astype(BF16)
    lo = (r - mid.astype(F32)).astype(BF16)
    return hi, mid, lo


def _split2(x):
    hi = x.astype(BF16)
    lo = (x - hi.astype(F32)).astype(BF16)
    return hi, lo


def _gelu_exact(x):
    return 0.5 * x * (1.0 + lax.erf(x * (0.5 ** 0.5)))


def _ada_norm(x, g, shift, scale):
    ms = jnp.mean(x * x, axis=-1, keepdims=True)
    y = x * lax.rsqrt(ms + EPS) * g
    return y * (1.0 + scale) + shift


def _ada_kernel(c_ref, w_ref, b_ref, o_ref):
    c = c_ref[...]
    sc = c * jax.nn.sigmoid(c)
    o_ref[0] = _dot(sc.astype(BF16), w_ref[0].astype(BF16)) + b_ref[0]


def _ada(c, w, b):
    nl, d, n = w.shape
    bsz = c.shape[0]
    tn = _divisor_tile(n, 2048)
    return pl.pallas_call(
        _ada_kernel,
        out_shape=jax.ShapeDtypeStruct((nl, bsz, n), F32),
        grid=(nl, n // tn),
        in_specs=[
            pl.BlockSpec((bsz, d), lambda l, j: (0, 0)),
            pl.BlockSpec((1, d, tn), lambda l, j: (l, 0, j)),
            pl.BlockSpec((1, 1, tn), lambda l, j: (l, 0, j)),
        ],
        out_specs=pl.BlockSpec((1, bsz, tn), lambda l, j: (l, 0, j)),
        compiler_params=_cparams(("arbitrary", "arbitrary")),
        name="ada",
    )(c, w, b.reshape(nl, 1, n))


def _gmlp_kernel(x_ref, sh_ref, sc_ref, gt_ref, g_ref, win_ref, vg_ref, vb_ref,
                 ws_ref, bst_ref, wout_ref, o_ref, gated_ref, *, ts, width):
    x = x_ref[0]
    h = _ada_norm(x, g_ref[...], sh_ref[0], sc_ref[0]).astype(BF16)
    z = _gelu_exact(_dot(h, win_ref[...]))
    u = z[:, :width]
    v = z[:, width:]
    mu = jnp.mean(v, axis=-1, keepdims=True)
    vc = v - mu
    var = jnp.mean(vc * vc, axis=-1, keepdims=True)
    vn = (vc * lax.rsqrt(var + EPS) * vg_ref[...] + vb_ref[...]).astype(BF16)

    gd = width // GMLP_GROUPS
    row = lax.broadcasted_iota(jnp.int32, (CHUNK, CHUNK), 0)
    col = lax.broadcasted_iota(jnp.int32, (CHUNK, CHUNK), 1)
    causal = col <= row
    bst = bst_ref[...]
    for g in range(GMLP_GROUPS):
        wc = jnp.where(causal, ws_ref[g], 0.0).astype(BF16)
        bias = bst[:, g:g + 1]
        for ci in range(ts // CHUNK):
            rs = slice(ci * CHUNK, (ci + 1) * CHUNK)
            cs = slice(g * gd, (g + 1) * gd)
            mixed = _dot(wc, vn[rs, cs]) + bias
            gated_ref[rs, cs] = (u[rs, cs] * mixed).astype(BF16)
    y = _dot(gated_ref[...], wout_ref[...])
    o_ref[0] = x + gt_ref[0] * y


def _gmlp(x, sh, sc, gt, g, w_in, v_g, v_b, w_s, b_s, w_out, ts=256):
    bsz, s, d = x.shape
    width = w_out.shape[0]
    ts = min(ts, s)
    vec = pl.BlockSpec((1, 1, d), lambda b, i: (b, 0, 0))
    const2 = lambda shape: pl.BlockSpec(shape, lambda b, i: (0, 0))
    return pl.pallas_call(
        functools.partial(_gmlp_kernel, ts=ts, width=width),
        out_shape=jax.ShapeDtypeStruct((bsz, s, d), F32),
        grid=(bsz, s // ts),
        in_specs=[
            pl.BlockSpec((1, ts, d), lambda b, i: (b, i, 0)),
            vec, vec, vec,
            const2((1, d)),
            const2((d, 2 * width)),
            const2((1, width)),
            const2((1, width)),
            pl.BlockSpec((GMLP_GROUPS, CHUNK, CHUNK), lambda b, i: (0, 0, 0)),
            const2((CHUNK, GMLP_GROUPS)),
            const2((width, d)),
        ],
        out_specs=pl.BlockSpec((1, ts, d), lambda b, i: (b, i, 0)),
        scratch_shapes=[pltpu.VMEM((ts, width), BF16)],
        compiler_params=_cparams(("parallel", "parallel")),
        name="gmlp",
    )(x, sh, sc, gt, g.reshape(1, d), w_in.astype(BF16), v_g.reshape(1, width),
      v_b.reshape(1, width), w_s, b_s.T, w_out.astype(BF16))


def _route_kernel(x_ref, sh_ref, sc_ref, g_ref, rwt_ref, rb_ref,
                  h_ref, idx_ref, gate_ref, rank_ref, cnt_ref, carry_ref, *, ts, n_exp):
    first = jnp.logical_and(pl.program_id(0) == 0, pl.program_id(1) == 0)

    @pl.when(first)
    def _():
        carry_ref[...] = jnp.zeros_like(carry_ref)

    h = _ada_norm(x_ref[0], g_ref[...], sh_ref[0], sc_ref[0])
    h_ref[0] = h
    logits = _dot_nt(rwt_ref[...], h.astype(BF16)) + rb_ref[...]

    eio = lax.broadcasted_iota(jnp.int32, (n_exp, ts), 0)
    work = logits
    vals, idxs, hots = [], [], []
    for _ in range(TOP_K):
        m = jnp.max(work, axis=0, keepdims=True)
        sel = jnp.min(jnp.where(work == m, eio, n_exp), axis=0, keepdims=True)
        hot = eio == sel
        vals.append(m)
        idxs.append(sel)
        hots.append(hot)
        work = jnp.where(hot, -jnp.inf, work)

    exps = [jnp.exp(v - vals[0]) for v in vals]
    denom = exps[0]
    for e in exps[1:]:
        denom = denom + e
    gates = [e / denom for e in exps]

    chosen = hots[0]
    for hot in hots[1:]:
        chosen = jnp.logical_or(chosen, hot)
    chosen_f = jnp.where(chosen, 1.0, 0.0)
    r = lax.broadcasted_iota(jnp.int32, (ts, ts), 0)
    c = lax.broadcasted_iota(jnp.int32, (ts, ts), 1)
    before = jnp.where(r < c, 1.0, 0.0).astype(BF16)
    base = carry_ref[...] + _dot(chosen_f.astype(BF16), before)
    ranks = [jnp.sum(jnp.where(hot, base, 0.0), axis=0, keepdims=True) for hot in hots]

    idx_ref[...] = jnp.concatenate(idxs, axis=0)
    gate_ref[...] = jnp.concatenate(gates, axis=0)
    rank_ref[...] = jnp.concatenate(ranks, axis=0).astype(jnp.int32)
    carry = carry_ref[...] + jnp.sum(chosen_f, axis=1, keepdims=True)
    carry_ref[...] = carry
    cnt_ref[...] = jnp.broadcast_to(carry, cnt_ref.shape)


def _route(x, sh, sc, g, router_w, router_b, ts=512):
    bsz, s, d = x.shape
    n_exp = router_w.shape[1]
    ts = min(ts, s)
    nt = s // ts
    t = bsz * s
    vec = pl.BlockSpec((1, 1, d), lambda b, i: (b, 0, 0))
    slot = pl.BlockSpec((TOP_K, ts), lambda b, i: (0, b * nt + i))
    return pl.pallas_call(
        functools.partial(_route_kernel, ts=ts, n_exp=n_exp),
        out_shape=(
            jax.ShapeDtypeStruct((bsz, s, d), F32),
            jax.ShapeDtypeStruct((TOP_K, t), jnp.int32),
            jax.ShapeDtypeStruct((TOP_K, t), F32),
            jax.ShapeDtypeStruct((TOP_K, t), jnp.int32),
            jax.ShapeDtypeStruct((n_exp, LANES), F32),
        ),
        grid=(bsz, nt),
        in_specs=[
            pl.BlockSpec((1, ts, d), lambda b, i: (b, i, 0)),
            vec, vec,
            pl.BlockSpec((1, d), lambda b, i: (0, 0)),
            pl.BlockSpec((n_exp, d), lambda b, i: (0, 0)),
            pl.BlockSpec((n_exp, 1), lambda b, i: (0, 0)),
        ],
        out_specs=(
            pl.BlockSpec((1, ts, d), lambda b, i: (b, i, 0)),
            slot, slot, slot,
            pl.BlockSpec((n_exp, LANES), lambda b, i: (0, 0)),
        ),
        scratch_shapes=[pltpu.VMEM((n_exp, 1), F32)],
        compiler_params=_cparams(("arbitrary", "arbitrary")),
        name="route",
    )(x, sh, sc, g.reshape(1, d), router_w.T.astype(BF16), router_b.reshape(n_exp, 1))


def _slotpos_kernel(pstart_ref, idx_ref, rank_ref, pos_ref, *, n_exp):
    idx = idx_ref[...]
    start = jnp.zeros_like(idx)
    for e in range(n_exp):
        start = jnp.where(idx == e, pstart_ref[e], start)
    pos_ref[...] = start + rank_ref[...]


def _slotpos(pstart, idx, rank, tp=4096):
    k, t = idx.shape
    tp = min(tp, t)
    blk = pl.BlockSpec((k, tp), lambda i, ps: (0, i))
    return pl.pallas_call(
        functools.partial(_slotpos_kernel, n_exp=pstart.shape[0]),
        out_shape=jax.ShapeDtypeStruct((k, t), jnp.int32),
        grid_spec=pltpu.PrefetchScalarGridSpec(
            num_scalar_prefetch=1, grid=(t // tp,), in_specs=[blk, blk], out_specs=blk),
        compiler_params=_cparams(("parallel",)),
        name="slotpos",
    )(pstart, idx, rank)


def _wait_rows(src, dst, sem):
    pltpu.make_async_copy(src, dst, sem).wait()


def _dispatch_kernel(pos_hbm, h_ref, zero_hbm, buf_hbm, pos_smem, sem_pos, sem_rows, *, ts):
    del zero_hbm
    i = pl.program_id(0)
    n = TOP_K * ts
    cp = pltpu.make_async_copy(pos_hbm.at[pl.ds(i * n, n)], pos_smem, sem_pos)
    cp.start()
    cp.wait()

    def body(t, carry):
        for k in range(TOP_K):
            p = pos_smem[k * ts + t]
            pltpu.make_async_copy(h_ref.at[pl.ds(t, 1)], buf_hbm.at[pl.ds(p, 1)], sem_rows).start()
        return carry

    lax.fori_loop(0, ts, body, 0)
    for k in range(TOP_K):
        _wait_rows(h_ref, buf_hbm.at[pl.ds(0, ts)], sem_rows)


def _dispatch(pos_flat, h, n_rows, ts=256):
    t, d = h.shape
    ts = min(ts, t)
    return pl.pallas_call(
        functools.partial(_dispatch_kernel, ts=ts),
        out_shape=jax.ShapeDtypeStruct((n_rows, d), F32),
        grid=(t // ts,),
        in_specs=[
            pl.BlockSpec(memory_space=pl.ANY),
            pl.BlockSpec((ts, d), lambda i: (i, 0)),
            pl.BlockSpec(memory_space=pl.ANY),
        ],
        out_specs=pl.BlockSpec(memory_space=pl.ANY),
        scratch_shapes=[
            pltpu.SMEM((TOP_K * ts,), jnp.int32),
            pltpu.SemaphoreType.DMA,
            pltpu.SemaphoreType.DMA,
        ],
        input_output_aliases={2: 0},
        compiler_params=_cparams(("arbitrary",)),
        name="dispatch",
    )(pos_flat, h, jnp.zeros((n_rows, d), F32))


def _expert_kernel(be_ref, nu_ref, x_ref, w1_ref, b1_ref, w2_ref, b2_ref, y_ref, *, d_exp):
    del be_ref

    @pl.when(pl.program_id(0) < nu_ref[0])
    def _():
        gu = _dot(x_ref[...].astype(BF16), w1_ref[0]) + b1_ref[0]
        glu = jnp.minimum(gu[:, :d_exp], SWIGLU_LIMIT)
        lin = jnp.clip(gu[:, d_exp:], -SWIGLU_LIMIT, SWIGLU_LIMIT)
        act = (lin + 1.0) * (glu * jax.nn.sigmoid(SWIGLU_ALPHA * glu))
        y_ref[...] = _dot(act.astype(BF16), w2_ref[0]) + b2_ref[0]

    @pl.when(pl.program_id(0) >= nu_ref[0])
    def _():
        y_ref[...] = jnp.zeros_like(y_ref)


def _experts(block_expert, n_used, buf, w1, b1, w2, b2):
    n_rows, d = buf.shape
    n_exp, _, d_exp2 = w1.shape
    d_exp = d_exp2 // 2
    tm = EXPERT_ROWS
    n_blocks = n_rows // tm

    def row_map(i, be, nu):
        return (i, 0)

    def w_map(i, be, nu):
        return (be[i], 0, 0)

    return pl.pallas_call(
        functools.partial(_expert_kernel, d_exp=d_exp),
        out_shape=jax.ShapeDtypeStruct((n_rows, d), F32),
        grid_spec=pltpu.PrefetchScalarGridSpec(
            num_scalar_prefetch=2,
            grid=(n_blocks,),
            in_specs=[
                pl.BlockSpec((tm, d), row_map),
                pl.BlockSpec((1, d, d_exp2), w_map),
                pl.BlockSpec((1, 1, d_exp2), w_map),
                pl.BlockSpec((1, d_exp, d), w_map),
                pl.BlockSpec((1, 1, d), w_map),
            ],
            out_specs=pl.BlockSpec((tm, d), row_map),
        ),
        compiler_params=_cparams(("arbitrary",)),
        name="experts",
    )(block_expert, n_used, buf, w1, b1.reshape(n_exp, 1, d_exp2), w2, b2.reshape(n_exp, 1, d))


def _combine_kernel(pos_hbm, y_hbm, gate_ref, x_ref, gt_ref, fg_ref, o_ref,
                    rows_ref, pos_smem, sem_pos, sem_rows, *, ts, final_norm):
    i = pl.program_id(0)
    n = TOP_K * ts
    cp = pltpu.make_async_copy(pos_hbm.at[pl.ds(i * n, n)], pos_smem, sem_pos)
    cp.start()
    cp.wait()

    def body(t, carry):
        for k in range(TOP_K):
            p = pos_smem[k * ts + t]
            pltpu.make_async_copy(y_hbm.at[pl.ds(p, 1)], rows_ref.at[k, pl.ds(t, 1)], sem_rows).start()
        return carry

    lax.fori_loop(0, ts, body, 0)
    for k in range(TOP_K):
        _wait_rows(y_hbm.at[pl.ds(0, ts)], rows_ref.at[k], sem_rows)

    gates = gate_ref[...]
    acc = gates[:, 0:1] * rows_ref[0]
    for k in range(1, TOP_K):
        acc = acc + gates[:, k:k + 1] * rows_ref[k]
    out = x_ref[...] + gt_ref[0] * acc
    if final_norm:
        ms = jnp.mean(out * out, axis=-1, keepdims=True)
        out = out * lax.rsqrt(ms + EPS) * fg_ref[...]
    o_ref[...] = out


def _combine(pos_flat, y_buf, gates_tk, x2d, gt, final_g, s, final_norm, ts=256):
    t, d = x2d.shape
    ts = min(ts, s)
    per_b = s // ts
    return pl.pallas_call(
        functools.partial(_combine_kernel, ts=ts, final_norm=final_norm),
        out_shape=jax.ShapeDtypeStruct((t, d), F32),
        grid=(t // ts,),
        in_specs=[
            pl.BlockSpec(memory_space=pl.ANY),
            pl.BlockSpec(memory_space=pl.ANY),
            pl.BlockSpec((ts, TOP_K), lambda i: (i, 0)),
            pl.BlockSpec((ts, d), lambda i: (i, 0)),
            pl.BlockSpec((1, 1, d), lambda i: (i // per_b, 0, 0)),
            pl.BlockSpec((1, d), lambda i: (0, 0)),
        ],
        out_specs=pl.BlockSpec((ts, d), lambda i: (i, 0)),
        scratch_shapes=[
            pltpu.VMEM((TOP_K, ts, d), F32),
            pltpu.SMEM((TOP_K * ts,), jnp.int32),
            pltpu.SemaphoreType.DMA,
            pltpu.SemaphoreType.DMA,
        ],
        compiler_params=_cparams(("arbitrary",)),
        name="combine",
    )(pos_flat, y_buf, gates_tk, x2d, gt, final_g.reshape(1, d))


def _moe_layer(x, sh, sc, gt, norm_g, router_w, router_b, w1, b1, w2, b2, final_g, final_norm):
    bsz, s, d = x.shape
    t = bsz * s
    n_exp = router_w.shape[1]
    tm = EXPERT_ROWS
    h, idx, gates, rank, cnt = _route(x, sh, sc, norm_g, router_w, router_b)

    counts = cnt[:, 0].astype(jnp.int32)
    padded = (counts + tm - 1) // tm * tm
    pends = jnp.cumsum(padded)
    pstart = pends - padded
    n_blocks = -(-(t * TOP_K + n_exp * (tm - 1)) // tm)
    n_used = (pends[-1] // tm).astype(jnp.int32)
    blk = jnp.minimum(jnp.arange(n_blocks, dtype=jnp.int32), n_used - 1)
    block_expert = jnp.minimum(
        jnp.searchsorted(pends, blk * tm, side='right'), n_exp - 1).astype(jnp.int32)

    pos = _slotpos(pstart.astype(jnp.int32), idx, rank)

    ts = min(256, s)
    pos_flat = pos.reshape(TOP_K, t // ts, ts).transpose(1, 0, 2).reshape(-1)
    buf = _dispatch(pos_flat, h.reshape(t, d), n_blocks * tm, ts=ts)
    y_buf = _experts(block_expert, n_used.reshape(1), buf,
                     w1.astype(BF16), b1, w2.astype(BF16), b2)
    out = _combine(pos_flat, y_buf, gates.T, x.reshape(t, d), gt, final_g, s, final_norm, ts=ts)
    return out.reshape(bsz, s, d)


def _head_rms(x, seg, seg_t, inv_dh):
    hi, lo = _split2(x * x)
    ssum = _dot(hi, seg) + _dot(lo, seg)
    inv = lax.rsqrt(ssum * inv_dh + EPS)
    ihi, ilo = _split2(inv)
    return _dot(ihi, seg_t) + _dot(ilo, seg_t)


def _kvq_kernel(x_ref, ksh_ref, ksc_ref, msh_ref, msc_ref, kg_ref, mg_ref,
                wk_ref, wv_ref, wf_ref, wft_ref, fb_ref, fbt_ref, kng_ref, qng_ref,
                wq_ref, wg_ref,
                k_ref, v_ref, q_ref, sg_ref, fcol_ref, frow_ref,
                ccol_ref, crow_ref, *, ts, d, n_heads):
    @pl.when(pl.program_id(1) == 0)
    def _():
        ccol_ref[...] = jnp.zeros_like(ccol_ref)
        crow_ref[...] = jnp.zeros_like(crow_ref)

    x = x_ref[0]
    ms = jnp.mean(x * x, axis=-1, keepdims=True)
    xn = x * lax.rsqrt(ms + EPS)
    hk = ((xn * kg_ref[...]) * (1.0 + ksc_ref[0]) + ksh_ref[0]).astype(BF16)
    hq = ((xn * mg_ref[...]) * (1.0 + msc_ref[0]) + msh_ref[0]).astype(BF16)

    di = lax.broadcasted_iota(jnp.int32, (d, LANES), 0) // HEAD_DIM
    hi_ = lax.broadcasted_iota(jnp.int32, (d, LANES), 1)
    seg = jnp.where(di == hi_, 1.0, 0.0).astype(BF16)
    dt = lax.broadcasted_iota(jnp.int32, (LANES, d), 1) // HEAD_DIM
    ht = lax.broadcasted_iota(jnp.int32, (LANES, d), 0)
    seg_t = jnp.where(dt == ht, 1.0, 0.0).astype(BF16)
    inv_dh = 1.0 / HEAD_DIM

    k = _dot(hk, wk_ref[...])
    k_ref[0] = (k * _head_rms(k, seg, seg_t, inv_dh) * kng_ref[...]).astype(BF16)
    v_ref[0] = _dot(hk, wv_ref[...]).astype(BF16)

    q = _dot(hq, wq_ref[...])
    qn = q * _head_rms(q, seg, seg_t, inv_dh) * qng_ref[...]
    q_ref[0] = (qn * (HEAD_DIM ** -0.5)).astype(BF16)
    sg_ref[0] = jax.nn.sigmoid(_dot(hq, wg_ref[...]))

    lf_c = jax.nn.log_sigmoid(_dot(hk, wf_ref[...]) + fb_ref[...])
    r = lax.broadcasted_iota(jnp.int32, (ts, ts), 0)
    c = lax.broadcasted_iota(jnp.int32, (ts, ts), 1)
    lower = jnp.where(c <= r, 1.0, 0.0).astype(BF16)
    upper = jnp.where(r <= c, 1.0, 0.0).astype(BF16)
    a, b, e = _split3(lf_c)
    cum_c = ccol_ref[...] + (_dot(lower, a) + _dot(lower, b) + _dot(lower, e))
    fcol_ref[0] = cum_c
    ccol_ref[...] = cum_c[ts - 1:ts, :]

    lf_r = jax.nn.log_sigmoid(_dot_nt(wft_ref[...], hk) + fbt_ref[...])
    a, b, e = _split3(lf_r)
    cum_r = crow_ref[...] + (_dot(a, upper) + _dot(b, upper) + _dot(e, upper))
    frow_ref[0] = cum_r
    crow_ref[...] = cum_r[:, ts - 1:ts]


def _kvq(x, ksh, ksc, msh, msc, kv_norm_g, mix_norm_g, kv_w, k_norm_g, fgate_b,
         w_qg, q_norm_g, ts=256):
    bsz, s, d = x.shape
    n_heads = d // HEAD_DIM
    ts = min(ts, s)
    wk = kv_w[:, :d].astype(BF16)
    wv = kv_w[:, d:2 * d].astype(BF16)
    wf = kv_w[:, 2 * d:]
    wf_pad = jnp.pad(wf, ((0, 0), (0, LANES - n_heads))).astype(BF16)
    fb_pad = jnp.pad(fgate_b, (0, LANES - n_heads)).reshape(1, LANES)
    vec = pl.BlockSpec((1, 1, d), lambda b, i: (b, 0, 0))
    c2 = lambda shape: pl.BlockSpec(shape, lambda b, i: (0, 0))
    tile = pl.BlockSpec((1, ts, d), lambda b, i: (b, i, 0))
    return pl.pallas_call(
        functools.partial(_kvq_kernel, ts=ts, d=d, n_heads=n_heads),
        out_shape=(
            jax.ShapeDtypeStruct((bsz, s, d), BF16),
            jax.ShapeDtypeStruct((bsz, s, d), BF16),
            jax.ShapeDtypeStruct((bsz, s, d), BF16),
            jax.ShapeDtypeStruct((bsz, s, d), F32),
            jax.ShapeDtypeStruct((bsz, s, LANES), F32),
            jax.ShapeDtypeStruct((bsz, n_heads, s), F32),
        ),
        grid=(bsz, s // ts),
        in_specs=[
            tile, vec, vec, vec, vec,
            c2((1, d)), c2((1, d)),
            c2((d, d)), c2((d, d)), c2((d, LANES)), c2((n_heads, d)),
            c2((1, LANES)), c2((n_heads, 1)),
            c2((1, d)), c2((1, d)),
            c2((d, d)), c2((d, d)),
        ],
        out_specs=(
            tile, tile, tile, tile,
            pl.BlockSpec((1, ts, LANES), lambda b, i: (b, i, 0)),
            pl.BlockSpec((1, n_heads, ts), lambda b, i: (b, 0, i)),
        ),
        scratch_shapes=[pltpu.VMEM((1, LANES), F32), pltpu.VMEM((n_heads, 1), F32)],
        compiler_params=_cparams(("arbitrary", "arbitrary")),
        name="kvq",
    )(x, ksh, ksc, msh, msc, kv_norm_g.reshape(1, d), mix_norm_g.reshape(1, d),
      wk, wv, wf_pad, wf.T.astype(BF16), fb_pad, fgate_b.reshape(n_heads, 1),
      jnp.tile(k_norm_g, n_heads).reshape(1, d), jnp.tile(q_norm_g, n_heads).reshape(1, d),
      w_qg[:, :d].astype(BF16), w_qg[:, d:].astype(BF16))


def _attn_kernel(q_ref, k_ref, v_ref, fcol_ref, frow_ref, o_ref, *, tq, tk):
    j = pl.program_id(1)
    qi = pl.program_id(2)
    q = q_ref[0]
    lane = lax.broadcasted_iota(jnp.int32, (tq, LANES), 1)
    first = lane < HEAD_DIM
    zero = jnp.zeros_like(q)
    q_heads = (jnp.where(first, q, zero), jnp.where(first, zero, q))
    fc = fcol_ref[0]
    fq = tuple(jnp.sum(jnp.where(lane == 2 * j + hh, fc, 0.0), axis=1, keepdims=True)
               for hh in range(2))

    def step(kb, carry, masked):
        k0 = pl.multiple_of(kb * tk, tk)
        ks = k_ref[0, pl.ds(k0, tk), :]
        vs = v_ref[0, pl.ds(k0, tk), :]
        fk = frow_ref[0, 0, :, pl.ds(k0, tk)]
        if masked:
            rowp = qi * tq + lax.broadcasted_iota(jnp.int32, (tq, tk), 0)
            colp = k0 + lax.broadcasted_iota(jnp.int32, (tq, tk), 1)
            keep = colp <= rowp
        out = []
        for hh in range(2):
            m_prev, l_prev, acc_prev = carry[hh]
            sc = _dot_nt(q_heads[hh], ks) + (fq[hh] - fk[hh:hh + 1, :])
            if masked:
                sc = jnp.where(keep, sc, NEG_INF)
            m_new = jnp.maximum(m_prev, jnp.max(sc, axis=1, keepdims=True))
            p = jnp.exp(sc - m_new)
            alpha = jnp.exp(m_prev - m_new)
            l_new = alpha * l_prev + jnp.sum(p, axis=1, keepdims=True)
            acc_new = alpha * acc_prev + _dot(p.astype(BF16), vs)
            out.append((m_new, l_new, acc_new))
        return tuple(out)

    init = tuple((jnp.full((tq, 1), NEG_INF, F32), jnp.zeros((tq, 1), F32),
                  jnp.zeros((tq, LANES), F32)) for _ in range(2))
    per_q = tq // tk
    carry = lax.fori_loop(0, qi * per_q, lambda kb, cr: step(kb, cr, False), init)
    for dgi in range(per_q):
        carry = step(qi * per_q + dgi, carry, True)
    (_, la, acca), (_, lb, accb) = carry
    o_ref[0] = jnp.where(first, acca / la, accb / lb)


def _attention(q, k, v, fcol, frow, tq=256, tk=256):
    bsz, s, d = q.shape
    n_pairs = d // LANES
    tq = min(tq, s)
    tk = min(tk, tq)
    frow4 = frow.reshape(bsz, n_pairs, 2, s)
    return pl.pallas_call(
        functools.partial(_attn_kernel, tq=tq, tk=tk),
        out_shape=jax.ShapeDtypeStruct((bsz, s, d), F32),
        grid=(bsz, n_pairs, s // tq),
        in_specs=[
            pl.BlockSpec((1, tq, LANES), lambda b, j, i: (b, i, j)),
            pl.BlockSpec((1, s, LANES), lambda b, j, i: (b, 0, j)),
            pl.BlockSpec((1, s, LANES), lambda b, j, i: (b, 0, j)),
            pl.BlockSpec((1, tq, LANES), lambda b, j, i: (b, i, 0)),
            pl.BlockSpec((1, 1, 2, s), lambda b, j, i: (b, j, 0, 0)),
        ],
        out_specs=pl.BlockSpec((1, tq, LANES), lambda b, j, i: (b, i, j)),
        compiler_params=_cparams(("parallel", "parallel", "arbitrary")),
        name="attn",
    )(q, k, v, fcol, frow4)


def _attnout_kernel(x_ref, o_ref, sg_ref, gt_ref, wo_ref, out_ref):
    y = _dot((o_ref[0] * sg_ref[0]).astype(BF16), wo_ref[...])
    out_ref[0] = x_ref[0] + gt_ref[0] * y


def _attnout(x, o, sg, gt, w_o, ts=512):
    bsz, s, d = x.shape
    ts = min(ts, s)
    tile = pl.BlockSpec((1, ts, d), lambda b, i: (b, i, 0))
    return pl.pallas_call(
        _attnout_kernel,
        out_shape=jax.ShapeDtypeStruct((bsz, s, d), F32),
        grid=(bsz, s // ts),
        in_specs=[tile, tile, tile,
                  pl.BlockSpec((1, 1, d), lambda b, i: (b, 0, 0)),
                  pl.BlockSpec((d, d), lambda b, i: (0, 0))],
        out_specs=tile,
        compiler_params=_cparams(("parallel", "parallel")),
        name="attnout",
    )(x, o, sg, gt, w_o.astype(BF16))


def _chunks(mod, n):
    return [m[:, None, :] for m in jnp.split(mod, n, axis=-1)]


def kernel(x, c, ada_w, ada_b, norm_mix_g, norm_ffn_g, gmlp_w_in, gmlp_v_g, gmlp_v_b, gmlp_w_s,
           gmlp_b_s, gmlp_w_out, kv_ada_w, kv_ada_b, kv_norm_g, kv_w, k_norm_g, fgate_b, fox_w_qg,
           q_norm_g, fox_w_o, router_w, router_b, exp_w1, exp_b1, exp_w2, exp_b2, final_g):
    mods = _ada(c, ada_w, ada_b)
    kv_mod = _ada(c, kv_ada_w[None], kv_ada_b[None])[0]

    sh_m, sc_m, gt_m, sh_f, sc_f, gt_f = _chunks(mods[0], 6)
    x = _gmlp(x, sh_m, sc_m, gt_m, norm_mix_g[0], gmlp_w_in[0], gmlp_v_g[0], gmlp_v_b[0],
              gmlp_w_s[0], gmlp_b_s[0], gmlp_w_out[0])
    x = _moe_layer(x, sh_f, sc_f, gt_f, norm_ffn_g[0], router_w[0], router_b[0],
                   exp_w1[0], exp_b1[0], exp_w2[0], exp_b2[0], final_g, False)

    sh_m, sc_m, gt_m, sh_f, sc_f, gt_f = _chunks(mods[1], 6)
    ksh, ksc = _chunks(kv_mod, 2)
    k, v, q, sg, fcol, frow = _kvq(x, ksh, ksc, sh_m, sc_m, kv_norm_g, norm_mix_g[1], kv_w,
                                   k_norm_g, fgate_b, fox_w_qg[0], q_norm_g[0])
    o = _attention(q, k, v, fcol, frow)
    x = _attnout(x, o, sg, gt_m, fox_w_o[0])
    return _moe_layer(x, sh_f, sc_f, gt_f, norm_ffn_g[1], router_w[1], router_b[1],
                      exp_w1[1], exp_b1[1], exp_w2[1], exp_b2[1], final_g, True)
```

```python
import functools

import numpy as np
import jax
import jax.numpy as jnp
from jax import lax
from jax.experimental import pallas as pl
from jax.experimental.pallas import tpu as pltpu

F32 = jnp.float32
BF16 = jnp.bfloat16

CHUNK = 128
GMLP_GROUPS = 8
HEAD_DIM = 64
TOP_K = 4
SWIGLU_LIMIT = 7.0
SWIGLU_ALPHA = 1.702
EPS = 1e-6
NEG_INF = -1e30
LOG2E = 1.4426950408889634
N_PARTS = 3

LANES = 128
EXPERT_ROWS = 256
VMEM_LIMIT = 56 * 1024 * 1024


def _cparams(sem, vmem=VMEM_LIMIT):
    return pltpu.CompilerParams(dimension_semantics=sem, vmem_limit_bytes=vmem)


def _divisor_tile(n, pref):
    if n <= pref:
        return n
    return max(t for t in range(LANES, pref + 1, LANES) if n % t == 0)


def _dot(a, b):
    return jnp.dot(a, b, preferred_element_type=F32)


def _dot_nt(a, b):
    return lax.dot_general(a, b, (((1,), (1,)), ((), ())), preferred_element_type=F32)


def _split3(x):
    hi = x.astype(BF16)
    r = x - hi.astype(F32)
    mid = r.astype(BF16)
    lo = (r - mid.astype(F32)).astype(BF16)
    return hi, mid, lo


def _split2(x):
    hi = x.astype(BF16)
    lo = (x - hi.astype(F32)).astype(BF16)
    return hi, lo


def _gelu_exact(x):
    return 0.5 * x * (1.0 + lax.erf(x * (0.5 ** 0.5)))


def _ada_norm(x, g, shift, scale):
    ms = jnp.mean(x * x, axis=-1, keepdims=True)
    y = x * lax.rsqrt(ms + EPS) * g
    return y * (1.0 + scale) + shift


def _ada_kernel(c_ref, w_ref, b_ref, o_ref):
    c = c_ref[...]
    sc = c * jax.nn.sigmoid(c)
    o_ref[0] = _dot(sc.astype(BF16), w_ref[0].astype(BF16)) + b_ref[0]


def _ada(c, w, b):
    nl, d, n = w.shape
    bsz = c.shape[0]
    tn = _divisor_tile(n, 2048)
    return pl.pallas_call(
        _ada_kernel,
        out_shape=jax.ShapeDtypeStruct((nl, bsz, n), F32),
        grid=(nl, n // tn),
        in_specs=[
            pl.BlockSpec((bsz, d), lambda l, j: (0, 0)),
            pl.BlockSpec((1, d, tn), lambda l, j: (l, 0, j)),
            pl.BlockSpec((1, 1, tn), lambda l, j: (l, 0, j)),
        ],
        out_specs=pl.BlockSpec((1, bsz, tn), lambda l, j: (l, 0, j)),
        compiler_params=_cparams(("arbitrary", "arbitrary")),
        name="ada",
    )(c, w, b.reshape(nl, 1, n))


def _gmlp_kernel(x_ref, sh_ref, sc_ref, gt_ref, g_ref, win_ref, vg_ref, vb_ref,
                 ws_ref, bst_ref, wout_ref, o_ref, gated_ref, *, ts, width):
    x = x_ref[0]
    h = _ada_norm(x, g_ref[...], sh_ref[0], sc_ref[0]).astype(BF16)
    z = _gelu_exact(_dot(h, win_ref[...]))
    u = z[:, :width]
    v = z[:, width:]
    mu = jnp.mean(v, axis=-1, keepdims=True)
    vc = v - mu
    var = jnp.mean(vc * vc, axis=-1, keepdims=True)
    vn = (vc * lax.rsqrt(var + EPS) * vg_ref[...] + vb_ref[...]).astype(BF16)

    gd = width // GMLP_GROUPS
    row = lax.broadcasted_iota(jnp.int32, (CHUNK, CHUNK), 0)
    col = lax.broadcasted_iota(jnp.int32, (CHUNK, CHUNK), 1)
    causal = col <= row
    bst = bst_ref[...]
    for g in range(GMLP_GROUPS):
        wc = jnp.where(causal, ws_ref[g], 0.0).astype(BF16)
        bias = bst[:, g:g + 1]
        for ci in range(ts // CHUNK):
            rs = slice(ci * CHUNK, (ci + 1) * CHUNK)
            cs = slice(g * gd, (g + 1) * gd)
            mixed = _dot(wc, vn[rs, cs]) + bias
            gated_ref[rs, cs] = (u[rs, cs] * mixed).astype(BF16)
    y = _dot(gated_ref[...], wout_ref[...])
    o_ref[0] = x + gt_ref[0] * y


def _gmlp(x, sh, sc, gt, g, w_in, v_g, v_b, w_s, b_s, w_out, ts=256):
    bsz, s, d = x.shape
    width = w_out.shape[0]
    ts = min(ts, s)
    vec = pl.BlockSpec((1, 1, d), lambda b, i: (b, 0, 0))
    const2 = lambda shape: pl.BlockSpec(shape, lambda b, i: (0, 0))
    return pl.pallas_call(
        functools.partial(_gmlp_kernel, ts=ts, width=width),
        out_shape=jax.ShapeDtypeStruct((bsz, s, d), F32),
        grid=(bsz, s // ts),
        in_specs=[
            pl.BlockSpec((1, ts, d), lambda b, i: (b, i, 0)),
            vec, vec, vec,
            const2((1, d)),
            const2((d, 2 * width)),
            const2((1, width)),
            const2((1, width)),
            pl.BlockSpec((GMLP_GROUPS, CHUNK, CHUNK), lambda b, i: (0, 0, 0)),
            const2((CHUNK, GMLP_GROUPS)),
            const2((width, d)),
        ],
        out_specs=pl.BlockSpec((1, ts, d), lambda b, i: (b, i, 0)),
        scratch_shapes=[pltpu.VMEM((ts, width), BF16)],
        compiler_params=_cparams(("parallel", "parallel")),
        name="gmlp",
    )(x, sh, sc, gt, g.reshape(1, d), w_in.astype(BF16), v_g.reshape(1, width),
      v_b.reshape(1, width), w_s, b_s.T, w_out.astype(BF16))


def _route_kernel(x_ref, sh_ref, sc_ref, g_ref, rwt_ref, rb_ref,
                  h_ref, idx_ref, gate_ref, rank_ref, cnt_ref, carry_ref, *, ts, n_exp):
    first = jnp.logical_and(pl.program_id(0) == 0, pl.program_id(1) == 0)

    @pl.when(first)
    def _():
        carry_ref[...] = jnp.zeros_like(carry_ref)

    h = _ada_norm(x_ref[0], g_ref[...], sh_ref[0], sc_ref[0])
    h_ref[0] = h
    logits = _dot_nt(rwt_ref[...], h.astype(BF16)) + rb_ref[...]

    eio = lax.broadcasted_iota(jnp.int32, (n_exp, ts), 0)
    work = logits
    vals, idxs, hots = [], [], []
    for _ in range(TOP_K):
        m = jnp.max(work, axis=0, keepdims=True)
        sel = jnp.min(jnp.where(work == m, eio, n_exp), axis=0, keepdims=True)
        hot = eio == sel
        vals.append(m)
        idxs.append(sel)
        hots.append(hot)
        work = jnp.where(hot, -jnp.inf, work)

    exps = [jnp.exp(v - vals[0]) for v in vals]
    denom = exps[0]
    for e in exps[1:]:
        denom = denom + e
    gates = [e / denom for e in exps]

    chosen = hots[0]
    for hot in hots[1:]:
        chosen = jnp.logical_or(chosen, hot)
    chosen_f = jnp.where(chosen, 1.0, 0.0)
    r = lax.broadcasted_iota(jnp.int32, (ts, ts), 0)
    c = lax.broadcasted_iota(jnp.int32, (ts, ts), 1)
    before = jnp.where(r < c, 1.0, 0.0).astype(BF16)
    base = carry_ref[...] + _dot(chosen_f.astype(BF16), before)
    ranks = [jnp.sum(jnp.where(hot, base, 0.0), axis=0, keepdims=True) for hot in hots]

    idx_ref[...] = jnp.concatenate(idxs, axis=0)
    gate_ref[...] = jnp.concatenate(gates, axis=0)
    rank_ref[...] = jnp.concatenate(ranks, axis=0).astype(jnp.int32)
    carry = carry_ref[...] + jnp.sum(chosen_f, axis=1, keepdims=True)
    carry_ref[...] = carry
    cnt_ref[...] = jnp.broadcast_to(carry, cnt_ref.shape)


def _route(x, sh, sc, g, router_w, router_b, ts=512):
    bsz, s, d = x.shape
    n_exp = router_w.shape[1]
    ts = min(ts, s)
    nt = s // ts
    t = bsz * s
    vec = pl.BlockSpec((1, 1, d), lambda b, i: (b, 0, 0))
    slot = pl.BlockSpec((TOP_K, ts), lambda b, i: (0, b * nt + i))
    return pl.pallas_call(
        functools.partial(_route_kernel, ts=ts, n_exp=n_exp),
        out_shape=(
            jax.ShapeDtypeStruct((bsz, s, d), F32),
            jax.ShapeDtypeStruct((TOP_K, t), jnp.int32),
            jax.ShapeDtypeStruct((TOP_K, t), F32),
            jax.ShapeDtypeStruct((TOP_K, t), jnp.int32),
            jax.ShapeDtypeStruct((n_exp, LANES), F32),
        ),
        grid=(bsz, nt),
        in_specs=[
            pl.BlockSpec((1, ts, d), lambda b, i: (b, i, 0)),
            vec, vec,
            pl.BlockSpec((1, d), lambda b, i: (0, 0)),
            pl.BlockSpec((n_exp, d), lambda b, i: (0, 0)),
            pl.BlockSpec((n_exp, 1), lambda b, i: (0, 0)),
        ],
        out_specs=(
            pl.BlockSpec((1, ts, d), lambda b, i: (b, i, 0)),
            slot, slot, slot,
            pl.BlockSpec((n_exp, LANES), lambda b, i: (0, 0)),
        ),
        scratch_shapes=[pltpu.VMEM((n_exp, 1), F32)],
        compiler_params=_cparams(("arbitrary", "arbitrary")),
        name="route",
    )(x, sh, sc, g.reshape(1, d), router_w.T.astype(BF16), router_b.reshape(n_exp, 1))


def _slotpos_kernel(pstart_ref, idx_ref, rank_ref, pos_ref, *, n_exp):
    idx = idx_ref[...]
    start = jnp.zeros_like(idx)
    for e in range(n_exp):
        start = jnp.where(idx == e, pstart_ref[e], start)
    pos_ref[...] = start + rank_ref[...]


def _slotpos(pstart, idx, rank, tp=4096):
    k, t = idx.shape
    tp = min(tp, t)
    blk = pl.BlockSpec((k, tp), lambda i, ps: (0, i))
    return pl.pallas_call(
        functools.partial(_slotpos_kernel, n_exp=pstart.shape[0]),
        out_shape=jax.ShapeDtypeStruct((k, t), jnp.int32),
        grid_spec=pltpu.PrefetchScalarGridSpec(
            num_scalar_prefetch=1, grid=(t // tp,), in_specs=[blk, blk], out_specs=blk),
        compiler_params=_cparams(("parallel",)),
        name="slotpos",
    )(pstart, idx, rank)


def _wait_rows(src, dst, sem):
    pltpu.make_async_copy(src, dst, sem).wait()


def _dispatch_kernel(pstart_ref, pend_ref, pos_hbm, h_ref, buf_hbm,
                     zero_ref, pos_smem, sem_pos, sem_rows, sem_zero, *, ts, tm, n_exp, n_blocks):
    i = pl.program_id(0)
    n = TOP_K * ts
    cp = pltpu.make_async_copy(pos_hbm.at[pl.ds(i * n, n)], pos_smem, sem_pos)
    cp.start()

    @pl.when(i == 0)
    def _():
        zero_ref[...] = jnp.zeros_like(zero_ref)
        for e in range(n_exp):
            @pl.when(pend_ref[e] > pstart_ref[e])
            def _():
                last = pl.multiple_of(pend_ref[e] - tm, tm)
                pltpu.make_async_copy(zero_ref, buf_hbm.at[pl.ds(last, tm)], sem_zero).start()
        n_used = pend_ref[n_exp - 1] // tm

        def zero_tail(b, carry):
            off = pl.multiple_of(b * tm, tm)
            pltpu.make_async_copy(zero_ref, buf_hbm.at[pl.ds(off, tm)], sem_zero).start()
            return carry

        def wait_tail(b, carry):
            _wait_rows(zero_ref, buf_hbm.at[pl.ds(0, tm)], sem_zero)
            return carry

        lax.fori_loop(n_used, n_blocks, zero_tail, 0)
        for e in range(n_exp):
            @pl.when(pend_ref[e] > pstart_ref[e])
            def _():
                _wait_rows(zero_ref, buf_hbm.at[pl.ds(0, tm)], sem_zero)
        lax.fori_loop(n_used, n_blocks, wait_tail, 0)

    cp.wait()

    def body(t, carry):
        for k in range(TOP_K):
            p = pos_smem[k * ts + t]
            pltpu.make_async_copy(h_ref.at[pl.ds(t, 1)], buf_hbm.at[pl.ds(p, 1)], sem_rows).start()
        return carry

    lax.fori_loop(0, ts, body, 0)
    for k in range(TOP_K):
        _wait_rows(h_ref, buf_hbm.at[pl.ds(0, ts)], sem_rows)


def _dispatch(pstart, pends, pos_flat, h, n_rows, ts=256):
    t, d = h.shape
    ts = min(ts, t)
    tm = EXPERT_ROWS
    return pl.pallas_call(
        functools.partial(_dispatch_kernel, ts=ts, tm=tm, n_exp=pstart.shape[0],
                          n_blocks=n_rows // tm),
        out_shape=jax.ShapeDtypeStruct((n_rows, d), F32),
        grid_spec=pltpu.PrefetchScalarGridSpec(
            num_scalar_prefetch=2,
            grid=(t // ts,),
            in_specs=[
                pl.BlockSpec(memory_space=pl.ANY),
                pl.BlockSpec((ts, d), lambda i, ps, pe: (i, 0)),
            ],
            out_specs=pl.BlockSpec(memory_space=pl.ANY),
            scratch_shapes=[
                pltpu.VMEM((tm, d), F32),
                pltpu.SMEM((TOP_K * ts,), jnp.int32),
                pltpu.SemaphoreType.DMA,
                pltpu.SemaphoreType.DMA,
                pltpu.SemaphoreType.DMA,
            ],
        ),
        compiler_params=_cparams(("arbitrary",)),
        name="dispatch",
    )(pstart, pends, pos_flat, h)


def _expert_kernel(be_ref, nu_ref, x_ref, w1_ref, b1_ref, w2_ref, b2_ref, y_ref,
                   w1b_ref, w2b_ref, *, d_exp):
    i = pl.program_id(0)
    changed = jnp.logical_or(i == 0, be_ref[i] != be_ref[jnp.maximum(i - 1, 0)])

    @pl.when(changed)
    def _():
        w1b_ref[...] = w1_ref[0, 0].astype(BF16)
        w2b_ref[...] = w2_ref[0, 0].astype(BF16)

    @pl.when(i < nu_ref[0])
    def _():
        gu = _dot(x_ref[...].astype(BF16), w1b_ref[...]) + b1_ref[0, 0]
        glu = jnp.minimum(gu[:, :d_exp], SWIGLU_LIMIT)
        lin = jnp.clip(gu[:, d_exp:], -SWIGLU_LIMIT, SWIGLU_LIMIT)
        act = (lin + 1.0) * (glu * jax.nn.sigmoid(SWIGLU_ALPHA * glu))
        y_ref[...] = _dot(act.astype(BF16), w2b_ref[...]) + b2_ref[0, 0]

    @pl.when(i >= nu_ref[0])
    def _():
        y_ref[...] = jnp.zeros_like(y_ref)


def _experts(block_expert, n_used, buf, w1, b1, w2, b2, layer):
    n_rows, d = buf.shape
    _, n_exp, _, d_exp2 = w1.shape
    d_exp = d_exp2 // 2
    tm = EXPERT_ROWS
    n_blocks = n_rows // tm

    def in_map(i, be, nu):
        return (jnp.minimum(i, nu[0] - 1), 0)

    def out_map(i, be, nu):
        return (i, 0)

    def w_map(i, be, nu):
        return (layer, be[i], 0, 0)

    return pl.pallas_call(
        functools.partial(_expert_kernel, d_exp=d_exp),
        out_shape=jax.ShapeDtypeStruct((n_rows, d), F32),
        grid_spec=pltpu.PrefetchScalarGridSpec(
            num_scalar_prefetch=2,
            grid=(n_blocks,),
            in_specs=[
                pl.BlockSpec((tm, d), in_map),
                pl.BlockSpec((1, 1, d, d_exp2), w_map),
                pl.BlockSpec((1, 1, 1, d_exp2), w_map),
                pl.BlockSpec((1, 1, d_exp, d), w_map),
                pl.BlockSpec((1, 1, 1, d), w_map),
            ],
            out_specs=pl.BlockSpec((tm, d), out_map),
            scratch_shapes=[pltpu.VMEM((d, d_exp2), BF16), pltpu.VMEM((d_exp, d), BF16)],
        ),
        compiler_params=_cparams(("arbitrary",)),
        name="experts",
    )(block_expert, n_used, buf, w1, b1.reshape(b1.shape[0], n_exp, 1, d_exp2),
      w2, b2.reshape(b2.shape[0], n_exp, 1, d))


def _combine_kernel(pos_hbm, y_hbm, gate_ref, x_ref, gt_ref, fg_ref, o_ref,
                    rows_ref, pos_smem, sem_pos, sem_rows, *, ts, final_norm):
    i = pl.program_id(0)
    n = TOP_K * ts
    cp = pltpu.make_async_copy(pos_hbm.at[pl.ds(i * n, n)], pos_smem, sem_pos)
    cp.start()
    cp.wait()

    def body(t, carry):
        for k in range(TOP_K):
            p = pos_smem[k * ts + t]
            pltpu.make_async_copy(y_hbm.at[pl.ds(p, 1)], rows_ref.at[k, pl.ds(t, 1)], sem_rows).start()
        return carry

    lax.fori_loop(0, ts, body, 0)
    for k in range(TOP_K):
        _wait_rows(y_hbm.at[pl.ds(0, ts)], rows_ref.at[k], sem_rows)

    gates = gate_ref[...]
    acc = gates[:, 0:1] * rows_ref[0]
    for k in range(1, TOP_K):
        acc = acc + gates[:, k:k + 1] * rows_ref[k]
    out = x_ref[...] + gt_ref[0] * acc
    if final_norm:
        ms = jnp.mean(out * out, axis=-1, keepdims=True)
        out = out * lax.rsqrt(ms + EPS) * fg_ref[...]
    o_ref[...] = out


def _combine(pos_flat, y_buf, gates_tk, x2d, gt, final_g, s, final_norm, ts=256):
    t, d = x2d.shape
    ts = min(ts, s)
    per_b = s // ts
    return pl.pallas_call(
        functools.partial(_combine_kernel, ts=ts, final_norm=final_norm),
        out_shape=jax.ShapeDtypeStruct((t, d), F32),
        grid=(t // ts,),
        in_specs=[
            pl.BlockSpec(memory_space=pl.ANY),
            pl.BlockSpec(memory_space=pl.ANY),
            pl.BlockSpec((ts, TOP_K), lambda i: (i, 0)),
            pl.BlockSpec((ts, d), lambda i: (i, 0)),
            pl.BlockSpec((1, 1, d), lambda i: (i // per_b, 0, 0)),
            pl.BlockSpec((1, d), lambda i: (0, 0)),
        ],
        out_specs=pl.BlockSpec((ts, d), lambda i: (i, 0)),
        scratch_shapes=[
            pltpu.VMEM((TOP_K, ts, d), F32),
            pltpu.SMEM((TOP_K * ts,), jnp.int32),
            pltpu.SemaphoreType.DMA,
            pltpu.SemaphoreType.DMA,
        ],
        compiler_params=_cparams(("arbitrary",)),
        name="combine",
    )(pos_flat, y_buf, gates_tk, x2d, gt, final_g.reshape(1, d))


def _moe_layer(x, sh, sc, gt, norm_g, router_w, router_b, w1, b1, w2, b2, layer, final_g, final_norm):
    bsz, s, d = x.shape
    t = bsz * s
    n_exp = router_w.shape[1]
    tm = EXPERT_ROWS
    h, idx, gates, rank, cnt = _route(x, sh, sc, norm_g, router_w, router_b)

    counts = cnt[:, 0].astype(jnp.int32)
    padded = (counts + tm - 1) // tm * tm
    pends = jnp.cumsum(padded).astype(jnp.int32)
    pstart = pends - padded
    n_blocks = -(-(t * TOP_K + n_exp * (tm - 1)) // tm)
    n_used = pends[-1] // tm
    blk = jnp.minimum(jnp.arange(n_blocks, dtype=jnp.int32), n_used - 1)
    block_expert = jnp.minimum(
        jnp.sum((pends[None, :] <= (blk * tm)[:, None]).astype(jnp.int32), axis=1), n_exp - 1)

    pos = _slotpos(pstart, idx, rank)

    ts = min(256, s)
    pos_flat = pos.reshape(TOP_K, t // ts, ts).transpose(1, 0, 2).reshape(-1)
    buf = _dispatch(pstart, pends, pos_flat, h.reshape(t, d), n_blocks * tm, ts=ts)
    y_buf = _experts(block_expert, n_used.reshape(1), buf, w1, b1, w2, b2, layer)
    out = _combine(pos_flat, y_buf, gates.T, x.reshape(t, d), gt, final_g, s, final_norm, ts=ts)
    return out.reshape(bsz, s, d)


def _head_rms(x, seg, seg_t, inv_dh):
    hi, lo = _split2(x * x)
    ssum = _dot(hi, seg) + _dot(lo, seg)
    inv = lax.rsqrt(ssum * inv_dh + EPS)
    ihi, ilo = _split2(inv)
    return _dot(ihi, seg_t) + _dot(ilo, seg_t)


def _kvq_kernel(x_ref, ksh_ref, ksc_ref, msh_ref, msc_ref, kg_ref, mg_ref,
                wk_ref, wvt_ref, wf_ref, fb_ref, kng_ref, qng_ref, wq_ref, wg_ref,
                place_ref, cst_ref,
                k_ref, vt_ref, q_ref, sg_ref, ccol_ref, *, ts, d):
    @pl.when(pl.program_id(1) == 0)
    def _():
        ccol_ref[...] = jnp.zeros_like(ccol_ref)

    x = x_ref[0]
    ms = jnp.mean(x * x, axis=-1, keepdims=True)
    xn = x * lax.rsqrt(ms + EPS)
    hk = ((xn * kg_ref[...]) * (1.0 + ksc_ref[0]) + ksh_ref[0]).astype(BF16)
    hq = ((xn * mg_ref[...]) * (1.0 + msc_ref[0]) + msh_ref[0]).astype(BF16)

    di = lax.broadcasted_iota(jnp.int32, (d, LANES), 0) // HEAD_DIM
    hi_ = lax.broadcasted_iota(jnp.int32, (d, LANES), 1)
    seg = jnp.where(di == hi_, 1.0, 0.0).astype(BF16)
    dt = lax.broadcasted_iota(jnp.int32, (LANES, d), 1) // HEAD_DIM
    ht = lax.broadcasted_iota(jnp.int32, (LANES, d), 0)
    seg_t = jnp.where(dt == ht, 1.0, 0.0).astype(BF16)
    inv_dh = 1.0 / HEAD_DIM

    k = _dot(hk, wk_ref[...])
    kn = (k * _head_rms(k, seg, seg_t, inv_dh) * kng_ref[...]).astype(BF16)
    vt_ref[0] = _dot_nt(wvt_ref[...], hk).astype(BF16)

    q = _dot(hq, wq_ref[...])
    qn = q * _head_rms(q, seg, seg_t, inv_dh) * qng_ref[...]
    qn = (qn * (HEAD_DIM ** -0.5 * LOG2E)).astype(BF16)
    sg_ref[0] = jax.nn.sigmoid(_dot(hq, wg_ref[...]))

    lf = jax.nn.log_sigmoid(_dot(hk, wf_ref[...]) + fb_ref[...])
    r = lax.broadcasted_iota(jnp.int32, (ts, ts), 0)
    c = lax.broadcasted_iota(jnp.int32, (ts, ts), 1)
    lower = jnp.where(c <= r, 1.0, 0.0).astype(BF16)
    a, b, e = _split3(lf)
    cum = ccol_ref[...] + (_dot(lower, a) + _dot(lower, b) + _dot(lower, e))
    ccol_ref[...] = cum[ts - 1:ts, :]
    a, b, e = _split3(cum * LOG2E)
    extra = (_dot(a, place_ref[0]) + _dot(b, place_ref[1]) + _dot(e, place_ref[2])
             + cst_ref[...]).astype(BF16)

    for j in range(d // LANES):
        src = slice(j * LANES, (j + 1) * LANES)
        k_ref[0, :, 2 * j * LANES:(2 * j + 1) * LANES] = kn[:, src]
        k_ref[0, :, (2 * j + 1) * LANES:(2 * j + 2) * LANES] = extra[:, src]
        q_ref[0, :, 2 * j * LANES:(2 * j + 1) * LANES] = qn[:, src]
        q_ref[0, :, (2 * j + 1) * LANES:(2 * j + 2) * LANES] = extra[:, d + j * LANES:d + (j + 1) * LANES]


def _bias_lane_tables(d):
    n_heads = d // HEAD_DIM
    place =np.zeros((N_PARTS, LANES, 2 * d), np.float32)
    cst = np.zeros((1, 2 * d), np.float32)
    for h in range(n_heads):
        j, hh = divmod(h, 2)
        for part in range(N_PARTS):
            place[part, h, j * LANES + N_PARTS * hh + part] = 1.0
            place[part, h, d + j * LANES + 2 * N_PARTS + N_PARTS * hh + part] = 1.0
    for j in range(d // LANES):
        cst[0, j * LANES + 2 * N_PARTS:j * LANES + 4 * N_PARTS] = 1.0
        cst[0, d + j * LANES:d + j * LANES + 2 * N_PARTS] = -1.0
    return jnp.asarray(place, BF16), jnp.asarray(cst, F32)


def _kvq(x, ksh, ksc, msh, msc, kv_norm_g, mix_norm_g, kv_w, k_norm_g, fgate_b,
         w_qg, q_norm_g, ts=256):
    bsz, s, d = x.shape
    n_heads = d // HEAD_DIM
    ts = min(ts, s)
    wk = kv_w[:, :d].astype(BF16)
    wvt = kv_w[:, d:2 * d].T.astype(BF16)
    wf = kv_w[:, 2 * d:]
    wf_pad = jnp.pad(wf, ((0, 0), (0, LANES - n_heads))).astype(BF16)
    fb_pad = jnp.pad(fgate_b, (0, LANES - n_heads)).reshape(1, LANES)
    place, cst = _bias_lane_tables(d)
    vec = pl.BlockSpec((1, 1, d), lambda b, i: (b, 0, 0))
    c2 = lambda shape: pl.BlockSpec(shape, lambda b, i: (0, 0))
    tile = pl.BlockSpec((1, ts, d), lambda b, i: (b, i, 0))
    tile2 = pl.BlockSpec((1, ts, 2 * d), lambda b, i: (b, i, 0))
    return pl.pallas_call(
        functools.partial(_kvq_kernel, ts=ts, d=d),
        out_shape=(
            jax.ShapeDtypeStruct((bsz, s, 2 * d), BF16),
            jax.ShapeDtypeStruct((bsz, d, s), BF16),
            jax.ShapeDtypeStruct((bsz, s, 2 * d), BF16),
            jax.ShapeDtypeStruct((bsz, s, d), F32),
        ),
        grid=(bsz, s // ts),
        in_specs=[
            tile, vec, vec, vec, vec,
            c2((1, d)), c2((1, d)),
            c2((d, d)), c2((d, d)), c2((d, LANES)), c2((1, LANES)),
            c2((1, d)), c2((1, d)),
            c2((d, d)), c2((d, d)),
            pl.BlockSpec((N_PARTS, LANES, 2 * d), lambda b, i: (0, 0, 0)),
            c2((1, 2 * d)),
        ],
        out_specs=(
            tile2,
            pl.BlockSpec((1, d, ts), lambda b, i: (b, 0, i)),
            tile2,
            tile,
        ),
        scratch_shapes=[pltpu.VMEM((1, LANES), F32)],
        compiler_params=_cparams(("arbitrary", "arbitrary")),
        name="kvq",
    )(x, ksh, ksc, msh, msc, kv_norm_g.reshape(1, d), mix_norm_g.reshape(1, d),
      wk, wvt, wf_pad, fb_pad,
      jnp.tile(k_norm_g, n_heads).reshape(1, d), jnp.tile(q_norm_g, n_heads).reshape(1, d),
      w_qg[:, :d].astype(BF16), w_qg[:, d:].astype(BF16), place, cst)


def _attn_kernel(q_ref, k_ref, vt_ref, o_ref, acc_ref, *, tq, tk):
    qi = pl.program_id(2)
    q = q_ref[0]
    lane = lax.broadcasted_iota(jnp.int32, (tq, 2 * LANES), 1)
    bias_lane = lane - LANES
    zero = jnp.zeros_like(q)
    q_heads = []
    for hh in range(2):
        feat = jnp.logical_and(lane >= hh * HEAD_DIM, lane < (hh + 1) * HEAD_DIM)
        minus = jnp.logical_and(bias_lane >= N_PARTS * hh, bias_lane < N_PARTS * (hh + 1))
        plus = jnp.logical_and(bias_lane >= N_PARTS * (2 + hh), bias_lane < N_PARTS * (3 + hh))
        q_heads.append(jnp.where(feat | minus | plus, q, zero))
    acc_ref[...] = jnp.zeros_like(acc_ref)

    def scores(kb):
        k0 = pl.multiple_of(kb * tk, tk)
        ks = k_ref[0, pl.ds(k0, tk), :]
        return tuple(_dot_nt(ks, q_heads[hh]) for hh in range(2))

    def softmax(sts, stats, masked):
        if masked:
            keep = (lax.broadcasted_iota(jnp.int32, (tk, tq), 0)
                    <= lax.broadcasted_iota(jnp.int32, (tk, tq), 1))
        ps, alphas, out = [], [], []
        for hh in range(2):
            m_prev, l_prev = stats[hh]
            st = sts[hh]
            if masked:
                st = jnp.where(keep, st, NEG_INF)
            m_new = jnp.maximum(m_prev, jnp.max(st, axis=0, keepdims=True))
            p = jnp.exp2(st - m_new)
            alpha = jnp.exp2(m_prev - m_new)
            out.append((m_new, alpha * l_prev + jnp.sum(p, axis=0, keepdims=True)))
            ps.append(p.astype(BF16))
            alphas.append(alpha)
        return tuple(ps), tuple(alphas), tuple(out)

    def pv_update(kb, ps, alphas):
        k0 = pl.multiple_of(kb * tk, tk)
        vt = vt_ref[0, :, pl.ds(k0, tk)]
        for hh in range(2):
            pv = _dot(vt[hh * HEAD_DIM:(hh + 1) * HEAD_DIM, :], ps[hh])
            acc_ref[hh] = alphas[hh] * acc_ref[hh] + pv

    def body(kb, carry):
        sts, stats = carry
        nxt = scores(kb + 1)
        ps, alphas, stats = softmax(sts, stats, False)
        pv_update(kb, ps, alphas)
        return nxt, stats

    init = (scores(0),
            tuple((jnp.full((1, tq), NEG_INF, F32), jnp.zeros((1, tq), F32)) for _ in range(2)))
    sts, stats = lax.fori_loop(0, qi, body, init)
    ps, alphas, stats = softmax(sts, stats, True)
    pv_update(qi, ps, alphas)
    (_, la), (_, lb) = stats
    ot = jnp.concatenate([acc_ref[0] / la, acc_ref[1] / lb], axis=0)
    o_ref[0] = ot.T


def _attention(q_aug, k_aug, vt, tq=512):
    bsz, s, d2 = q_aug.shape
    d = d2 // 2
    n_pairs = d // LANES
    tq = min(tq, s)
    tk = tq
    return pl.pallas_call(
        functools.partial(_attn_kernel, tq=tq, tk=tk),
        out_shape=jax.ShapeDtypeStruct((bsz, s, d), F32),
        grid=(bsz, n_pairs, s // tq),
        in_specs=[
            pl.BlockSpec((1, tq, 2 * LANES), lambda b, j, i: (b, i, j)),
            pl.BlockSpec((1, s, 2 * LANES), lambda b, j, i: (b, 0, j)),
            pl.BlockSpec((1, LANES, s), lambda b, j, i: (b, j, 0)),
        ],
        out_specs=pl.BlockSpec((1, tq, LANES), lambda b, j, i: (b, i, j)),
        scratch_shapes=[pltpu.VMEM((2, HEAD_DIM, tq), F32)],
        compiler_params=_cparams(("parallel", "parallel", "arbitrary")),
        name="attn",
    )(q_aug, k_aug, vt)


def _attnout_kernel(x_ref, o_ref, sg_ref, gt_ref, wo_ref, out_ref):
    y = _dot((o_ref[0] * sg_ref[0]).astype(BF16), wo_ref[...])
    out_ref[0] = x_ref[0] + gt_ref[0] * y


def _attnout(x, o, sg, gt, w_o, ts=512):
    bsz, s, d = x.shape
    ts = min(ts, s)
    tile = pl.BlockSpec((1, ts, d), lambda b, i: (b, i, 0))
    return pl.pallas_call(
        _attnout_kernel,
        out_shape=jax.ShapeDtypeStruct((bsz, s, d), F32),
        grid=(bsz, s // ts),
        in_specs=[tile, tile, tile,
                  pl.BlockSpec((1, 1, d), lambda b, i: (b, 0, 0)),
                  pl.BlockSpec((d, d), lambda b, i: (0, 0))],
        out_specs=tile,
        compiler_params=_cparams(("parallel", "parallel")),
        name="attnout",
    )(x, o, sg, gt, w_o.astype(BF16))


def _chunks(mod, n):
    return [m[:, None, :] for m in jnp.split(mod, n, axis=-1)]


def kernel(x, c, ada_w, ada_b, norm_mix_g, norm_ffn_g, gmlp_w_in, gmlp_v_g, gmlp_v_b, gmlp_w_s,
           gmlp_b_s, gmlp_w_out, kv_ada_w, kv_ada_b, kv_norm_g, kv_w, k_norm_g, fgate_b, fox_w_qg,
           q_norm_g, fox_w_o, router_w, router_b, exp_w1, exp_b1, exp_w2, exp_b2, final_g):
    mods = _ada(c, ada_w, ada_b)
    kv_mod = _ada(c, kv_ada_w[None], kv_ada_b[None])[0]

    sh_m, sc_m, gt_m, sh_f, sc_f, gt_f = _chunks(mods[0], 6)
    x = _gmlp(x, sh_m, sc_m, gt_m, norm_mix_g[0], gmlp_w_in[0], gmlp_v_g[0], gmlp_v_b[0],
              gmlp_w_s[0], gmlp_b_s[0], gmlp_w_out[0])
    x = _moe_layer(x, sh_f, sc_f, gt_f, norm_ffn_g[0], router_w[0], router_b[0],
                   exp_w1, exp_b1, exp_w2, exp_b2, 0, final_g, False)

    sh_m, sc_m, gt_m, sh_f, sc_f, gt_f = _chunks(mods[1], 6)
    ksh, ksc = _chunks(kv_mod, 2)
    k_aug, vt, q_aug, sg = _kvq(x, ksh, ksc, sh_m, sc_m, kv_norm_g, norm_mix_g[1], kv_w,
                                k_norm_g, fgate_b, fox_w_qg[0], q_norm_g[0])
    o = _attention(q_aug, k_aug, vt)
    x = _attnout(x, o, sg, gt_m, fox_w_o[0])
    return _moe_layer(x, sh_f, sc_f, gt_f, norm_ffn_g[1], router_w[1], router_b[1],
                      exp_w1, exp_b1, exp_w2, exp_b2, 1, final_g, True)
```

```python
import functools

import numpy as np
import jax
import jax.numpy as jnp
from jax import lax
from jax.experimental import pallas as pl
from jax.experimental.pallas import tpu as pltpu

F32 = jnp.float32
BF16 = jnp.bfloat16

CHUNK = 128
GMLP_GROUPS = 8
HEAD_DIM = 64
TOP_K = 4
SWIGLU_LIMIT = 7.0
SWIGLU_ALPHA = 1.702
EPS = 1e-6
NEG_INF = -1e30
LOG2E = 1.4426950408889634
N_PARTS = 3

LANES = 128
EXPERT_ROWS = 256
SEG_ALIGN = 8
MOE_TILE = 512
VMEM_LIMIT = 56 * 1024 * 1024


def _cparams(sem, vmem=VMEM_LIMIT):
    return pltpu.CompilerParams(dimension_semantics=sem, vmem_limit_bytes=vmem)


def _divisor_tile(n, pref):
    if n <= pref:
        return n
    return max(t for t in range(LANES, pref + 1, LANES) if n % t == 0)


def _dot(a, b):
    return jnp.dot(a, b, preferred_element_type=F32)


def _dot_nt(a, b):
    return lax.dot_general(a, b, (((1,), (1,)), ((), ())), preferred_element_type=F32)


def _split3(x):
    hi = x.astype(BF16)
    r = x - hi.astype(F32)
    mid = r.astype(BF16)
    lo = (r - mid.astype(F32)).astype(BF16)
    return hi, mid, lo


def _split2(x):
    hi = x.astype(BF16)
    lo = (x - hi.astype(F32)).astype(BF16)
    return hi, lo


def _gelu_exact(x):
    return 0.5 * x * (1.0 + lax.erf(x * (0.5 ** 0.5)))


def _ada_norm(x, g, shift, scale):
    ms = jnp.mean(x * x, axis=-1, keepdims=True)
    y = x * lax.rsqrt(ms + EPS) * g
    return y * (1.0 + scale) + shift


def _ada_kernel(c_ref, w_ref, b_ref, o_ref):
    c = c_ref[...]
    sc = c * jax.nn.sigmoid(c)
    o_ref[0] = _dot(sc.astype(BF16), w_ref[0].astype(BF16)) + b_ref[0]


def _ada(c, w, b):
    nl, d, n = w.shape
    bsz = c.shape[0]
    tn = _divisor_tile(n, 2048)
    return pl.pallas_call(
        _ada_kernel,
        out_shape=jax.ShapeDtypeStruct((nl, bsz, n), F32),
        grid=(nl, n // tn),
        in_specs=[
            pl.BlockSpec((bsz, d), lambda l, j: (0, 0)),
            pl.BlockSpec((1, d, tn), lambda l, j: (l, 0, j)),
            pl.BlockSpec((1, 1, tn), lambda l, j: (l, 0, j)),
        ],
        out_specs=pl.BlockSpec((1, bsz, tn), lambda l, j: (l, 0, j)),
        compiler_params=_cparams(("arbitrary", "arbitrary")),
        name="ada",
    )(c, w, b.reshape(nl, 1, n))


def _gmlp_kernel(x_ref, sh_ref, sc_ref, gt_ref, g_ref, win_ref, vg_ref, vb_ref,
                 ws_ref, bst_ref, wout_ref, o_ref, gated_ref, *, ts, width):
    x = x_ref[0]
    h = _ada_norm(x, g_ref[...], sh_ref[0], sc_ref[0]).astype(BF16)
    z = _gelu_exact(_dot(h, win_ref[...]))
    u = z[:, :width]
    v = z[:, width:]
    mu = jnp.mean(v, axis=-1, keepdims=True)
    vc = v - mu
    var = jnp.mean(vc * vc, axis=-1, keepdims=True)
    vn = (vc * lax.rsqrt(var + EPS) * vg_ref[...] + vb_ref[...]).astype(BF16)

    gd = width // GMLP_GROUPS
    row = lax.broadcasted_iota(jnp.int32, (CHUNK, CHUNK), 0)
    col = lax.broadcasted_iota(jnp.int32, (CHUNK, CHUNK), 1)
    causal = col <= row
    bst = bst_ref[...]
    for g in range(GMLP_GROUPS):
        wc = jnp.where(causal, ws_ref[g], 0.0).astype(BF16)
        bias = bst[:, g:g + 1]
        for ci in range(ts // CHUNK):
            rs = slice(ci * CHUNK, (ci + 1) * CHUNK)
            cs = slice(g * gd, (g + 1) * gd)
            mixed = _dot(wc, vn[rs, cs]) + bias
            gated_ref[rs, cs] = (u[rs, cs] * mixed).astype(BF16)
    y = _dot(gated_ref[...], wout_ref[...])
    o_ref[0] = x + gt_ref[0] * y


def _gmlp(x, sh, sc, gt, g, w_in, v_g, v_b, w_s, b_s, w_out, ts=256):
    bsz, s, d = x.shape
    width = w_out.shape[0]
    ts = min(ts, s)
    vec = pl.BlockSpec((1, 1, d), lambda b, i: (b, 0, 0))
    const2 = lambda shape: pl.BlockSpec(shape, lambda b, i: (0, 0))
    return pl.pallas_call(
        functools.partial(_gmlp_kernel, ts=ts, width=width),
        out_shape=jax.ShapeDtypeStruct((bsz, s, d), F32),
        grid=(bsz, s // ts),
        in_specs=[
            pl.BlockSpec((1, ts, d), lambda b, i: (b, i, 0)),
            vec, vec, vec,
            const2((1, d)),
            const2((d, 2 * width)),
            const2((1, width)),
            const2((1, width)),
            pl.BlockSpec((GMLP_GROUPS, CHUNK, CHUNK), lambda b, i: (0, 0, 0)),
            const2((CHUNK, GMLP_GROUPS)),
            const2((width, d)),
        ],
        out_specs=pl.BlockSpec((1, ts, d), lambda b, i: (b, i, 0)),
        scratch_shapes=[pltpu.VMEM((ts, width), BF16)],
        compiler_params=_cparams(("parallel", "parallel")),
        name="gmlp",
    )(x, sh, sc, gt, g.reshape(1, d), w_in.astype(BF16), v_g.reshape(1, width),
      v_b.reshape(1, width), w_s, b_s.T, w_out.astype(BF16))


def _route_kernel(x_ref, sh_ref, sc_ref, g_ref, rwt_ref, rb_ref,
                  h_ref, lpos_ref, gate_ref, tab_ref, tot_ref, carry_ref, *, ts, n_exp):
    first = jnp.logical_and(pl.program_id(0) == 0, pl.program_id(1) == 0)

    @pl.when(first)
    def _():
        carry_ref[...] = jnp.zeros_like(carry_ref)

    h = _ada_norm(x_ref[0], g_ref[...], sh_ref[0], sc_ref[0]).astype(BF16)
    h_ref[0] = h
    logits = _dot_nt(rwt_ref[...], h) + rb_ref[...]

    eio = lax.broadcasted_iota(jnp.int32, (n_exp, ts), 0)
    work = logits
    vals, hots = [], []
    for _ in range(TOP_K):
        m = jnp.max(work, axis=0, keepdims=True)
        sel = jnp.min(jnp.where(work == m, eio, n_exp), axis=0, keepdims=True)
        hot = eio == sel
        vals.append(m)
        hots.append(hot)
        work = jnp.where(hot, -jnp.inf, work)

    exps = [jnp.exp(v - vals[0]) for v in vals]
    denom = exps[0]
    for e in exps[1:]:
        denom = denom + e
    gates = [e / denom for e in exps]

    chosen = hots[0]
    for hot in hots[1:]:
        chosen = jnp.logical_or(chosen, hot)
    chosen_f = jnp.where(chosen, 1.0, 0.0)
    r = lax.broadcasted_iota(jnp.int32, (ts, ts), 0)
    c = lax.broadcasted_iota(jnp.int32, (ts, ts), 1)
    before = jnp.where(r < c, 1.0, 0.0).astype(BF16)
    prefix = _dot(chosen_f.astype(BF16), before)
    count = jnp.sum(chosen_f, axis=1, keepdims=True)
    seg_len = jnp.floor((count + (SEG_ALIGN - 1)) * (1.0 / SEG_ALIGN)) * SEG_ALIGN
    er = lax.broadcasted_iota(jnp.int32, (n_exp, n_exp), 0)
    ec = lax.broadcasted_iota(jnp.int32, (n_exp, n_exp), 1)
    lower = jnp.where(ec < er, 1.0, 0.0).astype(BF16)
    seg_start = _dot(lower, jnp.broadcast_to(seg_len, (n_exp, LANES)).astype(BF16))[:, 0:1]
    base = seg_start + prefix
    lpos = [jnp.sum(jnp.where(hot, base, 0.0), axis=0, keepdims=True) for hot in hots]

    lpos_ref[...] = jnp.concatenate(lpos, axis=0).astype(jnp.int32)
    gate_ref[...] = jnp.concatenate(gates, axis=0)
    used = carry_ref[...]
    lane = lax.broadcasted_iota(jnp.int32, (n_exp, LANES), 1)
    tab_ref[0] = jnp.where(lane == 0, seg_start,
                           jnp.where(lane == 1, used, jnp.where(lane == 2, seg_len, 0.0)))
    carry_ref[...] = used + seg_len
    tot_ref[...] = jnp.broadcast_to(used + seg_len, tot_ref.shape)


def _route(x, sh, sc, g, router_w, router_b, ts):
    bsz, s, d = x.shape
    n_exp = router_w.shape[1]
    nt = s // ts
    t = bsz * s
    vec = pl.BlockSpec((1, 1, d), lambda b, i: (b, 0, 0))
    slot = pl.BlockSpec((TOP_K, ts), lambda b, i: (0, b * nt + i))
    return pl.pallas_call(
        functools.partial(_route_kernel, ts=ts, n_exp=n_exp),
        out_shape=(
            jax.ShapeDtypeStruct((bsz, s, d), BF16),
            jax.ShapeDtypeStruct((TOP_K, t), jnp.int32),
            jax.ShapeDtypeStruct((TOP_K, t), F32),
            jax.ShapeDtypeStruct((bsz * nt, n_exp, LANES), F32),
            jax.ShapeDtypeStruct((n_exp, LANES), F32),
        ),
        grid=(bsz, nt),
        in_specs=[
            pl.BlockSpec((1, ts, d), lambda b, i: (b, i, 0)),
            vec, vec,
            pl.BlockSpec((1, d), lambda b, i: (0, 0)),
            pl.BlockSpec((n_exp, d), lambda b, i: (0, 0)),
            pl.BlockSpec((n_exp, 1), lambda b, i: (0, 0)),
        ],
        out_specs=(
            pl.BlockSpec((1, ts, d), lambda b, i: (b, i, 0)),
            slot, slot,
            pl.BlockSpec((1, n_exp, LANES), lambda b, i: (b * nt + i, 0, 0)),
            pl.BlockSpec((n_exp, LANES), lambda b, i: (0, 0)),
        ),
        scratch_shapes=[pltpu.VMEM((n_exp, 1), F32)],
        compiler_params=_cparams(("arbitrary", "arbitrary")),
        name="route",
    )(x, sh, sc, g.reshape(1, d), router_w.T.astype(BF16), router_b.reshape(n_exp, 1))


def _wait_rows(src, dst, sem):
    pltpu.make_async_copy(src, dst, sem).wait()


def _chunk_sizes(max_rows):
    sizes = []
    size = SEG_ALIGN
    while size <= max_rows:
        sizes.append(size)
        size *= 2
    return sizes


def _for_each_chunk(length, max_rows, fn):
    for size in reversed(_chunk_sizes(max_rows)):
        shift = size.bit_length() - 1
        @pl.when(((length >> shift) & 1) == 1)
        def _():
            fn(pl.multiple_of((length >> (shift + 1)) << (shift + 1), SEG_ALIGN), size)


def _segment_copies(tile, n_exp, max_rows, src_ref, dst_ref, len_ref, make_copy):
    for e in range(n_exp):
        s = tile * n_exp + e
        src, dst = src_ref[s], dst_ref[s]
        _for_each_chunk(len_ref[s], max_rows,
                        lambda off, size: make_copy(pl.multiple_of(src + off, SEG_ALIGN),
                                                    pl.multiple_of(dst + off, SEG_ALIGN),
                                                    size).start())


def _segment_wait(tile, n_exp, lrows, src_ref, len_ref, make_copy):
    last = tile * n_exp + n_exp - 1
    total = src_ref[last] + len_ref[last]
    _for_each_chunk(total, lrows, lambda off, size: make_copy(0, 0, size).wait())


def _dispatch_kernel(pstart_ref, pend_ref, src_ref, dst_ref, len_ref, lpos_ref, h_ref, buf_hbm,
                     comp_ref, zero_ref, sem_seg, sem_zero,
                     *, ts, lrows, tm, n_exp, n_blocks, n_tiles):
    i = pl.program_id(0)

    def seg_copy(local_row, global_row, size):
        return pltpu.make_async_copy(comp_ref.at[pl.ds(local_row, size)],
                                     buf_hbm.at[pl.ds(global_row, size)], sem_seg)

    @pl.when(i == 0)
    def _():
        zero_ref[...] = jnp.zeros_like(zero_ref)
        for e in range(n_exp):
            @pl.when(pend_ref[e] > pstart_ref[e])
            def _():
                last = pl.multiple_of(pend_ref[e] - tm, tm)
                pltpu.make_async_copy(zero_ref, buf_hbm.at[pl.ds(last, tm)], sem_zero).start()
        n_used = pend_ref[n_exp - 1] // tm

        def zero_tail(b, carry):
            off = pl.multiple_of(b * tm, tm)
            pltpu.make_async_copy(zero_ref, buf_hbm.at[pl.ds(off, tm)], sem_zero).start()
            return carry

        def wait_tail(b, carry):
            _wait_rows(zero_ref, buf_hbm.at[pl.ds(0, tm)], sem_zero)
            return carry

        lax.fori_loop(n_used, n_blocks, zero_tail, 0)
        for e in range(n_exp):
            @pl.when(pend_ref[e] > pstart_ref[e])
            def _():
                _wait_rows(zero_ref, buf_hbm.at[pl.ds(0, tm)], sem_zero)
        lax.fori_loop(n_used, n_blocks, wait_tail, 0)

    lpos = lpos_ref[...]
    rows = lax.broadcasted_iota(jnp.int32, (lrows, ts), 0)
    hit = rows == lpos[0:1, :]
    for k in range(1, TOP_K):
        hit = jnp.logical_or(hit, rows == lpos[k:k + 1, :])
    onehot = jnp.where(hit, 1.0, 0.0).astype(BF16)

    @pl.when(i > 0)
    def _():
        _segment_wait(i - 1, n_exp, lrows, src_ref, len_ref, seg_copy)

    comp_ref[...] = _dot(onehot, h_ref[...])
    _segment_copies(i, n_exp, ts, src_ref, dst_ref, len_ref, seg_copy)

    @pl.when(i == n_tiles - 1)
    def _():
        _segment_wait(i, n_exp, lrows, src_ref, len_ref, seg_copy)


def _dispatch(pstart, pends, seg_src, seg_dst, seg_len, lpos, h, n_rows, ts, lrows):
    t, d = h.shape
    tm = EXPERT_ROWS
    n_tiles = t // ts
    tile_map = lambda i, *_: (i, 0)
    return pl.pallas_call(
        functools.partial(_dispatch_kernel, ts=ts, lrows=lrows, tm=tm, n_exp=pstart.shape[0],
                          n_blocks=n_rows // tm, n_tiles=n_tiles),
        out_shape=jax.ShapeDtypeStruct((n_rows, d), F32),
        grid_spec=pltpu.PrefetchScalarGridSpec(
            num_scalar_prefetch=5,
            grid=(n_tiles,),
            in_specs=[
                pl.BlockSpec((TOP_K, ts), lambda i, *_: (0, i)),
                pl.BlockSpec((ts, d), tile_map),
            ],
            out_specs=pl.BlockSpec(memory_space=pl.ANY),
            scratch_shapes=[
                pltpu.VMEM((lrows, d), F32),
                pltpu.VMEM((tm, d), F32),
                pltpu.SemaphoreType.DMA,
                pltpu.SemaphoreType.DMA,
            ],
        ),
        compiler_params=_cparams(("arbitrary",)),
        name="dispatch",
    )(pstart, pends, seg_src, seg_dst, seg_len, lpos, h)


def _expert_kernel(be_ref, nu_ref, x_ref, w1_ref, b1_ref, w2_ref, b2_ref, y_ref,
                   w1b_ref, w2b_ref, *, d_exp):
    i = pl.program_id(0)
    changed = jnp.logical_or(i == 0, be_ref[i] != be_ref[jnp.maximum(i - 1, 0)])

    @pl.when(changed)
    def _():
        w1b_ref[...] = w1_ref[0, 0].astype(BF16)
        w2b_ref[...] = w2_ref[0, 0].astype(BF16)

    @pl.when(i < nu_ref[0])
    def _():
        gu = _dot(x_ref[...].astype(BF16), w1b_ref[...]) + b1_ref[0, 0]
        glu = jnp.minimum(gu[:, :d_exp], SWIGLU_LIMIT)
        lin = jnp.clip(gu[:, d_exp:], -SWIGLU_LIMIT, SWIGLU_LIMIT)
        act = (lin + 1.0) * (glu * jax.nn.sigmoid(SWIGLU_ALPHA * glu))
        y_ref[...] = _dot(act.astype(BF16), w2b_ref[...]) + b2_ref[0, 0]

    @pl.when(i >= nu_ref[0])
    def _():
        y_ref[...] = jnp.zeros_like(y_ref)


def _experts(block_expert, n_used, buf, w1, b1, w2, b2, layer):
    n_rows, d = buf.shape
    _, n_exp, _, d_exp2 = w1.shape
    d_exp = d_exp2 // 2
    tm = EXPERT_ROWS
    n_blocks = n_rows // tm

    def in_map(i, be, nu):
        return (jnp.minimum(i, nu[0] - 1), 0)

    def out_map(i, be, nu):
        return (i, 0)

    def w_map(i, be, nu):
        return (layer, be[i], 0, 0)

    return pl.pallas_call(
        functools.partial(_expert_kernel, d_exp=d_exp),
        out_shape=jax.ShapeDtypeStruct((n_rows, d), F32),
        grid_spec=pltpu.PrefetchScalarGridSpec(
            num_scalar_prefetch=2,
            grid=(n_blocks,),
            in_specs=[
                pl.BlockSpec((tm, d), in_map),
                pl.BlockSpec((1, 1, d, d_exp2), w_map),
                pl.BlockSpec((1, 1, 1, d_exp2), w_map),
                pl.BlockSpec((1, 1, d_exp, d), w_map),
                pl.BlockSpec((1, 1, 1, d), w_map),
            ],
            out_specs=pl.BlockSpec((tm, d), out_map),
            scratch_shapes=[pltpu.VMEM((d, d_exp2), BF16), pltpu.VMEM((d_exp, d), BF16)],
        ),
        compiler_params=_cparams(("arbitrary",)),
        name="experts",
    )(block_expert, n_used, buf, w1, b1.reshape(b1.shape[0], n_exp, 1, d_exp2),
      w2, b2.reshape(b2.shape[0], n_exp, 1, d))


def _combine_kernel(src_ref, dst_ref, len_ref, y_hbm, lpos_ref, gate_ref, x_ref, gt_ref, fg_ref,
                    o_ref, ycomp_ref, sem_seg, *, ts, lrows, n_exp, final_norm):
    i = pl.program_id(0)

    def seg_copy(local_row, global_row, size):
        return pltpu.make_async_copy(y_hbm.at[pl.ds(global_row, size)],
                                     ycomp_ref.at[pl.ds(local_row, size)], sem_seg)

    @pl.when(i == 0)
    def _():
        ycomp_ref[...] = jnp.zeros_like(ycomp_ref)

    _segment_copies(i, n_exp, ts, src_ref, dst_ref, len_ref, seg_copy)

    lpos = lpos_ref[...]
    gates = gate_ref[...]
    cols = lax.broadcasted_iota(jnp.int32, (ts, lrows), 1)
    sel = jnp.zeros((ts, lrows), F32)
    for k in range(TOP_K):
        sel = jnp.where(cols == lpos[:, k:k + 1], gates[:, k:k + 1], sel)
    sel = sel.astype(BF16)

    _segment_wait(i, n_exp, lrows, src_ref, len_ref, seg_copy)
    acc = _dot(sel, ycomp_ref[...].astype(BF16))
    out = x_ref[...] + gt_ref[0] * acc
    if final_norm:
        ms = jnp.mean(out * out, axis=-1, keepdims=True)
        out = out * lax.rsqrt(ms + EPS) * fg_ref[...]
    o_ref[...] = out


def _combine(seg_src, seg_dst, seg_len, y_buf, lpos_t, gates_t, x2d, gt, final_g, s, final_norm,
             ts, lrows, n_exp):
    t, d = x2d.shape
    per_b = s // ts
    tile_map = lambda i, *_: (i, 0)
    return pl.pallas_call(
        functools.partial(_combine_kernel, ts=ts, lrows=lrows, n_exp=n_exp, final_norm=final_norm),
        out_shape=jax.ShapeDtypeStruct((t, d), F32),
        grid_spec=pltpu.PrefetchScalarGridSpec(
            num_scalar_prefetch=3,
            grid=(t // ts,),
            in_specs=[
                pl.BlockSpec(memory_space=pl.ANY),
                pl.BlockSpec((ts, TOP_K), tile_map),
                pl.BlockSpec((ts, TOP_K), tile_map),
                pl.BlockSpec((ts, d), tile_map),
                pl.BlockSpec((1, 1, d), lambda i, *_: (i // per_b, 0, 0)),
                pl.BlockSpec((1, d), lambda i, *_: (0, 0)),
            ],
            out_specs=pl.BlockSpec((ts, d), tile_map),
            scratch_shapes=[
                pltpu.VMEM((lrows, d), F32),
                pltpu.SemaphoreType.DMA,
            ],
        ),
        compiler_params=_cparams(("arbitrary",)),
        name="combine",
    )(seg_src, seg_dst, seg_len, y_buf, lpos_t, gates_t, x2d, gt, final_g.reshape(1, d))


def _moe_layer(x, sh, sc, gt, norm_g, router_w, router_b, w1, b1, w2, b2, layer, final_g, final_norm):
    bsz, s, d = x.shape
    t = bsz * s
    n_exp = router_w.shape[1]
    tm = EXPERT_ROWS
    ts = min(MOE_TILE, s)
    n_tiles = t // ts
    lrows = -(-(ts * TOP_K + n_exp * (SEG_ALIGN - 1)) // tm) * tm
    h, lpos, gates, tab, tot = _route(x, sh, sc, norm_g, router_w, router_b, ts)

    rows = tot[:, 0].astype(jnp.int32)
    padded = (rows + tm - 1) // tm * tm
    pends = jnp.cumsum(padded).astype(jnp.int32)
    pstart = pends - padded
    n_blocks = -(-(t * TOP_K + n_tiles * n_exp * (SEG_ALIGN - 1) + n_exp * (tm - 1)) // tm)
    n_used = pends[-1] // tm
    blk = jnp.minimum(jnp.arange(n_blocks, dtype=jnp.int32), n_used - 1)
    block_expert = jnp.minimum(
        jnp.sum((pends[None, :] <= (blk * tm)[:, None]).astype(jnp.int32), axis=1), n_exp - 1)
    tab = tab[:, :, :3].astype(jnp.int32)
    seg_src = tab[:, :, 0].reshape(-1)
    seg_dst = (pstart[None, :] + tab[:, :, 1]).reshape(-1)
    seg_len = tab[:, :, 2].reshape(-1)

    buf = _dispatch(pstart, pends, seg_src, seg_dst, seg_len, lpos, h.reshape(t, d),
                    n_blocks * tm, ts, lrows)
    y_buf = _experts(block_expert, n_used.reshape(1), buf, w1, b1, w2, b2, layer)
    out = _combine(seg_src, seg_dst, seg_len, y_buf, lpos.T, gates.T, x.reshape(t, d), gt, final_g,
                   s, final_norm, ts, lrows, n_exp)
    return out.reshape(bsz, s, d)


def _head_rms(x, seg, seg_t, inv_dh):
    hi, lo = _split2(x * x)
    ssum = _dot(hi, seg) + _dot(lo, seg)
    inv = lax.rsqrt(ssum * inv_dh + EPS)
    ihi, ilo = _split2(inv)
    return _dot(ihi, seg_t) + _dot(ilo, seg_t)


def _kvq_kernel(x_ref, ksh_ref, ksc_ref, msh_ref, msc_ref, kg_ref, mg_ref,
                wk_ref, wvt_ref, wf_ref, fb_ref, kng_ref, qng_ref, wq_ref, wg_ref,
                place_ref, cst_ref,
                k_ref, vt_ref, q_ref, sg_ref, ccol_ref, *, ts, d):
    @pl.when(pl.program_id(1) == 0)
    def _():
        ccol_ref[...] = jnp.zeros_like(ccol_ref)

    x = x_ref[0]
    ms = jnp.mean(x * x, axis=-1, keepdims=True)
    xn = x * lax.rsqrt(ms + EPS)
    hk = ((xn * kg_ref[...]) * (1.0 + ksc_ref[0]) + ksh_ref[0]).astype(BF16)
    hq = ((xn * mg_ref[...]) * (1.0 + msc_ref[0]) + msh_ref[0]).astype(BF16)

    di = lax.broadcasted_iota(jnp.int32, (d, LANES), 0) // HEAD_DIM
    hi_ = lax.broadcasted_iota(jnp.int32, (d, LANES), 1)
    seg = jnp.where(di == hi_, 1.0, 0.0).astype(BF16)
    dt = lax.broadcasted_iota(jnp.int32, (LANES, d), 1) // HEAD_DIM
    ht = lax.broadcasted_iota(jnp.int32, (LANES, d), 0)
    seg_t = jnp.where(dt == ht, 1.0, 0.0).astype(BF16)
    inv_dh = 1.0 / HEAD_DIM

    k = _dot(hk, wk_ref[...])
    kn = (k * _head_rms(k, seg, seg_t, inv_dh) * kng_ref[...]).astype(BF16)
    vt_ref[0] = _dot_nt(wvt_ref[...], hk).astype(BF16)

    q = _dot(hq, wq_ref[...])
    qn = q * _head_rms(q, seg, seg_t, inv_dh) * qng_ref[...]
    qn = (qn * (HEAD_DIM ** -0.5 * LOG2E)).astype(BF16)
    sg_ref[0] = jax.nn.sigmoid(_dot(hq, wg_ref[...]))

    lf = jax.nn.log_sigmoid(_dot(hk, wf_ref[...]) + fb_ref[...])
    r = lax.broadcasted_iota(jnp.int32, (ts, ts), 0)
    c = lax.broadcasted_iota(jnp.int32, (ts, ts), 1)
    lower = jnp.where(c <= r, 1.0, 0.0).astype(BF16)
    a, b, e = _split3(lf)
    cum = ccol_ref[...] + (_dot(lower, a) + _dot(lower, b) + _dot(lower, e))
    ccol_ref[...] = cum[ts - 1:ts, :]
    a, b, e = _split3(cum * LOG2E)
    extra = (_dot(a, place_ref[0]) + _dot(b, place_ref[1]) + _dot(e, place_ref[2])
             + cst_ref[...]).astype(BF16)

    for j in range(d // LANES):
        src = slice(j * LANES, (j + 1) * LANES)
        k_ref[0, :, 2 * j * LANES:(2 * j + 1) * LANES] = kn[:, src]
        k_ref[0, :, (2 * j + 1) * LANES:(2 * j + 2) * LANES] = extra[:, src]
        q_ref[0, :, 2 * j * LANES:(2 * j + 1) * LANES] = qn[:, src]
        q_ref[0, :, (2 * j + 1) * LANES:(2 * j + 2) * LANES] = extra[:, d + j * LANES:d + (j + 1) * LANES]


def _bias_lane_tables(d):
    n_heads = d // HEAD_DIM
    place =np.zeros((N_PARTS, LANES, 2 * d), np.float32)
    cst = np.zeros((1, 2 * d), np.float32)
    for h in range(n_heads):
        j, hh = divmod(h, 2)
        for part in range(N_PARTS):
            place[part, h, j * LANES + N_PARTS * hh + part] = 1.0
            place[part, h, d + j * LANES + 2 * N_PARTS + N_PARTS * hh + part] = 1.0
    for j in range(d // LANES):
        cst[0, j * LANES + 2 * N_PARTS:j * LANES + 4 * N_PARTS] = 1.0
        cst[0, d + j * LANES:d + j * LANES + 2 * N_PARTS] = -1.0
    return jnp.asarray(place, BF16), jnp.asarray(cst, F32)


def _kvq(x, ksh, ksc, msh, msc, kv_norm_g, mix_norm_g, kv_w, k_norm_g, fgate_b,
         w_qg, q_norm_g, ts=256):
    bsz, s, d = x.shape
    n_heads = d // HEAD_DIM
    ts = min(ts, s)
    wk = kv_w[:, :d].astype(BF16)
    wvt = kv_w[:, d:2 * d].T.astype(BF16)
    wf = kv_w[:, 2 * d:]
    wf_pad = jnp.pad(wf, ((0, 0), (0, LANES - n_heads))).astype(BF16)
    fb_pad = jnp.pad(fgate_b, (0, LANES - n_heads)).reshape(1, LANES)
    place, cst = _bias_lane_tables(d)
    vec = pl.BlockSpec((1, 1, d), lambda b, i: (b, 0, 0))
    c2 = lambda shape: pl.BlockSpec(shape, lambda b, i: (0, 0))
    tile = pl.BlockSpec((1, ts, d), lambda b, i: (b, i, 0))
    tile2 = pl.BlockSpec((1, ts, 2 * d), lambda b, i: (b, i, 0))
    return pl.pallas_call(
        functools.partial(_kvq_kernel, ts=ts, d=d),
        out_shape=(
            jax.ShapeDtypeStruct((bsz, s, 2 * d), BF16),
            jax.ShapeDtypeStruct((bsz, d, s), BF16),
            jax.ShapeDtypeStruct((bsz, s, 2 * d), BF16),
            jax.ShapeDtypeStruct((bsz, s, d), F32),
        ),
        grid=(bsz, s // ts),
        in_specs=[
            tile, vec, vec, vec, vec,
            c2((1, d)), c2((1, d)),
            c2((d, d)), c2((d, d)), c2((d, LANES)), c2((1, LANES)),
            c2((1, d)), c2((1, d)),
            c2((d, d)), c2((d, d)),
            pl.BlockSpec((N_PARTS, LANES, 2 * d), lambda b, i: (0, 0, 0)),
            c2((1, 2 * d)),
        ],
        out_specs=(
            tile2,
            pl.BlockSpec((1, d, ts), lambda b, i: (b, 0, i)),
            tile2,
            tile,
        ),
        scratch_shapes=[pltpu.VMEM((1, LANES), F32)],
        compiler_params=_cparams(("arbitrary", "arbitrary")),
        name="kvq",
    )(x, ksh, ksc, msh, msc, kv_norm_g.reshape(1, d), mix_norm_g.reshape(1, d),
      wk, wvt, wf_pad, fb_pad,
      jnp.tile(k_norm_g, n_heads).reshape(1, d), jnp.tile(q_norm_g, n_heads).reshape(1, d),
      w_qg[:, :d].astype(BF16), w_qg[:, d:].astype(BF16), place, cst)


def _attn_kernel(q_ref, k_ref, vt_ref, o_ref, acc_ref, *, tq, tk):
    qi = pl.program_id(2)
    q = q_ref[0]
    lane = lax.broadcasted_iota(jnp.int32, (tq, 2 * LANES), 1)
    bias_lane = lane - LANES
    zero = jnp.zeros_like(q)
    q_heads = []
    for hh in range(2):
        feat = jnp.logical_and(lane >= hh * HEAD_DIM, lane < (hh + 1) * HEAD_DIM)
        minus = jnp.logical_and(bias_lane >= N_PARTS * hh, bias_lane < N_PARTS * (hh + 1))
        plus = jnp.logical_and(bias_lane >= N_PARTS * (2 + hh), bias_lane < N_PARTS * (3 + hh))
        q_heads.append(jnp.where(feat | minus | plus, q, zero))
    acc_ref[...] = jnp.zeros_like(acc_ref)

    def scores(kb):
        k0 = pl.multiple_of(kb * tk, tk)
        ks = k_ref[0, pl.ds(k0, tk), :]
        return tuple(_dot_nt(ks, q_heads[hh]) for hh in range(2))

    def softmax(sts, stats, masked):
        if masked:
            keep = (lax.broadcasted_iota(jnp.int32, (tk, tq), 0)
                    <= lax.broadcasted_iota(jnp.int32, (tk, tq), 1))
        ps, alphas, out = [], [], []
        for hh in range(2):
            m_prev, l_prev = stats[hh]
            st = sts[hh]
            if masked:
                st = jnp.where(keep, st, NEG_INF)
            m_new = jnp.maximum(m_prev, jnp.max(st, axis=0, keepdims=True))
            p = jnp.exp2(st - m_new)
            alpha = jnp.exp2(m_prev - m_new)
            out.append((m_new, alpha * l_prev + jnp.sum(p, axis=0, keepdims=True)))
            ps.append(p.astype(BF16))
            alphas.append(alpha)
        return tuple(ps), tuple(alphas), tuple(out)

    def pv_update(kb, ps, alphas):
        k0 = pl.multiple_of(kb * tk, tk)
        vt = vt_ref[0, :, pl.ds(k0, tk)]
        for hh in range(2):
            pv = _dot(vt[hh * HEAD_DIM:(hh + 1) * HEAD_DIM, :], ps[hh])
            acc_ref[hh] = alphas[hh] * acc_ref[hh] + pv

    def body(kb, carry):
        sts, stats = carry
        nxt = scores(kb + 1)
        ps, alphas, stats = softmax(sts, stats, False)
        pv_update(kb, ps, alphas)
        return nxt, stats

    init = (scores(0),
            tuple((jnp.full((1, tq), NEG_INF, F32), jnp.zeros((1, tq), F32)) for _ in range(2)))
    sts, stats = lax.fori_loop(0, qi, body, init)
    ps, alphas, stats = softmax(sts, stats, True)
    pv_update(qi, ps, alphas)
    (_, la), (_, lb) = stats
    ot = jnp.concatenate([acc_ref[0] / la, acc_ref[1] / lb], axis=0)
    o_ref[0] = ot.T


def _attention(q_aug, k_aug, vt, tq=512):
    bsz, s, d2 = q_aug.shape
    d = d2 // 2
    n_pairs = d // LANES
    tq = min(tq, s)
    tk = tq
    return pl.pallas_call(
        functools.partial(_attn_kernel, tq=tq, tk=tk),
        out_shape=jax.ShapeDtypeStruct((bsz, s, d), F32),
        grid=(bsz, n_pairs, s // tq),
        in_specs=[
            pl.BlockSpec((1, tq, 2 * LANES), lambda b, j, i: (b, i, j)),
            pl.BlockSpec((1, s, 2 * LANES), lambda b, j, i: (b, 0, j)),
            pl.BlockSpec((1, LANES, s), lambda b, j, i: (b, j, 0)),
        ],
        out_specs=pl.BlockSpec((1, tq, LANES), lambda b, j, i: (b, i, j)),
        scratch_shapes=[pltpu.VMEM((2, HEAD_DIM, tq), F32)],
        compiler_params=_cparams(("parallel", "parallel", "arbitrary")),
        name="attn",
    )(q_aug, k_aug, vt)


def _attnout_kernel(x_ref, o_ref, sg_ref, gt_ref, wo_ref, out_ref):
    y = _dot((o_ref[0] * sg_ref[0]).astype(BF16), wo_ref[...])
    out_ref[0] = x_ref[0] + gt_ref[0] * y


def _attnout(x, o, sg, gt, w_o, ts=512):
    bsz, s, d = x.shape
    ts = min(ts, s)
    tile = pl.BlockSpec((1, ts, d), lambda b, i: (b, i, 0))
    return pl.pallas_call(
        _attnout_kernel,
        out_shape=jax.ShapeDtypeStruct((bsz, s, d), F32),
        grid=(bsz, s // ts),
        in_specs=[tile, tile, tile,
                  pl.BlockSpec((1, 1, d), lambda b, i: (b, 0, 0)),
                  pl.BlockSpec((d, d), lambda b, i: (0, 0))],
        out_specs=tile,
        compiler_params=_cparams(("parallel", "parallel")),
        name="attnout",
    )(x, o, sg, gt, w_o.astype(BF16))


def _chunks(mod, n):
    return [m[:, None, :] for m in jnp.split(mod, n, axis=-1)]


def kernel(x, c, ada_w, ada_b, norm_mix_g, norm_ffn_g, gmlp_w_in, gmlp_v_g, gmlp_v_b, gmlp_w_s,
           gmlp_b_s, gmlp_w_out, kv_ada_w, kv_ada_b, kv_norm_g, kv_w, k_norm_g, fgate_b, fox_w_qg,
           q_norm_g, fox_w_o, router_w, router_b, exp_w1, exp_b1, exp_w2, exp_b2, final_g):
    mods = _ada(c, ada_w, ada_b)
    kv_mod = _ada(c, kv_ada_w[None], kv_ada_b[None])[0]

    sh_m, sc_m, gt_m, sh_f, sc_f, gt_f = _chunks(mods[0], 6)
    x = _gmlp(x, sh_m, sc_m, gt_m, norm_mix_g[0], gmlp_w_in[0], gmlp_v_g[0], gmlp_v_b[0],
              gmlp_w_s[0], gmlp_b_s[0], gmlp_w_out[0])
    x = _moe_layer(x, sh_f, sc_f, gt_f, norm_ffn_g[0], router_w[0], router_b[0],
                   exp_w1, exp_b1, exp_w2, exp_b2, 0, final_g, False)

    sh_m, sc_m, gt_m, sh_f, sc_f, gt_f = _chunks(mods[1], 6)
    ksh, ksc = _chunks(kv_mod, 2)
    k_aug, vt, q_aug, sg = _kvq(x, ksh, ksc, sh_m, sc_m, kv_norm_g, norm_mix_g[1], kv_w,
                                k_norm_g, fgate_b, fox_w_qg[0], q_norm_g[0])
    o = _attention(q_aug, k_aug, vt)
    x = _attnout(x, o, sg, gt_m, fox_w_o[0])
    return _moe_layer(x, sh_f, sc_f, gt_f, norm_ffn_g[1], router_w[1], router_b[1],
                      exp_w1, exp_b1, exp_w2, exp_b2, 1, final_g, True)
```

```python
import functools

import numpy as np
import jax
import jax.numpy as jnp
from jax import lax
from jax.experimental import pallas as pl
from jax.experimental.pallas import tpu as pltpu

F32 = jnp.float32
BF16 = jnp.bfloat16

CHUNK = 128
GMLP_GROUPS = 8
HEAD_DIM = 64
TOP_K = 4
SWIGLU_LIMIT = 7.0
SWIGLU_ALPHA = 1.702
EPS = 1e-6
NEG_INF = -1e30
LOG2E = 1.4426950408889634
N_PARTS = 3
DENOM_ROWS = 16

LANES = 128
EXPERT_ROWS = 256
SEG_ALIGN = 8
MOE_TILE = 512
VMEM_LIMIT = 56 * 1024 * 1024


def _cparams(sem, vmem=VMEM_LIMIT):
    return pltpu.CompilerParams(dimension_semantics=sem, vmem_limit_bytes=vmem)


def _divisor_tile(n, pref):
    if n <= pref:
        return n
    return max(t for t in range(LANES, pref + 1, LANES) if n % t == 0)


def _dot(a, b):
    return jnp.dot(a, b, preferred_element_type=F32)


def _dot_nt(a, b):
    return lax.dot_general(a, b, (((1,), (1,)), ((), ())), preferred_element_type=F32)


def _split3(x):
    hi = x.astype(BF16)
    r = x - hi.astype(F32)
    mid = r.astype(BF16)
    lo = (r - mid.astype(F32)).astype(BF16)
    return hi, mid, lo


def _split2(x):
    hi = x.astype(BF16)
    lo = (x - hi.astype(F32)).astype(BF16)
    return hi, lo


def _gelu_exact(x):
    return 0.5 * x * (1.0 + lax.erf(x * (0.5 ** 0.5)))


def _ada_norm(x, g, shift, scale):
    ms = jnp.mean(x * x, axis=-1, keepdims=True)
    y = x * lax.rsqrt(ms + EPS) * g
    return y * (1.0 + scale) + shift


def _ada_kernel(c_ref, w_ref, b_ref, o_ref):
    c = c_ref[...]
    sc = c * jax.nn.sigmoid(c)
    o_ref[0] = _dot(sc.astype(BF16), w_ref[0].astype(BF16)) + b_ref[0]


def _ada(c, w, b):
    nl, d, n = w.shape
    bsz = c.shape[0]
    tn = _divisor_tile(n, 2048)
    return pl.pallas_call(
        _ada_kernel,
        out_shape=jax.ShapeDtypeStruct((nl, bsz, n), F32),
        grid=(nl, n // tn),
        in_specs=[
            pl.BlockSpec((bsz, d), lambda l, j: (0, 0)),
            pl.BlockSpec((1, d, tn), lambda l, j: (l, 0, j)),
            pl.BlockSpec((1, 1, tn), lambda l, j: (l, 0, j)),
        ],
        out_specs=pl.BlockSpec((1, bsz, tn), lambda l, j: (l, 0, j)),
        compiler_params=_cparams(("arbitrary", "arbitrary")),
        name="ada",
    )(c, w, b.reshape(nl, 1, n))


def _gmlp_kernel(x_ref, sh_ref, sc_ref, gt_ref, g_ref, win_ref, vg_ref, vb_ref,
                 ws_ref, bst_ref, wout_ref, o_ref, gated_ref, *, ts, width):
    x = x_ref[0]
    h = _ada_norm(x, g_ref[...], sh_ref[0], sc_ref[0]).astype(BF16)
    z = _gelu_exact(_dot(h, win_ref[...]))
    u = z[:, :width]
    v = z[:, width:]
    mu = jnp.mean(v, axis=-1, keepdims=True)
    vc = v - mu
    var = jnp.mean(vc * vc, axis=-1, keepdims=True)
    vn = (vc * lax.rsqrt(var + EPS) * vg_ref[...] + vb_ref[...]).astype(BF16)

    gd = width // GMLP_GROUPS
    row = lax.broadcasted_iota(jnp.int32, (CHUNK, CHUNK), 0)
    col = lax.broadcasted_iota(jnp.int32, (CHUNK, CHUNK), 1)
    causal = col <= row
    bst = bst_ref[...]
    for g in range(GMLP_GROUPS):
        wc = jnp.where(causal, ws_ref[g], 0.0).astype(BF16)
        bias = bst[:, g:g + 1]
        for ci in range(ts // CHUNK):
            rs = slice(ci * CHUNK, (ci + 1) * CHUNK)
            cs = slice(g * gd, (g + 1) * gd)
            mixed = _dot(wc, vn[rs, cs]) + bias
            gated_ref[rs, cs] = (u[rs, cs] * mixed).astype(BF16)
    y = _dot(gated_ref[...], wout_ref[...])
    o_ref[0] = x + gt_ref[0] * y


def _gmlp(x, sh, sc, gt, g, w_in, v_g, v_b, w_s, b_s, w_out, ts=256):
    bsz, s, d = x.shape
    width = w_out.shape[0]
    ts = min(ts, s)
    vec = pl.BlockSpec((1, 1, d), lambda b, i: (b, 0, 0))
    const2 = lambda shape: pl.BlockSpec(shape, lambda b, i: (0, 0))
    return pl.pallas_call(
        functools.partial(_gmlp_kernel, ts=ts, width=width),
        out_shape=jax.ShapeDtypeStruct((bsz, s, d), F32),
        grid=(bsz, s // ts),
        in_specs=[
            pl.BlockSpec((1, ts, d), lambda b, i: (b, i, 0)),
            vec, vec, vec,
            const2((1, d)),
            const2((d, 2 * width)),
            const2((1, width)),
            const2((1, width)),
            pl.BlockSpec((GMLP_GROUPS, CHUNK, CHUNK), lambda b, i: (0, 0, 0)),
            const2((CHUNK, GMLP_GROUPS)),
            const2((width, d)),
        ],
        out_specs=pl.BlockSpec((1, ts, d), lambda b, i: (b, i, 0)),
        scratch_shapes=[pltpu.VMEM((ts, width), BF16)],
        compiler_params=_cparams(("parallel", "parallel")),
        name="gmlp",
    )(x, sh, sc, gt, g.reshape(1, d), w_in.astype(BF16), v_g.reshape(1, width),
      v_b.reshape(1, width), w_s, b_s.T, w_out.astype(BF16))


def _route_kernel(x_ref, sh_ref, sc_ref, g_ref, rwt_ref, rb_ref,
                  h_ref, lpos_ref, gate_ref, tab_ref, tot_ref, carry_ref, *, ts, n_exp):
    first = jnp.logical_and(pl.program_id(0) == 0, pl.program_id(1) == 0)

    @pl.when(first)
    def _():
        carry_ref[...] = jnp.zeros_like(carry_ref)

    h = _ada_norm(x_ref[0], g_ref[...], sh_ref[0], sc_ref[0]).astype(BF16)
    h_ref[0] = h
    logits = _dot_nt(rwt_ref[...], h) + rb_ref[...]

    eio = lax.broadcasted_iota(jnp.int32, (n_exp, ts), 0)
    work = logits
    vals, hots = [], []
    for _ in range(TOP_K):
        m = jnp.max(work, axis=0, keepdims=True)
        sel = jnp.min(jnp.where(work == m, eio, n_exp), axis=0, keepdims=True)
        hot = eio == sel
        vals.append(m)
        hots.append(hot)
        work = jnp.where(hot, -jnp.inf, work)

    exps = [jnp.exp(v - vals[0]) for v in vals]
    denom = exps[0]
    for e in exps[1:]:
        denom = denom + e
    gates = [e / denom for e in exps]

    chosen = hots[0]
    for hot in hots[1:]:
        chosen = jnp.logical_or(chosen, hot)
    chosen_f = jnp.where(chosen, 1.0, 0.0)
    r = lax.broadcasted_iota(jnp.int32, (ts, ts), 0)
    c = lax.broadcasted_iota(jnp.int32, (ts, ts), 1)
    before = jnp.where(r < c, 1.0, 0.0).astype(BF16)
    prefix = _dot(chosen_f.astype(BF16), before)
    count = jnp.sum(chosen_f, axis=1, keepdims=True)
    seg_len = jnp.floor((count + (SEG_ALIGN - 1)) * (1.0 / SEG_ALIGN)) * SEG_ALIGN
    er = lax.broadcasted_iota(jnp.int32, (n_exp, n_exp), 0)
    ec = lax.broadcasted_iota(jnp.int32, (n_exp, n_exp), 1)
    lower = jnp.where(ec < er, 1.0, 0.0).astype(BF16)
    seg_start = _dot(lower, jnp.broadcast_to(seg_len, (n_exp, LANES)).astype(BF16))[:, 0:1]
    base = seg_start + prefix
    lpos = [jnp.sum(jnp.where(hot, base, 0.0), axis=0, keepdims=True) for hot in hots]

    lpos_ref[...] = jnp.concatenate(lpos, axis=0).astype(jnp.int32)
    gate_ref[...] = jnp.concatenate(gates, axis=0)
    used = carry_ref[...]
    lane = lax.broadcasted_iota(jnp.int32, (n_exp, LANES), 1)
    tab_ref[0] = jnp.where(lane == 0, seg_start,
                           jnp.where(lane == 1, used, jnp.where(lane == 2, seg_len, 0.0)))
    carry_ref[...] = used + seg_len
    tot_ref[...] = jnp.broadcast_to(used + seg_len, tot_ref.shape)


def _route(x, sh, sc, g, router_w, router_b, ts):
    bsz, s, d = x.shape
    n_exp = router_w.shape[1]
    nt = s // ts
    t = bsz * s
    vec = pl.BlockSpec((1, 1, d), lambda b, i: (b, 0, 0))
    slot = pl.BlockSpec((TOP_K, ts), lambda b, i: (0, b * nt + i))
    return pl.pallas_call(
        functools.partial(_route_kernel, ts=ts, n_exp=n_exp),
        out_shape=(
            jax.ShapeDtypeStruct((bsz, s, d), BF16),
            jax.ShapeDtypeStruct((TOP_K, t), jnp.int32),
            jax.ShapeDtypeStruct((TOP_K, t), F32),
            jax.ShapeDtypeStruct((bsz * nt, n_exp, LANES), F32),
            jax.ShapeDtypeStruct((n_exp, LANES), F32),
        ),
        grid=(bsz, nt),
        in_specs=[
            pl.BlockSpec((1, ts, d), lambda b, i: (b, i, 0)),
            vec, vec,
            pl.BlockSpec((1, d), lambda b, i: (0, 0)),
            pl.BlockSpec((n_exp, d), lambda b, i: (0, 0)),
            pl.BlockSpec((n_exp, 1), lambda b, i: (0, 0)),
        ],
        out_specs=(
            pl.BlockSpec((1, ts, d), lambda b, i: (b, i, 0)),
            slot, slot,
            pl.BlockSpec((1, n_exp, LANES), lambda b, i: (b * nt + i, 0, 0)),
            pl.BlockSpec((n_exp, LANES), lambda b, i: (0, 0)),
        ),
        scratch_shapes=[pltpu.VMEM((n_exp, 1), F32)],
        compiler_params=_cparams(("arbitrary", "arbitrary")),
        name="route",
    )(x, sh, sc, g.reshape(1, d), router_w.T.astype(BF16), router_b.reshape(n_exp, 1))


def _wait_rows(src, dst, sem):
    pltpu.make_async_copy(src, dst, sem).wait()


def _chunk_sizes(max_rows):
    sizes = []
    size = SEG_ALIGN
    while size <= max_rows:
        sizes.append(size)
        size *= 2
    return sizes


def _for_each_chunk(length, max_rows, fn):
    for size in reversed(_chunk_sizes(max_rows)):
        shift = size.bit_length() - 1
        @pl.when(((length >> shift) & 1) == 1)
        def _():
            fn(pl.multiple_of((length >> (shift + 1)) << (shift + 1), SEG_ALIGN), size)


def _segment_copies(tile, n_exp, max_rows, src_ref, dst_ref, len_ref, make_copy):
    for e in range(n_exp):
        s = tile * n_exp + e
        src, dst = src_ref[s], dst_ref[s]
        _for_each_chunk(len_ref[s], max_rows,
                        lambda off, size: make_copy(pl.multiple_of(src + off, SEG_ALIGN),
                                                    pl.multiple_of(dst + off, SEG_ALIGN),
                                                    size).start())


def _segment_wait(tile, n_exp, lrows, src_ref, len_ref, make_copy):
    last = tile * n_exp + n_exp - 1
    total = src_ref[last] + len_ref[last]
    _for_each_chunk(total, lrows, lambda off, size: make_copy(0, 0, size).wait())


def _dispatch_kernel(pstart_ref, pend_ref, src_ref, dst_ref, len_ref, lpos_ref, h_ref, buf_hbm,
                     comp_ref, zero_ref, sem_seg, sem_zero,
                     *, ts, lrows, tm, n_exp, n_blocks, n_tiles):
    i = pl.program_id(0)

    def seg_copy(local_row, global_row, size):
        return pltpu.make_async_copy(comp_ref.at[pl.ds(local_row, size)],
                                     buf_hbm.at[pl.ds(global_row, size)], sem_seg)

    @pl.when(i == 0)
    def _():
        zero_ref[...] = jnp.zeros_like(zero_ref)
        for e in range(n_exp):
            @pl.when(pend_ref[e] > pstart_ref[e])
            def _():
                last = pl.multiple_of(pend_ref[e] - tm, tm)
                pltpu.make_async_copy(zero_ref, buf_hbm.at[pl.ds(last, tm)], sem_zero).start()
        n_used = pend_ref[n_exp - 1] // tm

        def zero_tail(b, carry):
            off = pl.multiple_of(b * tm, tm)
            pltpu.make_async_copy(zero_ref, buf_hbm.at[pl.ds(off, tm)], sem_zero).start()
            return carry

        def wait_tail(b, carry):
            _wait_rows(zero_ref, buf_hbm.at[pl.ds(0, tm)], sem_zero)
            return carry

        lax.fori_loop(n_used, n_blocks, zero_tail, 0)
        for e in range(n_exp):
            @pl.when(pend_ref[e] > pstart_ref[e])
            def _():
                _wait_rows(zero_ref, buf_hbm.at[pl.ds(0, tm)], sem_zero)
        lax.fori_loop(n_used, n_blocks, wait_tail, 0)

    lpos = lpos_ref[...]
    rows = lax.broadcasted_iota(jnp.int32, (lrows, ts), 0)
    hit = rows == lpos[0:1, :]
    for k in range(1, TOP_K):
        hit = jnp.logical_or(hit, rows == lpos[k:k + 1, :])
    onehot = jnp.where(hit, 1.0, 0.0).astype(BF16)

    @pl.when(i > 0)
    def _():
        _segment_wait(i - 1, n_exp, lrows, src_ref, len_ref, seg_copy)

    comp_ref[...] = _dot(onehot, h_ref[...])
    _segment_copies(i, n_exp, ts, src_ref, dst_ref, len_ref, seg_copy)

    @pl.when(i == n_tiles - 1)
    def _():
        _segment_wait(i, n_exp, lrows, src_ref, len_ref, seg_copy)


def _dispatch(pstart, pends, seg_src, seg_dst, seg_len, lpos, h, n_rows, ts, lrows):
    t, d = h.shape
    tm = EXPERT_ROWS
    n_tiles = t // ts
    tile_map = lambda i, *_: (i, 0)
    return pl.pallas_call(
        functools.partial(_dispatch_kernel, ts=ts, lrows=lrows, tm=tm, n_exp=pstart.shape[0],
                          n_blocks=n_rows // tm, n_tiles=n_tiles),
        out_shape=jax.ShapeDtypeStruct((n_rows, d), F32),
        grid_spec=pltpu.PrefetchScalarGridSpec(
            num_scalar_prefetch=5,
            grid=(n_tiles,),
            in_specs=[
                pl.BlockSpec((TOP_K, ts), lambda i, *_: (0, i)),
                pl.BlockSpec((ts, d), tile_map),
            ],
            out_specs=pl.BlockSpec(memory_space=pl.ANY),
            scratch_shapes=[
                pltpu.VMEM((lrows, d), F32),
                pltpu.VMEM((tm, d), F32),
                pltpu.SemaphoreType.DMA,
                pltpu.SemaphoreType.DMA,
            ],
        ),
        compiler_params=_cparams(("arbitrary",)),
        name="dispatch",
    )(pstart, pends, seg_src, seg_dst, seg_len, lpos, h)


def _expert_kernel(be_ref, nu_ref, x_ref, w1_ref, b1_ref, w2_ref, b2_ref, y_ref,
                   w1b_ref, w2b_ref, *, d_exp):
    i = pl.program_id(0)
    changed = jnp.logical_or(i == 0, be_ref[i] != be_ref[jnp.maximum(i - 1, 0)])

    @pl.when(changed)
    def _():
        w1b_ref[...] = w1_ref[0, 0].astype(BF16)
        w2b_ref[...] = w2_ref[0, 0].astype(BF16)

    @pl.when(i < nu_ref[0])
    def _():
        gu = _dot(x_ref[...].astype(BF16), w1b_ref[...]) + b1_ref[0, 0]
        glu = jnp.minimum(gu[:, :d_exp], SWIGLU_LIMIT)
        lin = jnp.clip(gu[:, d_exp:], -SWIGLU_LIMIT, SWIGLU_LIMIT)
        act = (lin + 1.0) * (glu * jax.nn.sigmoid(SWIGLU_ALPHA * glu))
        y_ref[...] = _dot(act.astype(BF16), w2b_ref[...]) + b2_ref[0, 0]

    @pl.when(i >= nu_ref[0])
    def _():
        y_ref[...] = jnp.zeros_like(y_ref)


def _experts(block_expert, n_used, buf, w1, b1, w2, b2, layer):
    n_rows, d = buf.shape
    _, n_exp, _, d_exp2 = w1.shape
    d_exp = d_exp2 // 2
    tm = EXPERT_ROWS
    n_blocks = n_rows // tm

    def in_map(i, be, nu):
        return (jnp.minimum(i, nu[0] - 1), 0)

    def out_map(i, be, nu):
        return (i, 0)

    def w_map(i, be, nu):
        return (layer, be[i], 0, 0)

    return pl.pallas_call(
        functools.partial(_expert_kernel, d_exp=d_exp),
        out_shape=jax.ShapeDtypeStruct((n_rows, d), F32),
        grid_spec=pltpu.PrefetchScalarGridSpec(
            num_scalar_prefetch=2,
            grid=(n_blocks,),
            in_specs=[
                pl.BlockSpec((tm, d), in_map),
                pl.BlockSpec((1, 1, d, d_exp2), w_map),
                pl.BlockSpec((1, 1, 1, d_exp2), w_map),
                pl.BlockSpec((1, 1, d_exp, d), w_map),
                pl.BlockSpec((1, 1, 1, d), w_map),
            ],
            out_specs=pl.BlockSpec((tm, d), out_map),
            scratch_shapes=[pltpu.VMEM((d, d_exp2), BF16), pltpu.VMEM((d_exp, d), BF16)],
        ),
        compiler_params=_cparams(("arbitrary",)),
        name="experts",
    )(block_expert, n_used, buf, w1, b1.reshape(b1.shape[0], n_exp, 1, d_exp2),
      w2, b2.reshape(b2.shape[0], n_exp, 1, d))


def _combine_kernel(src_ref, dst_ref, len_ref, y_hbm, lpos_ref, gate_ref, x_ref, gt_ref, fg_ref,
                    o_ref, ycomp_ref, sem_seg, *, ts, lrows, n_exp, final_norm):
    i = pl.program_id(0)

    def seg_copy(local_row, global_row, size):
        return pltpu.make_async_copy(y_hbm.at[pl.ds(global_row, size)],
                                     ycomp_ref.at[pl.ds(local_row, size)], sem_seg)

    @pl.when(i == 0)
    def _():
        ycomp_ref[...] = jnp.zeros_like(ycomp_ref)

    _segment_copies(i, n_exp, ts, src_ref, dst_ref, len_ref, seg_copy)

    lpos = lpos_ref[...]
    gates = gate_ref[...]
    cols = lax.broadcasted_iota(jnp.int32, (ts, lrows), 1)
    sel = jnp.zeros((ts, lrows), F32)
    for k in range(TOP_K):
        sel = jnp.where(cols == lpos[:, k:k + 1], gates[:, k:k + 1], sel)
    sel = sel.astype(BF16)

    _segment_wait(i, n_exp, lrows, src_ref, len_ref, seg_copy)
    acc = _dot(sel, ycomp_ref[...].astype(BF16))
    out = x_ref[...] + gt_ref[0] * acc
    if final_norm:
        ms = jnp.mean(out * out, axis=-1, keepdims=True)
        out = out * lax.rsqrt(ms + EPS) * fg_ref[...]
    o_ref[...] = out


def _combine(seg_src, seg_dst, seg_len, y_buf, lpos_t, gates_t, x2d, gt, final_g, s, final_norm,
             ts, lrows, n_exp):
    t, d = x2d.shape
    per_b = s // ts
    tile_map = lambda i, *_: (i, 0)
    return pl.pallas_call(
        functools.partial(_combine_kernel, ts=ts, lrows=lrows, n_exp=n_exp, final_norm=final_norm),
        out_shape=jax.ShapeDtypeStruct((t, d), F32),
        grid_spec=pltpu.PrefetchScalarGridSpec(
            num_scalar_prefetch=3,
            grid=(t // ts,),
            in_specs=[
                pl.BlockSpec(memory_space=pl.ANY),
                pl.BlockSpec((ts, TOP_K), tile_map),
                pl.BlockSpec((ts, TOP_K), tile_map),
                pl.BlockSpec((ts, d), tile_map),
                pl.BlockSpec((1, 1, d), lambda i, *_: (i // per_b, 0, 0)),
                pl.BlockSpec((1, d), lambda i, *_: (0, 0)),
            ],
            out_specs=pl.BlockSpec((ts, d), tile_map),
            scratch_shapes=[
                pltpu.VMEM((lrows, d), F32),
                pltpu.SemaphoreType.DMA,
            ],
        ),
        compiler_params=_cparams(("arbitrary",)),
        name="combine",
    )(seg_src, seg_dst, seg_len, y_buf, lpos_t, gates_t, x2d, gt, final_g.reshape(1, d))


def _moe_layer(x, sh, sc, gt, norm_g, router_w, router_b, w1, b1, w2, b2, layer, final_g, final_norm):
    bsz, s, d = x.shape
    t = bsz * s
    n_exp = router_w.shape[1]
    tm = EXPERT_ROWS
    ts = min(MOE_TILE, s)
    n_tiles = t // ts
    lrows = -(-(ts * TOP_K + n_exp * (SEG_ALIGN - 1)) // tm) * tm
    h, lpos, gates, tab, tot = _route(x, sh, sc, norm_g, router_w, router_b, ts)

    rows = tot[:, 0].astype(jnp.int32)
    padded = (rows + tm - 1) // tm * tm
    pends = jnp.cumsum(padded).astype(jnp.int32)
    pstart = pends - padded
    n_blocks = -(-(t * TOP_K + n_tiles * n_exp * (SEG_ALIGN - 1) + n_exp * (tm - 1)) // tm)
    n_used = pends[-1] // tm
    blk = jnp.minimum(jnp.arange(n_blocks, dtype=jnp.int32), n_used - 1)
    block_expert = jnp.minimum(
        jnp.sum((pends[None, :] <= (blk * tm)[:, None]).astype(jnp.int32), axis=1), n_exp - 1)
    tab = tab[:, :, :3].astype(jnp.int32)
    seg_src = tab[:, :, 0].reshape(-1)
    seg_dst = (pstart[None, :] + tab[:, :, 1]).reshape(-1)
    seg_len = tab[:, :, 2].reshape(-1)

    buf = _dispatch(pstart, pends, seg_src, seg_dst, seg_len, lpos, h.reshape(t, d),
                    n_blocks * tm, ts, lrows)
    y_buf = _experts(block_expert, n_used.reshape(1), buf, w1, b1, w2, b2, layer)
    out = _combine(seg_src, seg_dst, seg_len, y_buf, lpos.T, gates.T, x.reshape(t, d), gt, final_g,
                   s, final_norm, ts, lrows, n_exp)
    return out.reshape(bsz, s, d)


def _head_rms(x, seg, seg_t, inv_dh):
    hi, lo = _split2(x * x)
    ssum = _dot(hi, seg) + _dot(lo, seg)
    inv = lax.rsqrt(ssum * inv_dh + EPS)
    ihi, ilo = _split2(inv)
    return _dot(ihi, seg_t) + _dot(ilo, seg_t)


def _kvq_kernel(x_ref, ksh_ref, ksc_ref, msh_ref, msc_ref, kg_ref, mg_ref,
                wk_ref, wvt_ref, wf_ref, fb_ref, kng_ref, qng_ref, wq_ref, wg_ref,
                place_ref, cst_ref,
                k_ref, kx_ref, vt_ref, q_ref, qx_ref, sg_ref, ccol_ref, *, ts, d):
    @pl.when(pl.program_id(1) == 0)
    def _():
        ccol_ref[...] = jnp.zeros_like(ccol_ref)

    x = x_ref[0]
    ms = jnp.mean(x * x, axis=-1, keepdims=True)
    xn = x * lax.rsqrt(ms + EPS)
    hk = ((xn * kg_ref[...]) * (1.0 + ksc_ref[0]) + ksh_ref[0]).astype(BF16)
    hq = ((xn * mg_ref[...]) * (1.0 + msc_ref[0]) + msh_ref[0]).astype(BF16)

    di = lax.broadcasted_iota(jnp.int32, (d, LANES), 0) // HEAD_DIM
    hi_ = lax.broadcasted_iota(jnp.int32, (d, LANES), 1)
    seg = jnp.where(di == hi_, 1.0, 0.0).astype(BF16)
    dt = lax.broadcasted_iota(jnp.int32, (LANES, d), 1) // HEAD_DIM
    ht = lax.broadcasted_iota(jnp.int32, (LANES, d), 0)
    seg_t = jnp.where(dt == ht, 1.0, 0.0).astype(BF16)
    inv_dh = 1.0 / HEAD_DIM

    k = _dot(hk, wk_ref[...])
    k_ref[0] = (k * _head_rms(k, seg, seg_t, inv_dh) * kng_ref[...]).astype(BF16)
    vt_ref[0] = _dot_nt(wvt_ref[...], hk).astype(BF16)

    q = _dot(hq, wq_ref[...])
    qn = q * _head_rms(q, seg, seg_t, inv_dh) * qng_ref[...]
    q_ref[0] = (qn * (HEAD_DIM ** -0.5 * LOG2E)).astype(BF16)
    sg_ref[0] = jax.nn.sigmoid(_dot(hq, wg_ref[...]))

    lf = jax.nn.log_sigmoid(_dot(hk, wf_ref[...]) + fb_ref[...])
    r = lax.broadcasted_iota(jnp.int32, (ts, ts), 0)
    c = lax.broadcasted_iota(jnp.int32, (ts, ts), 1)
    lower = jnp.where(c <= r, 1.0, 0.0).astype(BF16)
    a, b, e = _split3(lf)
    cum = ccol_ref[...] + (_dot(lower, a) + _dot(lower, b) + _dot(lower, e))
    ccol_ref[...] = cum[ts - 1:ts, :]
    a, b, e = _split3(cum * LOG2E)
    extra = (_dot(a, place_ref[0]) + _dot(b, place_ref[1]) + _dot(e, place_ref[2])
             + cst_ref[...]).astype(BF16)
    kx_ref[0] = extra[:, :LANES]
    qx_ref[0] = extra[:, LANES:]


def _bias_lane_tables(n_heads):
    assert 2 * N_PARTS * n_heads <= LANES
    place = np.zeros((N_PARTS, LANES, 2 * LANES), np.float32)
    cst = np.zeros((1, 2 * LANES), np.float32)
    for h in range(n_heads):
        base = 2 * N_PARTS * h
        for part in range(N_PARTS):
            place[part, h, base + part] = 1.0
            place[part, h, LANES + base + N_PARTS + part] = 1.0
        cst[0, base + N_PARTS:base + 2 * N_PARTS] = 1.0
        cst[0, LANES + base:LANES + base + N_PARTS] = -1.0
    return jnp.asarray(place, BF16), jnp.asarray(cst, F32)


def _kvq(x, ksh, ksc, msh, msc, kv_norm_g, mix_norm_g, kv_w, k_norm_g, fgate_b,
         w_qg, q_norm_g, ts=256):
    bsz, s, d = x.shape
    n_heads = d // HEAD_DIM
    ts = min(ts, s)
    wk = kv_w[:, :d].astype(BF16)
    wvt = kv_w[:, d:2 * d].T.astype(BF16)
    wf = kv_w[:, 2 * d:]
    wf_pad = jnp.pad(wf, ((0, 0), (0, LANES - n_heads))).astype(BF16)
    fb_pad = jnp.pad(fgate_b, (0, LANES - n_heads)).reshape(1, LANES)
    place, cst = _bias_lane_tables(n_heads)
    vec = pl.BlockSpec((1, 1, d), lambda b, i: (b, 0, 0))
    c2 = lambda shape: pl.BlockSpec(shape, lambda b, i: (0, 0))
    tile = pl.BlockSpec((1, ts, d), lambda b, i: (b, i, 0))
    tile_x = pl.BlockSpec((1, ts, LANES), lambda b, i: (b, i, 0))
    return pl.pallas_call(
        functools.partial(_kvq_kernel, ts=ts, d=d),
        out_shape=(
            jax.ShapeDtypeStruct((bsz, s, d), BF16),
            jax.ShapeDtypeStruct((bsz, s, LANES), BF16),
            jax.ShapeDtypeStruct((bsz, d, s), BF16),
            jax.ShapeDtypeStruct((bsz, s, d), BF16),
            jax.ShapeDtypeStruct((bsz, s, LANES), BF16),
            jax.ShapeDtypeStruct((bsz, s, d), F32),
        ),
        grid=(bsz, s // ts),
        in_specs=[
            tile, vec, vec, vec, vec,
            c2((1, d)), c2((1, d)),
            c2((d, d)), c2((d, d)), c2((d, LANES)), c2((1, LANES)),
            c2((1, d)), c2((1, d)),
            c2((d, d)), c2((d, d)),
            pl.BlockSpec((N_PARTS, LANES, 2 * LANES), lambda b, i: (0, 0, 0)),
            c2((1, 2 * LANES)),
        ],
        out_specs=(
            tile, tile_x,
            pl.BlockSpec((1, d, ts), lambda b, i: (b, 0, i)),
            tile, tile_x,
            tile,
        ),
        scratch_shapes=[pltpu.VMEM((1, LANES), F32)],
        compiler_params=_cparams(("arbitrary", "arbitrary")),
        name="kvq",
    )(x, ksh, ksc, msh, msc, kv_norm_g.reshape(1, d), mix_norm_g.reshape(1, d),
      wk, wvt, wf_pad, fb_pad,
      jnp.tile(k_norm_g, n_heads).reshape(1, d), jnp.tile(q_norm_g, n_heads).reshape(1, d),
      w_qg[:, :d].astype(BF16), w_qg[:, d:].astype(BF16), place, cst)


def _attn_kernel(q_ref, qx_ref, k_ref, kx_ref, vt_ref, o_ref, qh_ref, s0_ref, s1_ref, acc_ref,
                 *, tq, tk):
    pair = pl.program_id(1)
    qi = pl.program_id(2)
    q = jnp.concatenate([q_ref[0], qx_ref[0]], axis=1)
    lane = lax.broadcasted_iota(jnp.int32, (tq, 2 * LANES), 1)
    zero = jnp.zeros_like(q)
    for hh in range(2):
        feat = jnp.logical_and(lane >= hh * HEAD_DIM, lane < (hh + 1) * HEAD_DIM)
        first_bias = LANES + 2 * N_PARTS * (2 * pair + hh)
        bias = jnp.logical_and(lane >= first_bias, lane < first_bias + 2 * N_PARTS)
        qh_ref[hh] = jnp.where(feat | bias, q, zero)
    acc_ref[...] = jnp.zeros_like(acc_ref)
    ones = jnp.ones((DENOM_ROWS, tk), BF16)

    def scores_into(kb, s_ref):
        k0 = pl.multiple_of(kb * tk, tk)
        ks = jnp.concatenate([k_ref[0, pl.ds(k0, tk), :], kx_ref[0, pl.ds(k0, tk), :]], axis=1)
        for hh in range(2):
            s_ref[hh] = _dot_nt(ks, qh_ref[hh])

    def softmax_pv(kb, s_ref, ms, masked):
        k0 = pl.multiple_of(kb * tk, tk)
        vt = vt_ref[0, :, pl.ds(k0, tk)]
        if masked:
            keep = (lax.broadcasted_iota(jnp.int32, (tk, tq), 0)
                    <= lax.broadcasted_iota(jnp.int32, (tk, tq), 1))
        out = []
        for hh in range(2):
            st = s_ref[hh]
            if masked:
                st = jnp.where(keep, st, NEG_INF)
            m_new = jnp.maximum(ms[hh], jnp.max(st, axis=0, keepdims=True))
            p = jnp.exp2(st - m_new).astype(BF16)
            alpha = jnp.exp2(ms[hh] - m_new)
            vth = jnp.concatenate([vt[hh * HEAD_DIM:(hh + 1) * HEAD_DIM, :], ones], axis=0)
            acc_ref[hh] = alpha * acc_ref[hh] + _dot(vth, p)
            out.append(m_new)
        return tuple(out)

    def two_blocks(j, ms):
        scores_into(2 * j + 1, s1_ref)
        ms = softmax_pv(2 * j, s0_ref, ms, False)
        scores_into(2 * j + 2, s0_ref)
        return softmax_pv(2 * j + 1, s1_ref, ms, False)

    scores_into(0, s0_ref)
    ms = lax.fori_loop(0, qi // 2, two_blocks,
                       tuple(jnp.full((1, tq), NEG_INF, F32) for _ in range(2)))

    @pl.when(qi % 2 == 1)
    def _():
        scores_into(qi, s1_ref)
        softmax_pv(qi, s1_ref, softmax_pv(qi - 1, s0_ref, ms, False), True)

    @pl.when(qi % 2 == 0)
    def _():
        softmax_pv(qi, s0_ref, ms, True)

    ot = jnp.concatenate(
        [acc_ref[hh, :HEAD_DIM, :] / acc_ref[hh, HEAD_DIM:HEAD_DIM + 1, :] for hh in range(2)],
        axis=0)
    o_ref[0] = ot.T


def _attention(q, qx, k, kx, vt, tq=512):
    bsz, s, d = q.shape
    n_pairs = d // LANES
    tq = min(tq, s)
    tk = tq
    return pl.pallas_call(
        functools.partial(_attn_kernel, tq=tq, tk=tk),
        out_shape=jax.ShapeDtypeStruct((bsz, s, d), F32),
        grid=(bsz, n_pairs, s // tq),
        in_specs=[
            pl.BlockSpec((1, tq, LANES), lambda b, j, i: (b, i, j)),
            pl.BlockSpec((1, tq, LANES), lambda b, j, i: (b, i, 0)),
            pl.BlockSpec((1, s, LANES), lambda b, j, i: (b, 0, j)),
            pl.BlockSpec((1, s, LANES), lambda b, j, i: (b, 0, 0)),
            pl.BlockSpec((1, LANES, s), lambda b, j, i: (b, j, 0)),
        ],
        out_specs=pl.BlockSpec((1, tq, LANES), lambda b, j, i: (b, i, j)),
        scratch_shapes=[
            pltpu.VMEM((2, tq, 2 * LANES), BF16),
            pltpu.VMEM((2, tk, tq), F32),
            pltpu.VMEM((2, tk, tq), F32),
            pltpu.VMEM((2, HEAD_DIM + DENOM_ROWS, tq), F32),
        ],
        compiler_params=_cparams(("parallel", "parallel", "arbitrary")),
        name="attn",
    )(q, qx, k, kx, vt)


def _attnout_kernel(x_ref, o_ref, sg_ref, gt_ref, wo_ref, out_ref):
    y = _dot((o_ref[0] * sg_ref[0]).astype(BF16), wo_ref[...])
    out_ref[0] = x_ref[0] + gt_ref[0] * y


def _attnout(x, o, sg, gt, w_o, ts=512):
    bsz, s, d = x.shape
    ts = min(ts, s)
    tile = pl.BlockSpec((1, ts, d), lambda b, i: (b, i, 0))
    return pl.pallas_call(
        _attnout_kernel,
        out_shape=jax.ShapeDtypeStruct((bsz, s, d), F32),
        grid=(bsz, s // ts),
        in_specs=[tile, tile, tile,
                  pl.BlockSpec((1, 1, d), lambda b, i: (b, 0, 0)),
                  pl.BlockSpec((d, d), lambda b, i: (0, 0))],
        out_specs=tile,
        compiler_params=_cparams(("parallel", "parallel")),
        name="attnout",
    )(x, o, sg, gt, w_o.astype(BF16))


def _chunks(mod, n):
    return [m[:, None, :] for m in jnp.split(mod, n, axis=-1)]


def kernel(x, c, ada_w, ada_b, norm_mix_g, norm_ffn_g, gmlp_w_in, gmlp_v_g, gmlp_v_b, gmlp_w_s,
           gmlp_b_s, gmlp_w_out, kv_ada_w, kv_ada_b, kv_norm_g, kv_w, k_norm_g, fgate_b, fox_w_qg,
           q_norm_g, fox_w_o, router_w, router_b, exp_w1, exp_b1, exp_w2, exp_b2, final_g):
    mods = _ada(c, ada_w, ada_b)
    kv_mod = _ada(c, kv_ada_w[None], kv_ada_b[None])[0]

    sh_m, sc_m, gt_m, sh_f, sc_f, gt_f = _chunks(mods[0], 6)
    x = _gmlp(x, sh_m, sc_m, gt_m, norm_mix_g[0], gmlp_w_in[0], gmlp_v_g[0], gmlp_v_b[0],
              gmlp_w_s[0], gmlp_b_s[0], gmlp_w_out[0])
    x = _moe_layer(x, sh_f, sc_f, gt_f, norm_ffn_g[0], router_w[0], router_b[0],
                   exp_w1, exp_b1, exp_w2, exp_b2, 0, final_g, False)

    sh_m, sc_m, gt_m, sh_f, sc_f, gt_f = _chunks(mods[1], 6)
    ksh, ksc = _chunks(kv_mod, 2)
    k, kx, vt, q, qx, sg = _kvq(x, ksh, ksc, sh_m, sc_m, kv_norm_g, norm_mix_g[1], kv_w,
                                k_norm_g, fgate_b, fox_w_qg[0], q_norm_g[0])
    o = _attention(q, qx, k, kx, vt)
    x = _attnout(x, o, sg, gt_m, fox_w_o[0])
    return _moe_layer(x, sh_f, sc_f, gt_f, norm_ffn_g[1], router_w[1], router_b[1],
                      exp_w1, exp_b1, exp_w2, exp_b2, 1, final_g, True)
```

```python
import functools

import numpy as np
import jax
import jax.numpy as jnp
from jax import lax
from jax.experimental import pallas as pl
from jax.experimental.pallas import tpu as pltpu

F32 = jnp.float32
BF16 = jnp.bfloat16

CHUNK = 128
GMLP_GROUPS = 8
HEAD_DIM = 64
TOP_K = 4
SWIGLU_LIMIT = 7.0
SWIGLU_ALPHA = 1.702
EPS = 1e-6
NEG_INF = -1e30
LOG2E = 1.4426950408889634
N_PARTS = 3
DENOM_ROWS = 16

LANES = 128
MXU_ROWS = 256
EXPERT_ROWS = 2 * MXU_ROWS
SEG_ALIGN = 8
MOE_TILE = 512
VMEM_LIMIT = 56 * 1024 * 1024


def _cparams(sem, vmem=VMEM_LIMIT):
    return pltpu.CompilerParams(dimension_semantics=sem, vmem_limit_bytes=vmem)


def _divisor_tile(n, pref):
    if n <= pref:
        return n
    return max(t for t in range(LANES, pref + 1, LANES) if n % t == 0)


def _dot(a, b):
    return jnp.dot(a, b, preferred_element_type=F32)


def _dot_nt(a, b):
    return lax.dot_general(a, b, (((1,), (1,)), ((), ())), preferred_element_type=F32)


def _split3(x):
    hi = x.astype(BF16)
    r = x - hi.astype(F32)
    mid = r.astype(BF16)
    lo = (r - mid.astype(F32)).astype(BF16)
    return hi, mid, lo


def _split2(x):
    hi = x.astype(BF16)
    lo = (x - hi.astype(F32)).astype(BF16)
    return hi, lo


def _gelu_exact(x):
    return 0.5 * x * (1.0 + lax.erf(x * (0.5 ** 0.5)))


def _ada_norm(x, g, shift, scale):
    ms = jnp.mean(x * x, axis=-1, keepdims=True)
    y = x * lax.rsqrt(ms + EPS) * g
    return y * (1.0 + scale) + shift


def _ada_kernel(c_ref, w_ref, b_ref, o_ref):
    c = c_ref[...]
    sc = c * jax.nn.sigmoid(c)
    o_ref[0] = _dot(sc.astype(BF16), w_ref[0].astype(BF16)) + b_ref[0]


def _ada(c, w, b):
    nl, d, n = w.shape
    bsz = c.shape[0]
    tn = _divisor_tile(n, 2048)
    return pl.pallas_call(
        _ada_kernel,
        out_shape=jax.ShapeDtypeStruct((nl, bsz, n), F32),
        grid=(nl, n // tn),
        in_specs=[
            pl.BlockSpec((bsz, d), lambda l, j: (0, 0)),
            pl.BlockSpec((1, d, tn), lambda l, j: (l, 0, j)),
            pl.BlockSpec((1, 1, tn), lambda l, j: (l, 0, j)),
        ],
        out_specs=pl.BlockSpec((1, bsz, tn), lambda l, j: (l, 0, j)),
        compiler_params=_cparams(("arbitrary", "arbitrary")),
        name="ada",
    )(c, w, b.reshape(nl, 1, n))


def _gmlp_kernel(x_ref, sh_ref, sc_ref, gt_ref, g_ref, win_ref, vg_ref, vb_ref,
                 ws_ref, bst_ref, wout_ref, o_ref, gated_ref, *, ts, width):
    x = x_ref[0]
    h = _ada_norm(x, g_ref[...], sh_ref[0], sc_ref[0]).astype(BF16)
    z = _gelu_exact(_dot(h, win_ref[...]))
    u = z[:, :width]
    v = z[:, width:]
    mu = jnp.mean(v, axis=-1, keepdims=True)
    vc = v - mu
    var = jnp.mean(vc * vc, axis=-1, keepdims=True)
    vn = (vc * lax.rsqrt(var + EPS) * vg_ref[...] + vb_ref[...]).astype(BF16)

    gd = width // GMLP_GROUPS
    row = lax.broadcasted_iota(jnp.int32, (CHUNK, CHUNK), 0)
    col = lax.broadcasted_iota(jnp.int32, (CHUNK, CHUNK), 1)
    causal = col <= row
    bst = bst_ref[...]
    for g in range(GMLP_GROUPS):
        wc = jnp.where(causal, ws_ref[g], 0.0).astype(BF16)
        bias = bst[:, g:g + 1]
        for ci in range(ts // CHUNK):
            rs = slice(ci * CHUNK, (ci + 1) * CHUNK)
            cs = slice(g * gd, (g + 1) * gd)
            mixed = _dot(wc, vn[rs, cs]) + bias
            gated_ref[rs, cs] = (u[rs, cs] * mixed).astype(BF16)
    y = _dot(gated_ref[...], wout_ref[...])
    o_ref[0] = x + gt_ref[0] * y


def _gmlp(x, sh, sc, gt, g, w_in, v_g, v_b, w_s, b_s, w_out, ts=512):
    bsz, s, d = x.shape
    width = w_out.shape[0]
    ts = min(ts, s)
    vec = pl.BlockSpec((1, 1, d), lambda b, i: (b, 0, 0))
    const2 = lambda shape: pl.BlockSpec(shape, lambda b, i: (0, 0))
    return pl.pallas_call(
        functools.partial(_gmlp_kernel, ts=ts, width=width),
        out_shape=jax.ShapeDtypeStruct((bsz, s, d), F32),
        grid=(bsz, s // ts),
        in_specs=[
            pl.BlockSpec((1, ts, d), lambda b, i: (b, i, 0)),
            vec, vec, vec,
            const2((1, d)),
            const2((d, 2 * width)),
            const2((1, width)),
            const2((1, width)),
            pl.BlockSpec((GMLP_GROUPS, CHUNK, CHUNK), lambda b, i: (0, 0, 0)),
            const2((CHUNK, GMLP_GROUPS)),
            const2((width, d)),
        ],
        out_specs=pl.BlockSpec((1, ts, d), lambda b, i: (b, i, 0)),
        scratch_shapes=[pltpu.VMEM((ts, width), BF16)],
        compiler_params=_cparams(("parallel", "parallel")),
        name="gmlp",
    )(x, sh, sc, gt, g.reshape(1, d), w_in.astype(BF16), v_g.reshape(1, width),
      v_b.reshape(1, width), w_s, b_s.T, w_out.astype(BF16))


def _route_kernel(x_ref, sh_ref, sc_ref, g_ref, rwt_ref, rb_ref,
                  h_ref, lpos_ref, gate_ref, tab_ref, tot_ref, carry_ref, *, ts, n_exp):
    first = jnp.logical_and(pl.program_id(0) == 0, pl.program_id(1) == 0)

    @pl.when(first)
    def _():
        carry_ref[...] = jnp.zeros_like(carry_ref)

    h = _ada_norm(x_ref[0], g_ref[...], sh_ref[0], sc_ref[0]).astype(BF16)
    h_ref[0] = h
    logits = _dot_nt(rwt_ref[...], h) + rb_ref[...]

    eio = lax.broadcasted_iota(jnp.int32, (n_exp, ts), 0)
    work = logits
    vals, hots = [], []
    for _ in range(TOP_K):
        m = jnp.max(work, axis=0, keepdims=True)
        sel = jnp.min(jnp.where(work == m, eio, n_exp), axis=0, keepdims=True)
        hot = eio == sel
        vals.append(m)
        hots.append(hot)
        work = jnp.where(hot, -jnp.inf, work)

    exps = [jnp.exp(v - vals[0]) for v in vals]
    denom = exps[0]
    for e in exps[1:]:
        denom = denom + e
    gates = [e / denom for e in exps]

    chosen = hots[0]
    for hot in hots[1:]:
        chosen = jnp.logical_or(chosen, hot)
    chosen_f = jnp.where(chosen, 1.0, 0.0)
    r = lax.broadcasted_iota(jnp.int32, (ts, ts), 0)
    c = lax.broadcasted_iota(jnp.int32, (ts, ts), 1)
    before = jnp.where(r < c, 1.0, 0.0).astype(BF16)
    prefix = _dot(chosen_f.astype(BF16), before)
    count = jnp.sum(chosen_f, axis=1, keepdims=True)
    seg_len = jnp.floor((count + (SEG_ALIGN - 1)) * (1.0 / SEG_ALIGN)) * SEG_ALIGN
    er = lax.broadcasted_iota(jnp.int32, (n_exp, n_exp), 0)
    ec = lax.broadcasted_iota(jnp.int32, (n_exp, n_exp), 1)
    lower = jnp.where(ec < er, 1.0, 0.0).astype(BF16)
    seg_start = _dot(lower, jnp.broadcast_to(seg_len, (n_exp, LANES)).astype(BF16))[:, 0:1]
    base = seg_start + prefix
    lpos = [jnp.sum(jnp.where(hot, base, 0.0), axis=0, keepdims=True) for hot in hots]

    lpos_ref[...] = jnp.concatenate(lpos, axis=0).astype(jnp.int32)
    gate_ref[...] = jnp.concatenate(gates, axis=0)
    used = carry_ref[...]
    lane = lax.broadcasted_iota(jnp.int32, (n_exp, LANES), 1)
    tab_ref[0] = jnp.where(lane == 0, seg_start,
                           jnp.where(lane == 1, used, jnp.where(lane == 2, seg_len, 0.0)))
    carry_ref[...] = used + seg_len
    tot_ref[...] = jnp.broadcast_to(used + seg_len, tot_ref.shape)


def _route(x, sh, sc, g, router_w, router_b, ts):
    bsz, s, d = x.shape
    n_exp = router_w.shape[1]
    nt = s // ts
    t = bsz * s
    vec = pl.BlockSpec((1, 1, d), lambda b, i: (b, 0, 0))
    slot = pl.BlockSpec((TOP_K, ts), lambda b, i: (0, b * nt + i))
    return pl.pallas_call(
        functools.partial(_route_kernel, ts=ts, n_exp=n_exp),
        out_shape=(
            jax.ShapeDtypeStruct((bsz, s, d), BF16),
            jax.ShapeDtypeStruct((TOP_K, t), jnp.int32),
            jax.ShapeDtypeStruct((TOP_K, t), F32),
            jax.ShapeDtypeStruct((bsz * nt, n_exp, LANES), F32),
            jax.ShapeDtypeStruct((n_exp, LANES), F32),
        ),
        grid=(bsz, nt),
        in_specs=[
            pl.BlockSpec((1, ts, d), lambda b, i: (b, i, 0)),
            vec, vec,
            pl.BlockSpec((1, d), lambda b, i: (0, 0)),
            pl.BlockSpec((n_exp, d), lambda b, i: (0, 0)),
            pl.BlockSpec((n_exp, 1), lambda b, i: (0, 0)),
        ],
        out_specs=(
            pl.BlockSpec((1, ts, d), lambda b, i: (b, i, 0)),
            slot, slot,
            pl.BlockSpec((1, n_exp, LANES), lambda b, i: (b * nt + i, 0, 0)),
            pl.BlockSpec((n_exp, LANES), lambda b, i: (0, 0)),
        ),
        scratch_shapes=[pltpu.VMEM((n_exp, 1), F32)],
        compiler_params=_cparams(("arbitrary", "arbitrary")),
        name="route",
    )(x, sh, sc, g.reshape(1, d), router_w.T.astype(BF16), router_b.reshape(n_exp, 1))


def _wait_rows(src, dst, sem):
    pltpu.make_async_copy(src, dst, sem).wait()


def _chunk_sizes(max_rows):
    sizes = []
    size = SEG_ALIGN
    while size <= max_rows:
        sizes.append(size)
        size *= 2
    return sizes


def _for_each_chunk(length, max_rows, fn):
    for size in reversed(_chunk_sizes(max_rows)):
        shift = size.bit_length() - 1
        @pl.when(((length >> shift) & 1) == 1)
        def _():
            fn(pl.multiple_of((length >> (shift + 1)) << (shift + 1), SEG_ALIGN), size)


def _segment_copies(tile, n_exp, max_rows, src_ref, dst_ref, len_ref, make_copy):
    for e in range(n_exp):
        s = tile * n_exp + e
        src, dst = src_ref[s], dst_ref[s]
        _for_each_chunk(len_ref[s], max_rows,
                        lambda off, size: make_copy(pl.multiple_of(src + off, SEG_ALIGN),
                                                    pl.multiple_of(dst + off, SEG_ALIGN),
                                                    size).start())


def _segment_wait(tile, n_exp, lrows, src_ref, len_ref, make_copy):
    last = tile * n_exp + n_exp - 1
    total = src_ref[last] + len_ref[last]
    _for_each_chunk(total, lrows, lambda off, size: make_copy(0, 0, size).wait())


def _dispatch_kernel(pstart_ref, pend_ref, src_ref, dst_ref, len_ref, lpos_ref, h_ref, buf_hbm,
                     comp_ref, zero_ref, sem_seg, sem_zero,
                     *, ts, lrows, tm, n_exp, n_blocks, n_tiles):
    i = pl.program_id(0)
    slot = i % 2

    def seg_copy_from(buf_slot):
        def seg_copy(local_row, global_row, size):
            return pltpu.make_async_copy(comp_ref.at[buf_slot, pl.ds(local_row, size)],
                                         buf_hbm.at[pl.ds(global_row, size)], sem_seg.at[buf_slot])
        return seg_copy

    @pl.when(i == 0)
    def _():
        zero_ref[...] = jnp.zeros_like(zero_ref)
        for e in range(n_exp):
            @pl.when(pend_ref[e] > pstart_ref[e])
            def _():
                last = pl.multiple_of(pend_ref[e] - tm, tm)
                pltpu.make_async_copy(zero_ref, buf_hbm.at[pl.ds(last, tm)], sem_zero).start()
        n_used = pend_ref[n_exp - 1] // tm

        def zero_tail(b, carry):
            off = pl.multiple_of(b * tm, tm)
            pltpu.make_async_copy(zero_ref, buf_hbm.at[pl.ds(off, tm)], sem_zero).start()
            return carry

        def wait_tail(b, carry):
            _wait_rows(zero_ref, buf_hbm.at[pl.ds(0, tm)], sem_zero)
            return carry

        lax.fori_loop(n_used, n_blocks, zero_tail, 0)
        for e in range(n_exp):
            @pl.when(pend_ref[e] > pstart_ref[e])
            def _():
                _wait_rows(zero_ref, buf_hbm.at[pl.ds(0, tm)], sem_zero)
        lax.fori_loop(n_used, n_blocks, wait_tail, 0)

    lpos = lpos_ref[...]
    rows = lax.broadcasted_iota(jnp.int32, (lrows, ts), 0)
    hit = rows == lpos[0:1, :]
    for k in range(1, TOP_K):
        hit = jnp.logical_or(hit, rows == lpos[k:k + 1, :])
    onehot = jnp.where(hit, 1.0, 0.0).astype(BF16)

    @pl.when(i > 1)
    def _():
        _segment_wait(i - 2, n_exp, lrows, src_ref, len_ref, seg_copy_from(slot))

    comp_ref[slot] = _dot(onehot, h_ref[...])
    _segment_copies(i, n_exp, ts, src_ref, dst_ref, len_ref, seg_copy_from(slot))

    @pl.when(i == n_tiles - 1)
    def _():
        @pl.when(i > 0)
        def _():
            _segment_wait(i - 1, n_exp, lrows, src_ref, len_ref, seg_copy_from(1 - slot))
        _segment_wait(i, n_exp, lrows, src_ref, len_ref, seg_copy_from(slot))


def _dispatch(pstart, pends, seg_src, seg_dst, seg_len, lpos, h, n_rows, ts, lrows):
    t, d = h.shape
    tm = EXPERT_ROWS
    n_tiles = t // ts
    tile_map = lambda i, *_: (i, 0)
    return pl.pallas_call(
        functools.partial(_dispatch_kernel, ts=ts, lrows=lrows, tm=tm, n_exp=pstart.shape[0],
                          n_blocks=n_rows // tm, n_tiles=n_tiles),
        out_shape=jax.ShapeDtypeStruct((n_rows, d), F32),
        grid_spec=pltpu.PrefetchScalarGridSpec(
            num_scalar_prefetch=5,
            grid=(n_tiles,),
            in_specs=[
                pl.BlockSpec((TOP_K, ts), lambda i, *_: (0, i)),
                pl.BlockSpec((ts, d), tile_map),
            ],
            out_specs=pl.BlockSpec(memory_space=pl.ANY),
            scratch_shapes=[
                pltpu.VMEM((2, lrows, d), F32),
                pltpu.VMEM((tm, d), F32),
                pltpu.SemaphoreType.DMA((2,)),
                pltpu.SemaphoreType.DMA,
            ],
        ),
        compiler_params=_cparams(("arbitrary",)),
        name="dispatch",
    )(pstart, pends, seg_src, seg_dst, seg_len, lpos, h)


def _expert_kernel(be_ref, nu_ref, x_ref, w1_ref, b1_ref, w2_ref, b2_ref, y_ref,
                   w1b_ref, w2b_ref, *, d_exp):
    i = pl.program_id(0)
    changed = jnp.logical_or(i == 0, be_ref[i] != be_ref[jnp.maximum(i - 1, 0)])

    @pl.when(changed)
    def _():
        w1b_ref[...] = w1_ref[0, 0].astype(BF16)
        w2b_ref[...] = w2_ref[0, 0].astype(BF16)

    @pl.when(i < nu_ref[0])
    def _():
        gu = _dot(x_ref[...].astype(BF16), w1b_ref[...]) + b1_ref[0, 0]
        glu = jnp.minimum(gu[:, :d_exp], SWIGLU_LIMIT)
        lin = jnp.clip(gu[:, d_exp:], -SWIGLU_LIMIT, SWIGLU_LIMIT)
        act = (lin + 1.0) * (glu * jax.nn.sigmoid(SWIGLU_ALPHA * glu))
        y_ref[...] = _dot(act.astype(BF16), w2b_ref[...]) + b2_ref[0, 0]

    @pl.when(i >= nu_ref[0])
    def _():
        y_ref[...] = jnp.zeros_like(y_ref)


def _experts(block_expert, n_used, buf, w1, b1, w2, b2, layer):
    n_rows, d = buf.shape
    _, n_exp, _, d_exp2 = w1.shape
    d_exp = d_exp2 // 2
    tm = EXPERT_ROWS
    n_blocks = n_rows // tm

    def in_map(i, be, nu):
        return (jnp.minimum(i, nu[0] - 1), 0)

    def out_map(i, be, nu):
        return (i, 0)

    def w_map(i, be, nu):
        return (layer, be[i], 0, 0)

    return pl.pallas_call(
        functools.partial(_expert_kernel, d_exp=d_exp),
        out_shape=jax.ShapeDtypeStruct((n_rows, d), F32),
        grid_spec=pltpu.PrefetchScalarGridSpec(
            num_scalar_prefetch=2,
            grid=(n_blocks,),
            in_specs=[
                pl.BlockSpec((tm, d), in_map),
                pl.BlockSpec((1, 1, d, d_exp2), w_map),
                pl.BlockSpec((1, 1, 1, d_exp2), w_map),
                pl.BlockSpec((1, 1, d_exp, d), w_map),
                pl.BlockSpec((1, 1, 1, d), w_map),
            ],
            out_specs=pl.BlockSpec((tm, d), out_map),
            scratch_shapes=[pltpu.VMEM((d, d_exp2), BF16), pltpu.VMEM((d_exp, d), BF16)],
        ),
        compiler_params=_cparams(("arbitrary",)),
        name="experts",
    )(block_expert, n_used, buf, w1, b1.reshape(b1.shape[0], n_exp, 1, d_exp2),
      w2, b2.reshape(b2.shape[0], n_exp, 1, d))


def _combine_kernel(src_ref, dst_ref, len_ref, y_hbm, lpos_ref, gate_ref, x_ref, gt_ref, fg_ref,
                    o_ref, ycomp_ref, sem_seg, *, ts, lrows, n_exp, n_tiles, final_norm):
    i = pl.program_id(0)
    slot = i % 2

    def seg_copy_into(buf_slot):
        def seg_copy(local_row, global_row, size):
            return pltpu.make_async_copy(y_hbm.at[pl.ds(global_row, size)],
                                         ycomp_ref.at[buf_slot, pl.ds(local_row, size)],
                                         sem_seg.at[buf_slot])
        return seg_copy

    @pl.when(i == 0)
    def _():
        ycomp_ref[...] = jnp.zeros_like(ycomp_ref)
        _segment_copies(0, n_exp, ts, src_ref, dst_ref, len_ref, seg_copy_into(0))

    @pl.when(i + 1 < n_tiles)
    def _():
        _segment_copies(i + 1, n_exp, ts, src_ref, dst_ref, len_ref, seg_copy_into(1 - slot))

    lpos = lpos_ref[...]
    gates = gate_ref[...]
    cols = lax.broadcasted_iota(jnp.int32, (ts, lrows), 1)
    sel = jnp.zeros((ts, lrows), F32)
    for k in range(TOP_K):
        sel = jnp.where(cols == lpos[:, k:k + 1], gates[:, k:k + 1], sel)
    sel = sel.astype(BF16)

    _segment_wait(i, n_exp, lrows, src_ref, len_ref, seg_copy_into(slot))
    acc = _dot(sel, ycomp_ref[slot].astype(BF16))
    out = x_ref[...] + gt_ref[0] * acc
    if final_norm:
        ms = jnp.mean(out * out, axis=-1, keepdims=True)
        out = out * lax.rsqrt(ms + EPS) * fg_ref[...]
    o_ref[...] = out


def _combine(seg_src, seg_dst, seg_len, y_buf, lpos_t, gates_t, x2d, gt, final_g, s, final_norm,
             ts, lrows, n_exp):
    t, d = x2d.shape
    per_b = s // ts
    tile_map = lambda i, *_: (i, 0)
    return pl.pallas_call(
        functools.partial(_combine_kernel, ts=ts, lrows=lrows, n_exp=n_exp, n_tiles=t // ts,
                          final_norm=final_norm),
        out_shape=jax.ShapeDtypeStruct((t, d), F32),
        grid_spec=pltpu.PrefetchScalarGridSpec(
            num_scalar_prefetch=3,
            grid=(t // ts,),
            in_specs=[
                pl.BlockSpec(memory_space=pl.ANY),
                pl.BlockSpec((ts, TOP_K), tile_map),
                pl.BlockSpec((ts, TOP_K), tile_map),
                pl.BlockSpec((ts, d), tile_map),
                pl.BlockSpec((1, 1, d), lambda i, *_: (i // per_b, 0, 0)),
                pl.BlockSpec((1, d), lambda i, *_: (0, 0)),
            ],
            out_specs=pl.BlockSpec((ts, d), tile_map),
            scratch_shapes=[
                pltpu.VMEM((2, lrows, d), F32),
                pltpu.SemaphoreType.DMA((2,)),
            ],
        ),
        compiler_params=_cparams(("arbitrary",)),
        name="combine",
    )(seg_src, seg_dst, seg_len, y_buf, lpos_t, gates_t, x2d, gt, final_g.reshape(1, d))


def _moe_layer(x, sh, sc, gt, norm_g, router_w, router_b, w1, b1, w2, b2, layer, final_g, final_norm):
    bsz, s, d = x.shape
    t = bsz * s
    n_exp = router_w.shape[1]
    tm = EXPERT_ROWS
    ts = min(MOE_TILE, s)
    n_tiles = t // ts
    lrows = -(-(ts * TOP_K + n_exp * (SEG_ALIGN - 1)) // MXU_ROWS) * MXU_ROWS
    h, lpos, gates, tab, tot = _route(x, sh, sc, norm_g, router_w, router_b, ts)

    rows = tot[:, 0].astype(jnp.int32)
    padded = (rows + tm - 1) // tm * tm
    pends = jnp.cumsum(padded).astype(jnp.int32)
    pstart = pends - padded
    n_blocks = -(-(t * TOP_K + n_tiles * n_exp * (SEG_ALIGN - 1) + n_exp * (tm - 1)) // tm)
    n_used = pends[-1] // tm
    blk = jnp.minimum(jnp.arange(n_blocks, dtype=jnp.int32), n_used - 1)
    block_expert = jnp.minimum(
        jnp.sum((pends[None, :] <= (blk * tm)[:, None]).astype(jnp.int32), axis=1), n_exp - 1)
    tab = tab[:, :, :3].astype(jnp.int32)
    seg_src = tab[:, :, 0].reshape(-1)
    seg_dst = (pstart[None, :] + tab[:, :, 1]).reshape(-1)
    seg_len = tab[:, :, 2].reshape(-1)

    buf = _dispatch(pstart, pends, seg_src, seg_dst, seg_len, lpos, h.reshape(t, d),
                    n_blocks * tm, ts, lrows)
    y_buf = _experts(block_expert, n_used.reshape(1), buf, w1, b1, w2, b2, layer)
    out = _combine(seg_src, seg_dst, seg_len, y_buf, lpos.T, gates.T, x.reshape(t, d), gt, final_g,
                   s, final_norm, ts, lrows, n_exp)
    return out.reshape(bsz, s, d)


def _head_rms(x, seg, seg_t, inv_dh):
    hi, lo = _split2(x * x)
    ssum = _dot(hi, seg) + _dot(lo, seg)
    inv = lax.rsqrt(ssum * inv_dh + EPS)
    ihi, ilo = _split2(inv)
    return _dot(ihi, seg_t) + _dot(ilo, seg_t)


def _kvq_kernel(x_ref, ksh_ref, ksc_ref, msh_ref, msc_ref, kg_ref, mg_ref,
                wk_ref, wvt_ref, wf_ref, fb_ref, kng_ref, qng_ref, wq_ref, wg_ref,
                place_ref, cst_ref,
                k_ref, kx_ref, vt_ref, q_ref, qx_ref, sg_ref, ccol_ref, *, ts, d):
    @pl.when(pl.program_id(1) == 0)
    def _():
        ccol_ref[...] = jnp.zeros_like(ccol_ref)

    x = x_ref[0]
    ms = jnp.mean(x * x, axis=-1, keepdims=True)
    xn = x * lax.rsqrt(ms + EPS)
    hk = ((xn * kg_ref[...]) * (1.0 + ksc_ref[0]) + ksh_ref[0]).astype(BF16)
    hq = ((xn * mg_ref[...]) * (1.0 + msc_ref[0]) + msh_ref[0]).astype(BF16)

    di = lax.broadcasted_iota(jnp.int32, (d, LANES), 0) // HEAD_DIM
    hi_ = lax.broadcasted_iota(jnp.int32, (d, LANES), 1)
    seg = jnp.where(di == hi_, 1.0, 0.0).astype(BF16)
    dt = lax.broadcasted_iota(jnp.int32, (LANES, d), 1) // HEAD_DIM
    ht = lax.broadcasted_iota(jnp.int32, (LANES, d), 0)
    seg_t = jnp.where(dt == ht, 1.0, 0.0).astype(BF16)
    inv_dh = 1.0 / HEAD_DIM

    k = _dot(hk, wk_ref[...])
    k_ref[0] = (k * _head_rms(k, seg, seg_t, inv_dh) * kng_ref[...]).astype(BF16)
    vt_ref[0] = _dot_nt(wvt_ref[...], hk).astype(BF16)

    q = _dot(hq, wq_ref[...])
    qn = q * _head_rms(q, seg, seg_t, inv_dh) * qng_ref[...]
    q_ref[0] = (qn * (HEAD_DIM ** -0.5 * LOG2E)).astype(BF16)
    sg_ref[0] = jax.nn.sigmoid(_dot(hq, wg_ref[...]))

    lf = jax.nn.log_sigmoid(_dot(hk, wf_ref[...]) + fb_ref[...])
    r = lax.broadcasted_iota(jnp.int32, (ts, ts), 0)
    c = lax.broadcasted_iota(jnp.int32, (ts, ts), 1)
    lower = jnp.where(c <= r, 1.0, 0.0).astype(BF16)
    a, b, e = _split3(lf)
    cum = ccol_ref[...] + (_dot(lower, a) + _dot(lower, b) + _dot(lower, e))
    ccol_ref[...] = cum[ts - 1:ts, :]
    a, b, e = _split3(cum * LOG2E)
    extra = (_dot(a, place_ref[0]) + _dot(b, place_ref[1]) + _dot(e, place_ref[2])
             + cst_ref[...]).astype(BF16)
    kx_ref[0] = extra[:, :LANES]
    qx_ref[0] = extra[:, LANES:]


def _bias_lane_tables(n_heads):
    assert 2 * N_PARTS * n_heads <= LANES
    place = np.zeros((N_PARTS, LANES, 2 * LANES), np.float32)
    cst = np.zeros((1, 2 * LANES), np.float32)
    for h in range(n_heads):
        base = 2 * N_PARTS * h
        for part in range(N_PARTS):
            place[part, h, base + part] = 1.0
            place[part, h, LANES + base + N_PARTS + part] = 1.0
        cst[0, base + N_PARTS:base + 2 * N_PARTS] = 1.0
        cst[0, LANES + base:LANES + base + N_PARTS] = -1.0
    return jnp.asarray(place, BF16), jnp.asarray(cst, F32)


def _kvq(x, ksh, ksc, msh, msc, kv_norm_g, mix_norm_g, kv_w, k_norm_g, fgate_b,
         w_qg, q_norm_g, ts=256):
    bsz, s, d = x.shape
    n_heads = d // HEAD_DIM
    ts = min(ts, s)
    wk = kv_w[:, :d].astype(BF16)
    wvt = kv_w[:, d:2 * d].T.astype(BF16)
    wf = kv_w[:, 2 * d:]
    wf_pad = jnp.pad(wf, ((0, 0), (0, LANES - n_heads))).astype(BF16)
    fb_pad = jnp.pad(fgate_b, (0, LANES - n_heads)).reshape(1, LANES)
    place, cst = _bias_lane_tables(n_heads)
    vec = pl.BlockSpec((1, 1, d), lambda b, i: (b, 0, 0))
    c2 = lambda shape: pl.BlockSpec(shape, lambda b, i: (0, 0))
    tile = pl.BlockSpec((1, ts, d), lambda b, i: (b, i, 0))
    tile_x = pl.BlockSpec((1, ts, LANES), lambda b, i: (b, i, 0))
    return pl.pallas_call(
        functools.partial(_kvq_kernel, ts=ts, d=d),
        out_shape=(
            jax.ShapeDtypeStruct((bsz, s, d), BF16),
            jax.ShapeDtypeStruct((bsz, s, LANES), BF16),
            jax.ShapeDtypeStruct((bsz, d, s), BF16),
            jax.ShapeDtypeStruct((bsz, s, d), BF16),
            jax.ShapeDtypeStruct((bsz, s, LANES), BF16),
            jax.ShapeDtypeStruct((bsz, s, d), F32),
        ),
        grid=(bsz, s // ts),
        in_specs=[
            tile, vec, vec, vec, vec,
            c2((1, d)), c2((1, d)),
            c2((d, d)), c2((d, d)), c2((d, LANES)), c2((1, LANES)),
            c2((1, d)), c2((1, d)),
            c2((d, d)), c2((d, d)),
            pl.BlockSpec((N_PARTS, LANES, 2 * LANES), lambda b, i: (0, 0, 0)),
            c2((1, 2 * LANES)),
        ],
        out_specs=(
            tile, tile_x,
            pl.BlockSpec((1, d, ts), lambda b, i: (b, 0, i)),
            tile, tile_x,
            tile,
        ),
        scratch_shapes=[pltpu.VMEM((1, LANES), F32)],
        compiler_params=_cparams(("arbitrary", "arbitrary")),
        name="kvq",
    )(x, ksh, ksc, msh, msc, kv_norm_g.reshape(1, d), mix_norm_g.reshape(1, d),
      wk, wvt, wf_pad, fb_pad,
      jnp.tile(k_norm_g, n_heads).reshape(1, d), jnp.tile(q_norm_g, n_heads).reshape(1, d),
      w_qg[:, :d].astype(BF16), w_qg[:, d:].astype(BF16), place, cst)


def _attn_kernel(q_ref, qx_ref, k_ref, kx_ref, vt_ref, o_ref, qh_ref, s0_ref, s1_ref, acc_ref,
                 *, tq, tk):
    pair = pl.program_id(1)
    qi = pl.program_id(2)
    q = jnp.concatenate([q_ref[0], qx_ref[0]], axis=1)
    lane = lax.broadcasted_iota(jnp.int32, (tq, 2 * LANES), 1)
    zero = jnp.zeros_like(q)
    for hh in range(2):
        feat = jnp.logical_and(lane >= hh * HEAD_DIM, lane < (hh + 1) * HEAD_DIM)
        first_bias = LANES + 2 * N_PARTS * (2 * pair + hh)
        bias = jnp.logical_and(lane >= first_bias, lane < first_bias + 2 * N_PARTS)
        qh_ref[hh] = jnp.where(feat | bias, q, zero)
    acc_ref[...] = jnp.zeros_like(acc_ref)
    ones = jnp.ones((DENOM_ROWS, tk), BF16)

    def scores_into(kb, s_ref):
        k0 = pl.multiple_of(kb * tk, tk)
        ks = jnp.concatenate([k_ref[0, pl.ds(k0, tk), :], kx_ref[0, pl.ds(k0, tk), :]], axis=1)
        for hh in range(2):
            s_ref[hh] = _dot_nt(ks, qh_ref[hh])

    def softmax_pv(kb, s_ref, ms, masked):
        k0 = pl.multiple_of(kb * tk, tk)
        vt = vt_ref[0, :, pl.ds(k0, tk)]
        if masked:
            keep = (lax.broadcasted_iota(jnp.int32, (tk, tq), 0)
                    <= lax.broadcasted_iota(jnp.int32, (tk, tq), 1))
        out = []
        for hh in range(2):
            st = s_ref[hh]
            if masked:
                st = jnp.where(keep, st, NEG_INF)
            m_new = jnp.maximum(ms[hh], jnp.max(st, axis=0, keepdims=True))
            p = jnp.exp2(st - m_new).astype(BF16)
            alpha = jnp.exp2(ms[hh] - m_new)
            vth = jnp.concatenate([vt[hh * HEAD_DIM:(hh + 1) * HEAD_DIM, :], ones], axis=0)
            acc_ref[hh] = alpha * acc_ref[hh] + _dot(vth, p)
            out.append(m_new)
        return tuple(out)

    def two_blocks(j, ms):
        scores_into(2 * j + 1, s1_ref)
        ms = softmax_pv(2 * j, s0_ref, ms, False)
        scores_into(2 * j + 2, s0_ref)
        return softmax_pv(2 * j + 1, s1_ref, ms, False)

    scores_into(0, s0_ref)
    ms = lax.fori_loop(0, qi // 2, two_blocks,
                       tuple(jnp.full((1, tq), NEG_INF, F32) for _ in range(2)))

    @pl.when(qi % 2 == 1)
    def _():
        scores_into(qi, s1_ref)
        softmax_pv(qi, s1_ref, softmax_pv(qi - 1, s0_ref, ms, False), True)

    @pl.when(qi % 2 == 0)
    def _():
        softmax_pv(qi, s0_ref, ms, True)

    ot = jnp.concatenate(
        [acc_ref[hh, :HEAD_DIM, :] / acc_ref[hh, HEAD_DIM:HEAD_DIM + 1, :] for hh in range(2)],
        axis=0)
    o_ref[0] = ot.T


def _attention(q, qx, k, kx, vt, tq=512):
    bsz, s, d = q.shape
    n_pairs = d // LANES
    tq = min(tq, s)
    tk = tq
    return pl.pallas_call(
        functools.partial(_attn_kernel, tq=tq, tk=tk),
        out_shape=jax.ShapeDtypeStruct((bsz, s, d), F32),
        grid=(bsz, n_pairs, s // tq),
        in_specs=[
            pl.BlockSpec((1, tq, LANES), lambda b, j, i: (b, i, j)),
            pl.BlockSpec((1, tq, LANES), lambda b, j, i: (b, i, 0)),
            pl.BlockSpec((1, s, LANES), lambda b, j, i: (b, 0, j)),
            pl.BlockSpec((1, s, LANES), lambda b, j, i: (b, 0, 0)),
            pl.BlockSpec((1, LANES, s), lambda b, j, i: (b, j, 0)),
        ],
        out_specs=pl.BlockSpec((1, tq, LANES), lambda b, j, i: (b, i, j)),
        scratch_shapes=[
            pltpu.VMEM((2, tq, 2 * LANES), BF16),
            pltpu.VMEM((2, tk, tq), F32),
            pltpu.VMEM((2, tk, tq), F32),
            pltpu.VMEM((2, HEAD_DIM + DENOM_ROWS, tq), F32),
        ],
        compiler_params=_cparams(("parallel", "parallel", "arbitrary")),
        name="attn",
    )(q, qx, k, kx, vt)


def _attnout_kernel(x_ref, o_ref, sg_ref, gt_ref, wo_ref, out_ref):
    y = _dot((o_ref[0] * sg_ref[0]).astype(BF16), wo_ref[...])
    out_ref[0] = x_ref[0] + gt_ref[0] * y


def _attnout(x, o, sg, gt, w_o, ts=512):
    bsz, s, d = x.shape
    ts = min(ts, s)
    tile = pl.BlockSpec((1, ts, d), lambda b, i: (b, i, 0))
    return pl.pallas_call(
        _attnout_kernel,
        out_shape=jax.ShapeDtypeStruct((bsz, s, d), F32),
        grid=(bsz, s // ts),
        in_specs=[tile, tile, tile,
                  pl.BlockSpec((1, 1, d), lambda b, i: (b, 0, 0)),
                  pl.BlockSpec((d, d), lambda b, i: (0, 0))],
        out_specs=tile,
        compiler_params=_cparams(("parallel", "parallel")),
        name="attnout",
    )(x, o, sg, gt, w_o.astype(BF16))


def _chunks(mod, n):
    return [m[:, None, :] for m in jnp.split(mod, n, axis=-1)]


def kernel(x, c, ada_w, ada_b, norm_mix_g, norm_ffn_g, gmlp_w_in, gmlp_v_g, gmlp_v_b, gmlp_w_s,
           gmlp_b_s, gmlp_w_out, kv_ada_w, kv_ada_b, kv_norm_g, kv_w, k_norm_g, fgate_b, fox_w_qg,
           q_norm_g, fox_w_o, router_w, router_b, exp_w1, exp_b1, exp_w2, exp_b2, final_g):
    mods = _ada(c, ada_w, ada_b)
    kv_mod = _ada(c, kv_ada_w[None], kv_ada_b[None])[0]

    sh_m, sc_m, gt_m, sh_f, sc_f, gt_f = _chunks(mods[0], 6)
    x = _gmlp(x, sh_m, sc_m, gt_m, norm_mix_g[0], gmlp_w_in[0], gmlp_v_g[0], gmlp_v_b[0],
              gmlp_w_s[0], gmlp_b_s[0], gmlp_w_out[0])
    x = _moe_layer(x, sh_f, sc_f, gt_f, norm_ffn_g[0], router_w[0], router_b[0],
                   exp_w1, exp_b1, exp_w2, exp_b2, 0, final_g, False)

    sh_m, sc_m, gt_m, sh_f, sc_f, gt_f = _chunks(mods[1], 6)
    ksh, ksc = _chunks(kv_mod, 2)
    k, kx, vt, q, qx, sg = _kvq(x, ksh, ksc, sh_m, sc_m, kv_norm_g, norm_mix_g[1], kv_w,
                                k_norm_g, fgate_b, fox_w_qg[0], q_norm_g[0])
    o = _attention(q, qx, k, kx, vt)
    x = _attnout(x, o, sg, gt_m, fox_w_o[0])
    return _moe_layer(x, sh_f, sc_f, gt_f, norm_ffn_g[1], router_w[1], router_b[1],
                      exp_w1, exp_b1, exp_w2, exp_b2, 1, final_g, True)
```

```python
import functools

import numpy as np
import jax
import jax.numpy as jnp
from jax import lax
from jax.experimental import pallas as pl
from jax.experimental.pallas import tpu as pltpu

F32 = jnp.float32
BF16 = jnp.bfloat16

CHUNK = 128
GMLP_GROUPS = 8
HEAD_DIM = 64
TOP_K = 4
SWIGLU_LIMIT = 7.0
SWIGLU_ALPHA = 1.702
EPS = 1e-6
NEG_INF = -1e30
LOG2E = 1.4426950408889634
N_PARTS = 3
DENOM_ROWS = 16

LANES = 128
MXU_ROWS = 256
EXPERT_ROWS = 2 * MXU_ROWS
SEG_ALIGN = 8
MOE_TILE = 512
VMEM_LIMIT = 56 * 1024 * 1024


def _cparams(sem, vmem=VMEM_LIMIT):
    return pltpu.CompilerParams(dimension_semantics=sem, vmem_limit_bytes=vmem)


def _divisor_tile(n, pref):
    if n <= pref:
        return n
    return max(t for t in range(LANES, pref + 1, LANES) if n % t == 0)


def _dot(a, b):
    return jnp.dot(a, b, preferred_element_type=F32)


def _dot_nt(a, b):
    return lax.dot_general(a, b, (((1,), (1,)), ((), ())), preferred_element_type=F32)


def _split3(x):
    hi = x.astype(BF16)
    r = x - hi.astype(F32)
    mid = r.astype(BF16)
    lo = (r - mid.astype(F32)).astype(BF16)
    return hi, mid, lo


def _split2(x):
    hi = x.astype(BF16)
    lo = (x - hi.astype(F32)).astype(BF16)
    return hi, lo


def _gelu_exact(x):
    return 0.5 * x * (1.0 + lax.erf(x * (0.5 ** 0.5)))


def _ada_norm(x, g, shift, scale):
    ms = jnp.mean(x * x, axis=-1, keepdims=True)
    y = x * lax.rsqrt(ms + EPS) * g
    return y * (1.0 + scale) + shift


def _ada_kernel(c_ref, w_ref, b_ref, o_ref):
    c = c_ref[...]
    sc = c * jax.nn.sigmoid(c)
    o_ref[0] = _dot(sc.astype(BF16), w_ref[0].astype(BF16)) + b_ref[0]


def _ada(c, w, b):
    nl, d, n = w.shape
    bsz = c.shape[0]
    tn = _divisor_tile(n, 2048)
    return pl.pallas_call(
        _ada_kernel,
        out_shape=jax.ShapeDtypeStruct((nl, bsz, n), F32),
        grid=(nl, n // tn),
        in_specs=[
            pl.BlockSpec((bsz, d), lambda l, j: (0, 0)),
            pl.BlockSpec((1, d, tn), lambda l, j: (l, 0, j)),
            pl.BlockSpec((1, 1, tn), lambda l, j: (l, 0, j)),
        ],
        out_specs=pl.BlockSpec((1, bsz, tn), lambda l, j: (l, 0, j)),
        compiler_params=_cparams(("arbitrary", "arbitrary")),
        name="ada",
    )(c, w, b.reshape(nl, 1, n))


def _gmlp_kernel(x_ref, sh_ref, sc_ref, gt_ref, g_ref, win_ref, vg_ref, vb_ref,
                 ws_ref, bst_ref, wout_ref, o_ref, gated_ref, *, ts, width):
    x = x_ref[0]
    h = _ada_norm(x, g_ref[...], sh_ref[0], sc_ref[0]).astype(BF16)
    z = _gelu_exact(_dot(h, win_ref[...]))
    u = z[:, :width]
    v = z[:, width:]
    mu = jnp.mean(v, axis=-1, keepdims=True)
    vc = v - mu
    var = jnp.mean(vc * vc, axis=-1, keepdims=True)
    vn = (vc * lax.rsqrt(var + EPS) * vg_ref[...] + vb_ref[...]).astype(BF16)

    gd = width // GMLP_GROUPS
    row = lax.broadcasted_iota(jnp.int32, (CHUNK, CHUNK), 0)
    col = lax.broadcasted_iota(jnp.int32, (CHUNK, CHUNK), 1)
    causal = col <= row
    bst = bst_ref[...]
    for g in range(GMLP_GROUPS):
        wc = jnp.where(causal, ws_ref[g], 0.0).astype(BF16)
        bias = bst[:, g:g + 1]
        for ci in range(ts // CHUNK):
            rs = slice(ci * CHUNK, (ci + 1) * CHUNK)
            cs = slice(g * gd, (g + 1) * gd)
            mixed = _dot(wc, vn[rs, cs]) + bias
            gated_ref[rs, cs] = (u[rs, cs] * mixed).astype(BF16)
    y = _dot(gated_ref[...], wout_ref[...])
    o_ref[0] = x + gt_ref[0] * y


def _gmlp(x, sh, sc, gt, g, w_in, v_g, v_b, w_s, b_s, w_out, ts=512):
    bsz, s, d = x.shape
    width = w_out.shape[0]
    ts = min(ts, s)
    vec = pl.BlockSpec((1, 1, d), lambda b, i: (b, 0, 0))
    const2 = lambda shape: pl.BlockSpec(shape, lambda b, i: (0, 0))
    return pl.pallas_call(
        functools.partial(_gmlp_kernel, ts=ts, width=width),
        out_shape=jax.ShapeDtypeStruct((bsz, s, d), F32),
        grid=(bsz, s // ts),
        in_specs=[
            pl.BlockSpec((1, ts, d), lambda b, i: (b, i, 0)),
            vec, vec, vec,
            const2((1, d)),
            const2((d, 2 * width)),
            const2((1, width)),
            const2((1, width)),
            pl.BlockSpec((GMLP_GROUPS, CHUNK, CHUNK), lambda b, i: (0, 0, 0)),
            const2((CHUNK, GMLP_GROUPS)),
            const2((width, d)),
        ],
        out_specs=pl.BlockSpec((1, ts, d), lambda b, i: (b, i, 0)),
        scratch_shapes=[pltpu.VMEM((ts, width), BF16)],
        compiler_params=_cparams(("parallel", "parallel")),
        name="gmlp",
    )(x, sh, sc, gt, g.reshape(1, d), w_in.astype(BF16), v_g.reshape(1, width),
      v_b.reshape(1, width), w_s, b_s.T, w_out.astype(BF16))


def _route_kernel(x_ref, sh_ref, sc_ref, g_ref, rwt_ref, rb_ref,
                  h_ref, lpos_ref, gate_ref, tab_ref, tot_ref, carry_ref, *, ts, n_exp):
    first = jnp.logical_and(pl.program_id(0) == 0, pl.program_id(1) == 0)

    @pl.when(first)
    def _():
        carry_ref[...] = jnp.zeros_like(carry_ref)

    h = _ada_norm(x_ref[0], g_ref[...], sh_ref[0], sc_ref[0]).astype(BF16)
    h_ref[0] = h
    logits = _dot_nt(rwt_ref[...], h) + rb_ref[...]

    eio = lax.broadcasted_iota(jnp.int32, (n_exp, ts), 0)
    work = logits
    vals, hots = [], []
    for _ in range(TOP_K):
        m = jnp.max(work, axis=0, keepdims=True)
        sel = jnp.min(jnp.where(work == m, eio, n_exp), axis=0, keepdims=True)
        hot = eio == sel
        vals.append(m)
        hots.append(hot)
        work = jnp.where(hot, -jnp.inf, work)

    exps = [jnp.exp(v - vals[0]) for v in vals]
    denom = exps[0]
    for e in exps[1:]:
        denom = denom + e
    gates = [e / denom for e in exps]

    chosen = hots[0]
    for hot in hots[1:]:
        chosen = jnp.logical_or(chosen, hot)
    chosen_f = jnp.where(chosen, 1.0, 0.0)
    r = lax.broadcasted_iota(jnp.int32, (ts, ts), 0)
    c = lax.broadcasted_iota(jnp.int32, (ts, ts), 1)
    before = jnp.where(r < c, 1.0, 0.0).astype(BF16)
    prefix = _dot(chosen_f.astype(BF16), before)
    count = jnp.sum(chosen_f, axis=1, keepdims=True)
    seg_len = jnp.floor((count + (SEG_ALIGN - 1)) * (1.0 / SEG_ALIGN)) * SEG_ALIGN
    er = lax.broadcasted_iota(jnp.int32, (n_exp, n_exp), 0)
    ec = lax.broadcasted_iota(jnp.int32, (n_exp, n_exp), 1)
    lower = jnp.where(ec < er, 1.0, 0.0).astype(BF16)
    seg_start = _dot(lower, jnp.broadcast_to(seg_len, (n_exp, LANES)).astype(BF16))[:, 0:1]
    base = seg_start + prefix
    lpos = [jnp.sum(jnp.where(hot, base, 0.0), axis=0, keepdims=True) for hot in hots]

    lpos_ref[...] = jnp.concatenate(lpos, axis=0).astype(jnp.int32)
    gate_ref[...] = jnp.concatenate(gates, axis=0)
    used = carry_ref[...]
    lane = lax.broadcasted_iota(jnp.int32, (n_exp, LANES), 1)
    tab_ref[0] = jnp.where(lane == 0, seg_start,
                           jnp.where(lane == 1, used, jnp.where(lane == 2, seg_len, 0.0)))
    carry_ref[...] = used + seg_len
    tot_ref[...] = jnp.broadcast_to(used + seg_len, tot_ref.shape)


def _route(x, sh, sc, g, router_w, router_b, ts):
    bsz, s, d = x.shape
    n_exp = router_w.shape[1]
    nt = s // ts
    t = bsz * s
    vec = pl.BlockSpec((1, 1, d), lambda b, i: (b, 0, 0))
    slot = pl.BlockSpec((TOP_K, ts), lambda b, i: (0, b * nt + i))
    return pl.pallas_call(
        functools.partial(_route_kernel, ts=ts, n_exp=n_exp),
        out_shape=(
            jax.ShapeDtypeStruct((bsz, s, d), BF16),
            jax.ShapeDtypeStruct((TOP_K, t), jnp.int32),
            jax.ShapeDtypeStruct((TOP_K, t), F32),
            jax.ShapeDtypeStruct((bsz * nt, n_exp, LANES), F32),
            jax.ShapeDtypeStruct((n_exp, LANES), F32),
        ),
        grid=(bsz, nt),
        in_specs=[
            pl.BlockSpec((1, ts, d), lambda b, i: (b, i, 0)),
            vec, vec,
            pl.BlockSpec((1, d), lambda b, i: (0, 0)),
            pl.BlockSpec((n_exp, d), lambda b, i: (0, 0)),
            pl.BlockSpec((n_exp, 1), lambda b, i: (0, 0)),
        ],
        out_specs=(
            pl.BlockSpec((1, ts, d), lambda b, i: (b, i, 0)),
            slot, slot,
            pl.BlockSpec((1, n_exp, LANES), lambda b, i: (b * nt + i, 0, 0)),
            pl.BlockSpec((n_exp, LANES), lambda b, i: (0, 0)),
        ),
        scratch_shapes=[pltpu.VMEM((n_exp, 1), F32)],
        compiler_params=_cparams(("arbitrary", "arbitrary")),
        name="route",
    )(x, sh, sc, g.reshape(1, d), router_w.T.astype(BF16), router_b.reshape(n_exp, 1))


def _wait_rows(src, dst, sem):
    pltpu.make_async_copy(src, dst, sem).wait()


def _chunk_sizes(max_rows):
    sizes = []
    size = SEG_ALIGN
    while size <= max_rows:
        sizes.append(size)
        size *= 2
    return sizes


def _for_each_chunk(length, max_rows, fn):
    for size in reversed(_chunk_sizes(max_rows)):
        shift = size.bit_length() - 1
        @pl.when(((length >> shift) & 1) == 1)
        def _():
            fn(pl.multiple_of((length >> (shift + 1)) << (shift + 1), SEG_ALIGN), size)


def _segment_copies(tile, n_exp, max_rows, src_ref, dst_ref, len_ref, make_copy):
    for e in range(n_exp):
        s = tile * n_exp + e
        src, dst = src_ref[s], dst_ref[s]
        _for_each_chunk(len_ref[s], max_rows,
                        lambda off, size: make_copy(pl.multiple_of(src + off, SEG_ALIGN),
                                                    pl.multiple_of(dst + off, SEG_ALIGN),
                                                    size).start())


def _segment_wait(tile, n_exp, lrows, src_ref, len_ref, make_copy):
    last = tile * n_exp + n_exp - 1
    total = src_ref[last] + len_ref[last]
    _for_each_chunk(total, lrows, lambda off, size: make_copy(0, 0, size).wait())


def _dispatch_kernel(pstart_ref, pend_ref, src_ref, dst_ref, len_ref, lpos_ref, h_ref, buf_hbm,
                     comp_ref, zero_ref, sem_seg, sem_zero,
                     *, ts, lrows, tm, n_exp, n_blocks, n_tiles):
    i = pl.program_id(0)
    slot = i % 2

    def seg_copy_from(buf_slot):
        def seg_copy(local_row, global_row, size):
            return pltpu.make_async_copy(comp_ref.at[buf_slot, pl.ds(local_row, size)],
                                         buf_hbm.at[pl.ds(global_row, size)], sem_seg.at[buf_slot])
        return seg_copy

    @pl.when(i == 0)
    def _():
        zero_ref[...] = jnp.zeros_like(zero_ref)
        for e in range(n_exp):
            @pl.when(pend_ref[e] > pstart_ref[e])
            def _():
                last = pl.multiple_of(pend_ref[e] - tm, tm)
                pltpu.make_async_copy(zero_ref, buf_hbm.at[pl.ds(last, tm)], sem_zero).start()
        n_used = pend_ref[n_exp - 1] // tm

        def zero_tail(b, carry):
            off = pl.multiple_of(b * tm, tm)
            pltpu.make_async_copy(zero_ref, buf_hbm.at[pl.ds(off, tm)], sem_zero).start()
            return carry

        def wait_tail(b, carry):
            _wait_rows(zero_ref, buf_hbm.at[pl.ds(0, tm)], sem_zero)
            return carry

        lax.fori_loop(n_used, n_blocks, zero_tail, 0)
        for e in range(n_exp):
            @pl.when(pend_ref[e] > pstart_ref[e])
            def _():
                _wait_rows(zero_ref, buf_hbm.at[pl.ds(0, tm)], sem_zero)
        lax.fori_loop(n_used, n_blocks, wait_tail, 0)

    lpos = lpos_ref[...]
    rows = lax.broadcasted_iota(jnp.int32, (lrows, ts), 0)
    hit = rows == lpos[0:1, :]
    for k in range(1, TOP_K):
        hit = jnp.logical_or(hit, rows == lpos[k:k + 1, :])
    onehot = jnp.where(hit, 1.0, 0.0).astype(BF16)

    @pl.when(i > 1)
    def _():
        _segment_wait(i - 2, n_exp, lrows, src_ref, len_ref, seg_copy_from(slot))

    comp_ref[slot] = _dot(onehot, h_ref[...])
    _segment_copies(i, n_exp, ts, src_ref, dst_ref, len_ref, seg_copy_from(slot))

    @pl.when(i == n_tiles - 1)
    def _():
        @pl.when(i > 0)
        def _():
            _segment_wait(i - 1, n_exp, lrows, src_ref, len_ref, seg_copy_from(1 - slot))
        _segment_wait(i, n_exp, lrows, src_ref, len_ref, seg_copy_from(slot))


def _dispatch(pstart, pends, seg_src, seg_dst, seg_len, lpos, h, n_rows, ts, lrows):
    t, d = h.shape
    tm = EXPERT_ROWS
    n_tiles = t // ts
    tile_map = lambda i, *_: (i, 0)
    return pl.pallas_call(
        functools.partial(_dispatch_kernel, ts=ts, lrows=lrows, tm=tm, n_exp=pstart.shape[0],
                          n_blocks=n_rows // tm, n_tiles=n_tiles),
        out_shape=jax.ShapeDtypeStruct((n_rows, d), F32),
        grid_spec=pltpu.PrefetchScalarGridSpec(
            num_scalar_prefetch=5,
            grid=(n_tiles,),
            in_specs=[
                pl.BlockSpec((TOP_K, ts), lambda i, *_: (0, i)),
                pl.BlockSpec((ts, d), tile_map),
            ],
            out_specs=pl.BlockSpec(memory_space=pl.ANY),
            scratch_shapes=[
                pltpu.VMEM((2, lrows, d), F32),
                pltpu.VMEM((tm, d), F32),
                pltpu.SemaphoreType.DMA((2,)),
                pltpu.SemaphoreType.DMA,
            ],
        ),
        compiler_params=_cparams(("arbitrary",)),
        name="dispatch",
    )(pstart, pends, seg_src, seg_dst, seg_len, lpos, h)


def _expert_kernel(be_ref, nu_ref, x_ref, w1_ref, b1_ref, w2_ref, b2_ref, y_ref,
                   w1b_ref, w2b_ref, *, d_exp):
    i = pl.program_id(0)
    changed = jnp.logical_or(i == 0, be_ref[i] != be_ref[jnp.maximum(i - 1, 0)])

    @pl.when(changed)
    def _():
        w1b_ref[...] = w1_ref[0, 0].astype(BF16)
        w2b_ref[...] = w2_ref[0, 0].astype(BF16)

    @pl.when(i < nu_ref[0])
    def _():
        gu = _dot(x_ref[...].astype(BF16), w1b_ref[...]) + b1_ref[0, 0]
        glu = jnp.minimum(gu[:, :d_exp], SWIGLU_LIMIT)
        lin = jnp.clip(gu[:, d_exp:], -SWIGLU_LIMIT, SWIGLU_LIMIT)
        act = (lin + 1.0) * (glu * jax.nn.sigmoid(SWIGLU_ALPHA * glu))
        y_ref[...] = _dot(act.astype(BF16), w2b_ref[...]) + b2_ref[0, 0]

    @pl.when(i >= nu_ref[0])
    def _():
        y_ref[...] = jnp.zeros_like(y_ref)


def _experts(block_expert, n_used, buf, w1, b1, w2, b2, layer):
    n_rows, d = buf.shape
    _, n_exp, _, d_exp2 = w1.shape
    d_exp = d_exp2 // 2
    tm = EXPERT_ROWS
    n_blocks = n_rows // tm

    def in_map(i, be, nu):
        return (jnp.minimum(i, nu[0] - 1), 0)

    def out_map(i, be, nu):
        return (i, 0)

    def w_map(i, be, nu):
        return (layer, be[i], 0, 0)

    return pl.pallas_call(
        functools.partial(_expert_kernel, d_exp=d_exp),
        out_shape=jax.ShapeDtypeStruct((n_rows, d), F32),
        grid_spec=pltpu.PrefetchScalarGridSpec(
            num_scalar_prefetch=2,
            grid=(n_blocks,),
            in_specs=[
                pl.BlockSpec((tm, d), in_map),
                pl.BlockSpec((1, 1, d, d_exp2), w_map),
                pl.BlockSpec((1, 1, 1, d_exp2), w_map),
                pl.BlockSpec((1, 1, d_exp, d), w_map),
                pl.BlockSpec((1, 1, 1, d), w_map),
            ],
            out_specs=pl.BlockSpec((tm, d), out_map),
            scratch_shapes=[pltpu.VMEM((d, d_exp2), BF16), pltpu.VMEM((d_exp, d), BF16)],
        ),
        compiler_params=_cparams(("arbitrary",)),
        name="experts",
    )(block_expert, n_used, buf, w1, b1.reshape(b1.shape[0], n_exp, 1, d_exp2),
      w2, b2.reshape(b2.shape[0], n_exp, 1, d))


def _combine_kernel(src_ref, dst_ref, len_ref, y_hbm, lpos_ref, gate_ref, x_ref, gt_ref, fg_ref,
                    o_ref, ycomp_ref, sem_seg, *, ts, lrows, n_exp, n_tiles, final_norm):
    i = pl.program_id(0)
    slot = i % 2

    def seg_copy_into(buf_slot):
        def seg_copy(local_row, global_row, size):
            return pltpu.make_async_copy(y_hbm.at[pl.ds(global_row, size)],
                                         ycomp_ref.at[buf_slot, pl.ds(local_row, size)],
                                         sem_seg.at[buf_slot])
        return seg_copy

    @pl.when(i == 0)
    def _():
        ycomp_ref[...] = jnp.zeros_like(ycomp_ref)
        _segment_copies(0, n_exp, ts, src_ref, dst_ref, len_ref, seg_copy_into(0))

    @pl.when(i + 1 < n_tiles)
    def _():
        _segment_copies(i + 1, n_exp, ts, src_ref, dst_ref, len_ref, seg_copy_into(1 - slot))

    lpos = lpos_ref[...]
    gates = gate_ref[...]
    cols = lax.broadcasted_iota(jnp.int32, (ts, lrows), 1)
    sel = jnp.zeros((ts, lrows), F32)
    for k in range(TOP_K):
        sel = jnp.where(cols == lpos[:, k:k + 1], gates[:, k:k + 1], sel)
    sel = sel.astype(BF16)

    _segment_wait(i, n_exp, lrows, src_ref, len_ref, seg_copy_into(slot))
    acc = _dot(sel, ycomp_ref[slot].astype(BF16))
    out = x_ref[...] + gt_ref[0] * acc
    if final_norm:
        ms = jnp.mean(out * out, axis=-1, keepdims=True)
        out = out * lax.rsqrt(ms + EPS) * fg_ref[...]
    o_ref[...] = out


def _combine(seg_src, seg_dst, seg_len, y_buf, lpos_t, gates_t, x2d, gt, final_g, s, final_norm,
             ts, lrows, n_exp):
    t, d = x2d.shape
    per_b = s // ts
    tile_map = lambda i, *_: (i, 0)
    return pl.pallas_call(
        functools.partial(_combine_kernel, ts=ts, lrows=lrows, n_exp=n_exp, n_tiles=t // ts,
                          final_norm=final_norm),
        out_shape=jax.ShapeDtypeStruct((t, d), F32),
        grid_spec=pltpu.PrefetchScalarGridSpec(
            num_scalar_prefetch=3,
            grid=(t // ts,),
            in_specs=[
                pl.BlockSpec(memory_space=pl.ANY),
                pl.BlockSpec((ts, TOP_K), tile_map),
                pl.BlockSpec((ts, TOP_K), tile_map),
                pl.BlockSpec((ts, d), tile_map),
                pl.BlockSpec((1, 1, d), lambda i, *_: (i // per_b, 0, 0)),
                pl.BlockSpec((1, d), lambda i, *_: (0, 0)),
            ],
            out_specs=pl.BlockSpec((ts, d), tile_map),
            scratch_shapes=[
                pltpu.VMEM((2, lrows, d), F32),
                pltpu.SemaphoreType.DMA((2,)),
            ],
        ),
        compiler_params=_cparams(("arbitrary",)),
        name="combine",
    )(seg_src, seg_dst, seg_len, y_buf, lpos_t, gates_t, x2d, gt, final_g.reshape(1, d))


def _moe_layer(x, sh, sc, gt, norm_g, router_w, router_b, w1, b1, w2, b2, layer, final_g, final_norm):
    bsz, s, d = x.shape
    t = bsz * s
    n_exp = router_w.shape[1]
    tm = EXPERT_ROWS
    ts = min(MOE_TILE, s)
    n_tiles = t // ts
    lrows = -(-(ts * TOP_K + n_exp * (SEG_ALIGN - 1)) // MXU_ROWS) * MXU_ROWS
    h, lpos, gates, tab, tot = _route(x, sh, sc, norm_g, router_w, router_b, ts)

    rows = tot[:, 0].astype(jnp.int32)
    padded = (rows + tm - 1) // tm * tm
    pends = jnp.cumsum(padded).astype(jnp.int32)
    pstart = pends - padded
    n_blocks = -(-(t * TOP_K + n_tiles * n_exp * (SEG_ALIGN - 1) + n_exp * (tm - 1)) // tm)
    n_used = pends[-1] // tm
    blk = jnp.minimum(jnp.arange(n_blocks, dtype=jnp.int32), n_used - 1)
    block_expert = jnp.minimum(
        jnp.sum((pends[None, :] <= (blk * tm)[:, None]).astype(jnp.int32), axis=1), n_exp - 1)
    tab = tab[:, :, :3].astype(jnp.int32)
    seg_src = tab[:, :, 0].reshape(-1)
    seg_dst = (pstart[None, :] + tab[:, :, 1]).reshape(-1)
    seg_len = tab[:, :, 2].reshape(-1)

    buf = _dispatch(pstart, pends, seg_src, seg_dst, seg_len, lpos, h.reshape(t, d),
                    n_blocks * tm, ts, lrows)
    y_buf = _experts(block_expert, n_used.reshape(1), buf, w1, b1, w2, b2, layer)
    out = _combine(seg_src, seg_dst, seg_len, y_buf, lpos.T, gates.T, x.reshape(t, d), gt, final_g,
                   s, final_norm, ts, lrows, n_exp)
    return out.reshape(bsz, s, d)


def _head_rms(x, seg, seg_t, inv_dh):
    hi, lo = _split2(x * x)
    ssum = _dot(hi, seg) + _dot(lo, seg)
    inv = lax.rsqrt(ssum * inv_dh + EPS)
    ihi, ilo = _split2(inv)
    return _dot(ihi, seg_t) + _dot(ilo, seg_t)


def _kvq_kernel(x_ref, ksh_ref, ksc_ref, msh_ref, msc_ref, kg_ref, mg_ref,
                wk_ref, wvt_ref, wf_ref, fb_ref, kng_ref, qng_ref, wq_ref, wg_ref,
                place_ref, cst_ref,
                k_ref, kx_ref, vt_ref, q_ref, qx_ref, sg_ref, ccol_ref, *, ts, d):
    @pl.when(pl.program_id(1) == 0)
    def _():
        ccol_ref[...] = jnp.zeros_like(ccol_ref)

    x = x_ref[0]
    ms = jnp.mean(x * x, axis=-1, keepdims=True)
    xn = x * lax.rsqrt(ms + EPS)
    hk = ((xn * kg_ref[...]) * (1.0 + ksc_ref[0]) + ksh_ref[0]).astype(BF16)
    hq = ((xn * mg_ref[...]) * (1.0 + msc_ref[0]) + msh_ref[0]).astype(BF16)

    di = lax.broadcasted_iota(jnp.int32, (d, LANES), 0) // HEAD_DIM
    hi_ = lax.broadcasted_iota(jnp.int32, (d, LANES), 1)
    seg = jnp.where(di == hi_, 1.0, 0.0).astype(BF16)
    dt = lax.broadcasted_iota(jnp.int32, (LANES, d), 1) // HEAD_DIM
    ht = lax.broadcasted_iota(jnp.int32, (LANES, d), 0)
    seg_t = jnp.where(dt == ht, 1.0, 0.0).astype(BF16)
    inv_dh = 1.0 / HEAD_DIM

    k = _dot(hk, wk_ref[...])
    k_ref[0] = (k * _head_rms(k, seg, seg_t, inv_dh) * kng_ref[...]).astype(BF16)
    vt_ref[0] = _dot_nt(wvt_ref[...], hk).astype(BF16)

    q = _dot(hq, wq_ref[...])
    qn = q * _head_rms(q, seg, seg_t, inv_dh) * qng_ref[...]
    q_ref[0] = (qn * (HEAD_DIM ** -0.5 * LOG2E)).astype(BF16)
    sg_ref[0] = jax.nn.sigmoid(_dot(hq, wg_ref[...])).astype(BF16)

    lf = jax.nn.log_sigmoid(_dot(hk, wf_ref[...]) + fb_ref[...])
    r = lax.broadcasted_iota(jnp.int32, (ts, ts), 0)
    c = lax.broadcasted_iota(jnp.int32, (ts, ts), 1)
    lower = jnp.where(c <= r, 1.0, 0.0).astype(BF16)
    a, b, e = _split3(lf)
    cum = ccol_ref[...] + (_dot(lower, a) + _dot(lower, b) + _dot(lower, e))
    ccol_ref[...] = cum[ts - 1:ts, :]
    a, b, e = _split3(cum * LOG2E)
    extra = (_dot(a, place_ref[0]) + _dot(b, place_ref[1]) + _dot(e, place_ref[2])
             + cst_ref[...]).astype(BF16)
    kx_ref[0] = extra[:, :LANES]
    qx_ref[0] = extra[:, LANES:]


def _bias_lane_tables(n_heads):
    assert 2 * N_PARTS * n_heads <= LANES
    place = np.zeros((N_PARTS, LANES, 2 * LANES), np.float32)
    cst = np.zeros((1, 2 * LANES), np.float32)
    for h in range(n_heads):
        base = 2 * N_PARTS * h
        for part in range(N_PARTS):
            place[part, h, base + part] = 1.0
            place[part, h, LANES + base + N_PARTS + part] = 1.0
        cst[0, base + N_PARTS:base + 2 * N_PARTS] = 1.0
        cst[0, LANES + base:LANES + base + N_PARTS] = -1.0
    return jnp.asarray(place, BF16), jnp.asarray(cst, F32)


def _kvq(x, ksh, ksc, msh, msc, kv_norm_g, mix_norm_g, kv_w, k_norm_g, fgate_b,
         w_qg, q_norm_g, ts=256):
    bsz, s, d = x.shape
    n_heads = d // HEAD_DIM
    ts = min(ts, s)
    wk = kv_w[:, :d].astype(BF16)
    wvt = kv_w[:, d:2 * d].T.astype(BF16)
    wf = kv_w[:, 2 * d:]
    wf_pad = jnp.pad(wf, ((0, 0), (0, LANES - n_heads))).astype(BF16)
    fb_pad = jnp.pad(fgate_b, (0, LANES - n_heads)).reshape(1, LANES)
    place, cst = _bias_lane_tables(n_heads)
    vec = pl.BlockSpec((1, 1, d), lambda b, i: (b, 0, 0))
    c2 = lambda shape: pl.BlockSpec(shape, lambda b, i: (0, 0))
    tile = pl.BlockSpec((1, ts, d), lambda b, i: (b, i, 0))
    tile_x = pl.BlockSpec((1, ts, LANES), lambda b, i: (b, i, 0))
    return pl.pallas_call(
        functools.partial(_kvq_kernel, ts=ts, d=d),
        out_shape=(
            jax.ShapeDtypeStruct((bsz, s, d), BF16),
            jax.ShapeDtypeStruct((bsz, s, LANES), BF16),
            jax.ShapeDtypeStruct((bsz, d, s), BF16),
            jax.ShapeDtypeStruct((bsz, s, d), BF16),
            jax.ShapeDtypeStruct((bsz, s, LANES), BF16),
            jax.ShapeDtypeStruct((bsz, s, d), BF16),
        ),
        grid=(bsz, s // ts),
        in_specs=[
            tile, vec, vec, vec, vec,
            c2((1, d)), c2((1, d)),
            c2((d, d)), c2((d, d)), c2((d, LANES)), c2((1, LANES)),
            c2((1, d)), c2((1, d)),
            c2((d, d)), c2((d, d)),
            pl.BlockSpec((N_PARTS, LANES, 2 * LANES), lambda b, i: (0, 0, 0)),
            c2((1, 2 * LANES)),
        ],
        out_specs=(
            tile, tile_x,
            pl.BlockSpec((1, d, ts), lambda b, i: (b, 0, i)),
            tile, tile_x,
            tile,
        ),
        scratch_shapes=[pltpu.VMEM((1, LANES), F32)],
        compiler_params=_cparams(("arbitrary", "arbitrary")),
        name="kvq",
    )(x, ksh, ksc, msh, msc, kv_norm_g.reshape(1, d), mix_norm_g.reshape(1, d),
      wk, wvt, wf_pad, fb_pad,
      jnp.tile(k_norm_g, n_heads).reshape(1, d), jnp.tile(q_norm_g, n_heads).reshape(1, d),
      w_qg[:, :d].astype(BF16), w_qg[:, d:].astype(BF16), place, cst)


def _attn_kernel(q_ref, qx_ref, k_ref, kx_ref, vt_ref, o_ref, qh_ref, s0_ref, s1_ref, acc_ref,
                 *, tq, tk, npp):
    nh = 2 * npp
    first_pair = pl.program_id(1) * npp
    qi = pl.program_id(2)
    lane = lax.broadcasted_iota(jnp.int32, (tq, 2 * LANES), 1)
    for pp in range(npp):
        q = jnp.concatenate([q_ref[0, :, pp * LANES:(pp + 1) * LANES], qx_ref[0]], axis=1)
        zero = jnp.zeros_like(q)
        for hh in range(2):
            feat = jnp.logical_and(lane >= hh * HEAD_DIM, lane < (hh + 1) * HEAD_DIM)
            first_bias = LANES + 2 * N_PARTS * (2 * (first_pair + pp) + hh)
            bias = jnp.logical_and(lane >= first_bias, lane < first_bias + 2 * N_PARTS)
            qh_ref[2 * pp + hh] = jnp.where(feat | bias, q, zero)
    acc_ref[...] = jnp.zeros_like(acc_ref)
    ones = jnp.ones((DENOM_ROWS, tk), BF16)

    def scores_into(kb, s_ref):
        k0 = pl.multiple_of(kb * tk, tk)
        kx = kx_ref[0, pl.ds(k0, tk), :]
        for pp in range(npp):
            ks = jnp.concatenate([k_ref[0, pl.ds(k0, tk), pp * LANES:(pp + 1) * LANES], kx], axis=1)
            for hh in range(2):
                s_ref[2 * pp + hh] = _dot_nt(ks, qh_ref[2 * pp + hh])

    def softmax_pv(kb, s_ref, ms, masked):
        k0 = pl.multiple_of(kb * tk, tk)
        vt = vt_ref[0, :, pl.ds(k0, tk)]
        if masked:
            keep = (lax.broadcasted_iota(jnp.int32, (tk, tq), 0)
                    <= lax.broadcasted_iota(jnp.int32, (tk, tq), 1))
        out = []
        for h in range(nh):
            st = s_ref[h]
            if masked:
                st = jnp.where(keep, st, NEG_INF)
            m_new = jnp.maximum(ms[h], jnp.max(st, axis=0, keepdims=True))
            p = jnp.exp2(st - m_new).astype(BF16)
            alpha = jnp.exp2(ms[h] - m_new)
            vth = jnp.concatenate([vt[h * HEAD_DIM:(h + 1) * HEAD_DIM, :], ones], axis=0)
            acc_ref[h] = alpha * acc_ref[h] + _dot(vth, p)
            out.append(m_new)
        return tuple(out)

    def two_blocks(j, ms):
        scores_into(2 * j + 1, s1_ref)
        ms = softmax_pv(2 * j, s0_ref, ms, False)
        scores_into(2 * j + 2, s0_ref)
        return softmax_pv(2 * j + 1, s1_ref, ms, False)

    scores_into(0, s0_ref)
    ms = lax.fori_loop(0, qi // 2, two_blocks,
                       tuple(jnp.full((1, tq), NEG_INF, F32) for _ in range(nh)))

    @pl.when(qi % 2 == 1)
    def _():
        scores_into(qi, s1_ref)
        softmax_pv(qi, s1_ref, softmax_pv(qi - 1, s0_ref, ms, False), True)

    @pl.when(qi % 2 == 0)
    def _():
        softmax_pv(qi, s0_ref, ms, True)

    ot = jnp.concatenate(
        [acc_ref[h, :HEAD_DIM, :] / acc_ref[h, HEAD_DIM:HEAD_DIM + 1, :] for h in range(nh)],
        axis=0)
    o_ref[0] = ot.T.astype(BF16)


def _attention(q, qx, k, kx, vt, tq=512, npp=4):
    bsz, s, d = q.shape
    n_pairs = d // LANES
    npp = min(npp, n_pairs)
    tq = min(tq, s)
    tk = tq
    w = npp * LANES
    return pl.pallas_call(
        functools.partial(_attn_kernel, tq=tq, tk=tk, npp=npp),
        out_shape=jax.ShapeDtypeStruct((bsz, s, d), BF16),
        grid=(bsz, n_pairs // npp, s // tq),
        in_specs=[
            pl.BlockSpec((1, tq, w), lambda b, j, i: (b, i, j)),
            pl.BlockSpec((1, tq, LANES), lambda b, j, i: (b, i, 0)),
            pl.BlockSpec((1, s, w), lambda b, j, i: (b, 0, j)),
            pl.BlockSpec((1, s, LANES), lambda b, j, i: (b, 0, 0)),
            pl.BlockSpec((1, w, s), lambda b, j, i: (b, j, 0)),
        ],
        out_specs=pl.BlockSpec((1, tq, w), lambda b, j, i: (b, i, j)),
        scratch_shapes=[
            pltpu.VMEM((2 * npp, tq, 2 * LANES), BF16),
            pltpu.VMEM((2 * npp, tk, tq), F32),
            pltpu.VMEM((2 * npp, tk, tq), F32),
            pltpu.VMEM((2 * npp, HEAD_DIM + DENOM_ROWS, tq), F32),
        ],
        compiler_params=_cparams(("parallel", "parallel", "arbitrary")),
        name="attn",
    )(q, qx, k, kx, vt)


def _attnout_kernel(x_ref, o_ref, sg_ref, gt_ref, wo_ref, out_ref):
    gated = o_ref[0].astype(F32) * sg_ref[0].astype(F32)
    y = _dot(gated.astype(BF16), wo_ref[...])
    out_ref[0] = x_ref[0] + gt_ref[0] * y


def _attnout(x, o, sg, gt, w_o, ts=512):
    bsz, s, d = x.shape
    ts = min(ts, s)
    tile = pl.BlockSpec((1, ts, d), lambda b, i: (b, i, 0))
    return pl.pallas_call(
        _attnout_kernel,
        out_shape=jax.ShapeDtypeStruct((bsz, s, d), F32),
        grid=(bsz, s // ts),
        in_specs=[tile, tile, tile,
                  pl.BlockSpec((1, 1, d), lambda b, i: (b, 0, 0)),
                  pl.BlockSpec((d, d), lambda b, i: (0, 0))],
        out_specs=tile,
        compiler_params=_cparams(("parallel", "parallel")),
        name="attnout",
    )(x, o, sg, gt, w_o.astype(BF16))


def _chunks(mod, n):
    return [m[:, None, :] for m in jnp.split(mod, n, axis=-1)]


def kernel(x, c, ada_w, ada_b, norm_mix_g, norm_ffn_g, gmlp_w_in, gmlp_v_g, gmlp_v_b, gmlp_w_s,
           gmlp_b_s, gmlp_w_out, kv_ada_w, kv_ada_b, kv_norm_g, kv_w, k_norm_g, fgate_b, fox_w_qg,
           q_norm_g, fox_w_o, router_w, router_b, exp_w1, exp_b1, exp_w2, exp_b2, final_g):
    mods = _ada(c, ada_w, ada_b)
    kv_mod = _ada(c, kv_ada_w[None], kv_ada_b[None])[0]

    sh_m, sc_m, gt_m, sh_f, sc_f, gt_f = _chunks(mods[0], 6)
    x = _gmlp(x, sh_m, sc_m, gt_m, norm_mix_g[0], gmlp_w_in[0], gmlp_v_g[0], gmlp_v_b[0],
              gmlp_w_s[0], gmlp_b_s[0], gmlp_w_out[0])
    x = _moe_layer(x, sh_f, sc_f, gt_f, norm_ffn_g[0], router_w[0], router_b[0],
                   exp_w1, exp_b1, exp_w2, exp_b2, 0, final_g, False)

    sh_m, sc_m, gt_m, sh_f, sc_f, gt_f = _chunks(mods[1], 6)
    ksh, ksc = _chunks(kv_mod, 2)
    k, kx, vt, q, qx, sg = _kvq(x, ksh, ksc, sh_m, sc_m, kv_norm_g, norm_mix_g[1], kv_w,
                                k_norm_g, fgate_b, fox_w_qg[0], q_norm_g[0])
    o = _attention(q, qx, k, kx, vt)
    x = _attnout(x, o, sg, gt_m, fox_w_o[0])
    return _moe_layer(x, sh_f, sc_f, gt_f, norm_ffn_g[1], router_w[1], router_b[1],
                      exp_w1, exp_b1, exp_w2, exp_b2, 1, final_g, True)
```

```python
import functools

import numpy as np
import jax
import jax.numpy as jnp
from jax import lax
from jax.experimental import pallas as pl
from jax.experimental.pallas import tpu as pltpu

F32 = jnp.float32
BF16 = jnp.bfloat16

CHUNK = 128
GMLP_GROUPS = 8
HEAD_DIM = 64
TOP_K = 4
SWIGLU_LIMIT = 7.0
SWIGLU_ALPHA = 1.702
EPS = 1e-6
NEG_INF = -1e30
LOG2E = 1.4426950408889634
N_PARTS = 3
DENOM_ROWS = 16

LANES = 128
MXU_ROWS = 256
EXPERT_ROWS = 2 * MXU_ROWS
SEG_ALIGN = 8
MOE_TILE = 512
COMBINE_CHUNK = 128
VMEM_LIMIT = 56 * 1024 * 1024


def _cparams(sem, vmem=VMEM_LIMIT):
    return pltpu.CompilerParams(dimension_semantics=sem, vmem_limit_bytes=vmem)


def _divisor_tile(n, pref):
    if n <= pref:
        return n
    return max(t for t in range(LANES, pref + 1, LANES) if n % t == 0)


def _dot(a, b):
    return jnp.dot(a, b, preferred_element_type=F32)


def _dot_nt(a, b):
    return lax.dot_general(a, b, (((1,), (1,)), ((), ())), preferred_element_type=F32)


def _split3(x):
    hi = x.astype(BF16)
    r = x - hi.astype(F32)
    mid = r.astype(BF16)
    lo = (r - mid.astype(F32)).astype(BF16)
    return hi, mid, lo


def _split2(x):
    hi = x.astype(BF16)
    lo = (x - hi.astype(F32)).astype(BF16)
    return hi, lo


def _gelu_exact(x):
    return 0.5 * x * (1.0 + lax.erf(x * (0.5 ** 0.5)))


def _ada_norm(x, g, shift, scale):
    ms = jnp.mean(x * x, axis=-1, keepdims=True)
    y = x * lax.rsqrt(ms + EPS) * g
    return y * (1.0 + scale) + shift


def _ada_kernel(c_ref, w_ref, b_ref, o_ref):
    c = c_ref[...]
    sc = c * jax.nn.sigmoid(c)
    o_ref[0] = _dot(sc.astype(BF16), w_ref[0].astype(BF16)) + b_ref[0]


def _ada(c, w, b):
    nl, d, n = w.shape
    bsz = c.shape[0]
    tn = _divisor_tile(n, 2048)
    return pl.pallas_call(
        _ada_kernel,
        out_shape=jax.ShapeDtypeStruct((nl, bsz, n), F32),
        grid=(nl, n // tn),
        in_specs=[
            pl.BlockSpec((bsz, d), lambda l, j: (0, 0)),
            pl.BlockSpec((1, d, tn), lambda l, j: (l, 0, j)),
            pl.BlockSpec((1, 1, tn), lambda l, j: (l, 0, j)),
        ],
        out_specs=pl.BlockSpec((1, bsz, tn), lambda l, j: (l, 0, j)),
        compiler_params=_cparams(("arbitrary", "arbitrary")),
        name="ada",
    )(c, w, b.reshape(nl, 1, n))


def _gmlp_kernel(x_ref, sh_ref, sc_ref, gt_ref, g_ref, win_ref, vg_ref, vb_ref,
                 ws_ref, bst_ref, wout_ref, o_ref, gated_ref, *, ts, width):
    x = x_ref[0]
    h = _ada_norm(x, g_ref[...], sh_ref[0], sc_ref[0]).astype(BF16)
    z = _gelu_exact(_dot(h, win_ref[...]))
    u = z[:, :width]
    v = z[:, width:]
    mu = jnp.mean(v, axis=-1, keepdims=True)
    vc = v - mu
    var = jnp.mean(vc * vc, axis=-1, keepdims=True)
    vn = (vc * lax.rsqrt(var + EPS) * vg_ref[...] + vb_ref[...]).astype(BF16)

    gd = width // GMLP_GROUPS
    row = lax.broadcasted_iota(jnp.int32, (CHUNK, CHUNK), 0)
    col = lax.broadcasted_iota(jnp.int32, (CHUNK, CHUNK), 1)
    causal = col <= row
    bst = bst_ref[...]
    for g in range(GMLP_GROUPS):
        wc = jnp.where(causal, ws_ref[g], 0.0).astype(BF16)
        bias = bst[:, g:g + 1]
        for ci in range(ts // CHUNK):
            rs = slice(ci * CHUNK, (ci + 1) * CHUNK)
            cs = slice(g * gd, (g + 1) * gd)
            mixed = _dot(wc, vn[rs, cs]) + bias
            gated_ref[rs, cs] = (u[rs, cs] * mixed).astype(BF16)
    y = _dot(gated_ref[...], wout_ref[...])
    o_ref[0] = x + gt_ref[0] * y


def _gmlp(x, sh, sc, gt, g, w_in, v_g, v_b, w_s, b_s, w_out, ts=512):
    bsz, s, d = x.shape
    width = w_out.shape[0]
    ts = min(ts, s)
    vec = pl.BlockSpec((1, 1, d), lambda b, i: (b, 0, 0))
    const2 = lambda shape: pl.BlockSpec(shape, lambda b, i: (0, 0))
    return pl.pallas_call(
        functools.partial(_gmlp_kernel, ts=ts, width=width),
        out_shape=jax.ShapeDtypeStruct((bsz, s, d), F32),
        grid=(bsz, s // ts),
        in_specs=[
            pl.BlockSpec((1, ts, d), lambda b, i: (b, i, 0)),
            vec, vec, vec,
            const2((1, d)),
            const2((d, 2 * width)),
            const2((1, width)),
            const2((1, width)),
            pl.BlockSpec((GMLP_GROUPS, CHUNK, CHUNK), lambda b, i: (0, 0, 0)),
            const2((CHUNK, GMLP_GROUPS)),
            const2((width, d)),
        ],
        out_specs=pl.BlockSpec((1, ts, d), lambda b, i: (b, i, 0)),
        scratch_shapes=[pltpu.VMEM((ts, width), BF16)],
        compiler_params=_cparams(("parallel", "parallel")),
        name="gmlp",
    )(x, sh, sc, gt, g.reshape(1, d), w_in.astype(BF16), v_g.reshape(1, width),
      v_b.reshape(1, width), w_s, b_s.T, w_out.astype(BF16))


def _route_kernel(x_ref, sh_ref, sc_ref, g_ref, rwt_ref, rb_ref,
                  h_ref, lpos_ref, gate_ref, tab_ref, tot_ref, carry_ref, *, ts, n_exp):
    _route_tile(x_ref[0], sh_ref, sc_ref, g_ref, rwt_ref, rb_ref,
                h_ref, lpos_ref, gate_ref, tab_ref, tot_ref, carry_ref, ts, n_exp)


def _mixroute_kernel(x_ref, o_ref, sg_ref, gt_ref, wo_ref, sh_ref, sc_ref, g_ref, rwt_ref, rb_ref,
                     xo_ref, h_ref, lpos_ref, gate_ref, tab_ref, tot_ref, carry_ref, *, ts, n_exp):
    gated = o_ref[0].astype(F32) * sg_ref[0].astype(F32)
    x = x_ref[0] + gt_ref[0] * _dot(gated.astype(BF16), wo_ref[...])
    xo_ref[0] = x
    _route_tile(x, sh_ref, sc_ref, g_ref, rwt_ref, rb_ref,
                h_ref, lpos_ref, gate_ref, tab_ref, tot_ref, carry_ref, ts, n_exp)


def _route_tile(x, sh_ref, sc_ref, g_ref, rwt_ref, rb_ref,
                h_ref, lpos_ref, gate_ref, tab_ref, tot_ref, carry_ref, ts, n_exp):
    first = jnp.logical_and(pl.program_id(0) == 0, pl.program_id(1) == 0)

    @pl.when(first)
    def _():
        carry_ref[...] = jnp.zeros_like(carry_ref)

    h = _ada_norm(x, g_ref[...], sh_ref[0], sc_ref[0]).astype(BF16)
    h_ref[0] = h
    logits = _dot_nt(rwt_ref[...], h) + rb_ref[...]

    eio = lax.broadcasted_iota(jnp.int32, (n_exp, ts), 0)
    work = logits
    vals, hots = [], []
    for _ in range(TOP_K):
        m = jnp.max(work, axis=0, keepdims=True)
        sel = jnp.min(jnp.where(work == m, eio, n_exp), axis=0, keepdims=True)
        hot = eio == sel
        vals.append(m)
        hots.append(hot)
        work = jnp.where(hot, -jnp.inf, work)

    exps = [jnp.exp(v - vals[0]) for v in vals]
    denom = exps[0]
    for e in exps[1:]:
        denom = denom + e
    gates = [e / denom for e in exps]

    chosen = hots[0]
    for hot in hots[1:]:
        chosen = jnp.logical_or(chosen, hot)
    chosen_f = jnp.where(chosen, 1.0, 0.0)
    r = lax.broadcasted_iota(jnp.int32, (ts, ts), 0)
    c = lax.broadcasted_iota(jnp.int32, (ts, ts), 1)
    before = jnp.where(r < c, 1.0, 0.0).astype(BF16)
    prefix = _dot(chosen_f.astype(BF16), before)
    count = jnp.sum(chosen_f, axis=1, keepdims=True)
    seg_len = jnp.floor((count + (SEG_ALIGN - 1)) * (1.0 / SEG_ALIGN)) * SEG_ALIGN
    er = lax.broadcasted_iota(jnp.int32, (n_exp, n_exp), 0)
    ec = lax.broadcasted_iota(jnp.int32, (n_exp, n_exp), 1)
    lower = jnp.where(ec < er, 1.0, 0.0).astype(BF16)
    seg_start = _dot(lower, jnp.broadcast_to(seg_len, (n_exp, LANES)).astype(BF16))[:, 0:1]
    base = seg_start + prefix
    lpos = [jnp.sum(jnp.where(hot, base, 0.0), axis=0, keepdims=True) for hot in hots]

    lpos_ref[...] = jnp.concatenate(lpos, axis=0).astype(jnp.int32)
    gate_ref[...] = jnp.concatenate(gates, axis=0)
    used = carry_ref[...]
    lane = lax.broadcasted_iota(jnp.int32, (n_exp, LANES), 1)
    tab_ref[0] = jnp.where(lane == 0, seg_start,
                           jnp.where(lane == 1, used, jnp.where(lane == 2, seg_len, 0.0)))
    carry_ref[...] = used + seg_len
    tot_ref[...] = jnp.broadcast_to(used + seg_len, tot_ref.shape)


def _route(x, sh, sc, g, router_w, router_b, ts, mix=None):
    bsz, s, d = x.shape
    n_exp = router_w.shape[1]
    nt = s // ts
    t = bsz * s
    vec = pl.BlockSpec((1, 1, d), lambda b, i: (b, 0, 0))
    tile = pl.BlockSpec((1, ts, d), lambda b, i: (b, i, 0))
    slot = pl.BlockSpec((TOP_K, ts), lambda b, i: (0, b * nt + i))
    out_shape = (
        jax.ShapeDtypeStruct((bsz, s, d), BF16),
        jax.ShapeDtypeStruct((TOP_K, t), jnp.int32),
        jax.ShapeDtypeStruct((TOP_K, t), F32),
        jax.ShapeDtypeStruct((bsz * nt, n_exp, LANES), F32),
        jax.ShapeDtypeStruct((n_exp, LANES), F32),
    )
    out_specs = (
        tile, slot, slot,
        pl.BlockSpec((1, n_exp, LANES), lambda b, i: (b * nt + i, 0, 0)),
        pl.BlockSpec((n_exp, LANES), lambda b, i: (0, 0)),
    )
    in_specs = [
        vec, vec,
        pl.BlockSpec((1, d), lambda b, i: (0, 0)),
        pl.BlockSpec((n_exp, d), lambda b, i: (0, 0)),
        pl.BlockSpec((n_exp, 1), lambda b, i: (0, 0)),
    ]
    args = (sh, sc, g.reshape(1, d), router_w.T.astype(BF16), router_b.reshape(n_exp, 1))
    if mix is None:
        body, name = _route_kernel, "route"
        in_specs = [tile] + in_specs
        args = (x,) + args
    else:
        o, sg, gt, w_o = mix
        body, name = _mixroute_kernel, "mixroute"
        in_specs = [tile, tile, tile, vec, pl.BlockSpec((d, d), lambda b, i: (0, 0))] + in_specs
        args = (x, o, sg, gt, w_o.astype(BF16)) + args
        out_shape = (jax.ShapeDtypeStruct((bsz, s, d), F32),) + out_shape
        out_specs = (tile,) + out_specs
    return pl.pallas_call(
        functools.partial(body, ts=ts, n_exp=n_exp),
        out_shape=out_shape,
        grid=(bsz, nt),
        in_specs=in_specs,
        out_specs=out_specs,
        scratch_shapes=[pltpu.VMEM((n_exp, 1), F32)],
        compiler_params=_cparams(("arbitrary", "arbitrary")),
        name=name,
    )(*args)


def _wait_rows(src, dst, sem):
    pltpu.make_async_copy(src, dst, sem).wait()


def _chunk_sizes(max_rows):
    sizes = []
    size = SEG_ALIGN
    while size <= max_rows:
        sizes.append(size)
        size *= 2
    return sizes


def _for_each_chunk(length, max_rows, fn):
    for size in reversed(_chunk_sizes(max_rows)):
        shift = size.bit_length() - 1
        @pl.when(((length >> shift) & 1) == 1)
        def _():
            fn(pl.multiple_of((length >> (shift + 1)) << (shift + 1), SEG_ALIGN), size)


def _segment_copies(tile, n_exp, max_rows, src_ref, dst_ref, len_ref, make_copy):
    for e in range(n_exp):
        s = tile * n_exp + e
        src, dst = src_ref[s], dst_ref[s]
        _for_each_chunk(len_ref[s], max_rows,
                        lambda off, size: make_copy(pl.multiple_of(src + off, SEG_ALIGN),
                                                    pl.multiple_of(dst + off, SEG_ALIGN),
                                                    size).start())


def _segment_wait(tile, n_exp, lrows, src_ref, len_ref, make_copy):
    last = tile * n_exp + n_exp - 1
    total = src_ref[last] + len_ref[last]
    _for_each_chunk(total, lrows, lambda off, size: make_copy(0, 0, size).wait())


def _dispatch_kernel(pstart_ref, pend_ref, src_ref, dst_ref, len_ref, lpos_ref, h_ref, buf_hbm,
                     comp_ref, zero_ref, sem_seg, sem_zero,
                     *, ts, lrows, tm, n_exp, n_blocks, n_tiles):
    i = pl.program_id(0)
    slot = i % 2

    def seg_copy_from(buf_slot):
        def seg_copy(local_row, global_row, size):
            return pltpu.make_async_copy(comp_ref.at[buf_slot, pl.ds(local_row, size)],
                                         buf_hbm.at[pl.ds(global_row, size)], sem_seg.at[buf_slot])
        return seg_copy

    @pl.when(i == 0)
    def _():
        zero_ref[...] = jnp.zeros_like(zero_ref)
        for e in range(n_exp):
            @pl.when(pend_ref[e] > pstart_ref[e])
            def _():
                last = pl.multiple_of(pend_ref[e] - tm, tm)
                pltpu.make_async_copy(zero_ref, buf_hbm.at[pl.ds(last, tm)], sem_zero).start()
        n_used = pend_ref[n_exp - 1] // tm

        def zero_tail(b, carry):
            off = pl.multiple_of(b * tm, tm)
            pltpu.make_async_copy(zero_ref, buf_hbm.at[pl.ds(off, tm)], sem_zero).start()
            return carry

        def wait_tail(b, carry):
            _wait_rows(zero_ref, buf_hbm.at[pl.ds(0, tm)], sem_zero)
            return carry

        lax.fori_loop(n_used, n_blocks, zero_tail, 0)
        for e in range(n_exp):
            @pl.when(pend_ref[e] > pstart_ref[e])
            def _():
                _wait_rows(zero_ref, buf_hbm.at[pl.ds(0, tm)], sem_zero)
        lax.fori_loop(n_used, n_blocks, wait_tail, 0)

    @pl.when(i > 1)
    def _():
        _segment_wait(i - 2, n_exp, lrows, src_ref, len_ref, seg_copy_from(slot))

    lpos = lpos_ref[...]
    rows = lax.broadcasted_iota(jnp.int32, (MXU_ROWS, ts), 0)
    h = h_ref[...]
    for c in range(lrows // MXU_ROWS):
        rel = lpos - c * MXU_ROWS
        hit = rows == rel[0:1, :]
        for k in range(1, TOP_K):
            hit = jnp.logical_or(hit, rows == rel[k:k + 1, :])
        onehot = jnp.where(hit, 1.0, 0.0).astype(BF16)
        comp_ref[slot, c * MXU_ROWS:(c + 1) * MXU_ROWS, :] = _dot(onehot, h)
    _segment_copies(i, n_exp, ts, src_ref, dst_ref, len_ref, seg_copy_from(slot))

    @pl.when(i == n_tiles - 1)
    def _():
        @pl.when(i > 0)
        def _():
            _segment_wait(i - 1, n_exp, lrows, src_ref, len_ref, seg_copy_from(1 - slot))
        _segment_wait(i, n_exp, lrows, src_ref, len_ref, seg_copy_from(slot))


def _dispatch(pstart, pends, seg_src, seg_dst, seg_len, lpos, h, n_rows, ts, lrows):
    t, d = h.shape
    tm = EXPERT_ROWS
    n_tiles = t // ts
    tile_map = lambda i, *_: (i, 0)
    return pl.pallas_call(
        functools.partial(_dispatch_kernel, ts=ts, lrows=lrows, tm=tm, n_exp=pstart.shape[0],
                          n_blocks=n_rows // tm, n_tiles=n_tiles),
        out_shape=jax.ShapeDtypeStruct((n_rows, d), F32),
        grid_spec=pltpu.PrefetchScalarGridSpec(
            num_scalar_prefetch=5,
            grid=(n_tiles,),
            in_specs=[
                pl.BlockSpec((TOP_K, ts), lambda i, *_: (0, i)),
                pl.BlockSpec((ts, d), tile_map),
            ],
            out_specs=pl.BlockSpec(memory_space=pl.ANY),
            scratch_shapes=[
                pltpu.VMEM((2, lrows, d), F32),
                pltpu.VMEM((tm, d), F32),
                pltpu.SemaphoreType.DMA((2,)),
                pltpu.SemaphoreType.DMA,
            ],
        ),
        compiler_params=_cparams(("arbitrary",)),
        name="dispatch",
    )(pstart, pends, seg_src, seg_dst, seg_len, lpos, h)


def _expert_kernel(be_ref, nu_ref, x_ref, w1_ref, b1_ref, w2_ref, b2_ref, y_ref,
                   w1b_ref, w2b_ref, *, d_exp):
    i = pl.program_id(0)
    changed = jnp.logical_or(i == 0, be_ref[i] != be_ref[jnp.maximum(i - 1, 0)])

    @pl.when(changed)
    def _():
        w1b_ref[...] = w1_ref[0, 0].astype(BF16)
        w2b_ref[...] = w2_ref[0, 0].astype(BF16)

    @pl.when(i < nu_ref[0])
    def _():
        gu = _dot(x_ref[...].astype(BF16), w1b_ref[...]) + b1_ref[0, 0]
        glu = jnp.minimum(gu[:, :d_exp], SWIGLU_LIMIT)
        lin = jnp.clip(gu[:, d_exp:], -SWIGLU_LIMIT, SWIGLU_LIMIT)
        act = (lin + 1.0) * (glu * jax.nn.sigmoid(SWIGLU_ALPHA * glu))
        y_ref[...] = _dot(act.astype(BF16), w2b_ref[...]) + b2_ref[0, 0]

    @pl.when(i >= nu_ref[0])
    def _():
        y_ref[...] = jnp.zeros_like(y_ref)


def _experts(block_expert, n_used, buf, w1, b1, w2, b2, layer):
    n_rows, d = buf.shape
    _, n_exp, _, d_exp2 = w1.shape
    d_exp = d_exp2 // 2
    tm = EXPERT_ROWS
    n_blocks = n_rows // tm

    def in_map(i, be, nu):
        return (jnp.minimum(i, nu[0] - 1), 0)

    def out_map(i, be, nu):
        return (i, 0)

    def w_map(i, be, nu):
        return (layer, be[i], 0, 0)

    return pl.pallas_call(
        functools.partial(_expert_kernel, d_exp=d_exp),
        out_shape=jax.ShapeDtypeStruct((n_rows, d), F32),
        grid_spec=pltpu.PrefetchScalarGridSpec(
            num_scalar_prefetch=2,
            grid=(n_blocks,),
            in_specs=[
                pl.BlockSpec((tm, d), in_map),
                pl.BlockSpec((1, 1, d, d_exp2), w_map),
                pl.BlockSpec((1, 1, 1, d_exp2), w_map),
                pl.BlockSpec((1, 1, d_exp, d), w_map),
                pl.BlockSpec((1, 1, 1, d), w_map),
            ],
            out_specs=pl.BlockSpec((tm, d), out_map),
            scratch_shapes=[pltpu.VMEM((d, d_exp2), BF16), pltpu.VMEM((d_exp, d), BF16)],
        ),
        compiler_params=_cparams(("arbitrary",)),
        name="experts",
    )(block_expert, n_used, buf, w1, b1.reshape(b1.shape[0], n_exp, 1, d_exp2),
      w2, b2.reshape(b2.shape[0], n_exp, 1, d))


def _combine_kernel(src_ref, dst_ref, len_ref, y_hbm, lpos_ref, gate_ref, x_ref, gt_ref, fg_ref,
                    o_ref, ycomp_ref, sem_seg, *, ts, lrows, n_exp, n_tiles, final_norm):
    i = pl.program_id(0)
    slot = i % 2

    def seg_copy_into(buf_slot):
        def seg_copy(local_row, global_row, size):
            return pltpu.make_async_copy(y_hbm.at[pl.ds(global_row, size)],
                                         ycomp_ref.at[buf_slot, pl.ds(local_row, size)],
                                         sem_seg.at[buf_slot])
        return seg_copy

    @pl.when(i == 0)
    def _():
        ycomp_ref[...] = jnp.zeros_like(ycomp_ref)
        _segment_copies(0, n_exp, ts, src_ref, dst_ref, len_ref, seg_copy_into(0))

    @pl.when(i + 1 < n_tiles)
    def _():
        _segment_copies(i + 1, n_exp, ts, src_ref, dst_ref, len_ref, seg_copy_into(1 - slot))

    _segment_wait(i, n_exp, lrows, src_ref, len_ref, seg_copy_into(slot))
    y = ycomp_ref[slot].astype(BF16)

    cols = lax.broadcasted_iota(jnp.int32, (COMBINE_CHUNK, lrows), 1)
    for c in range(ts // COMBINE_CHUNK):
        rs = slice(c * COMBINE_CHUNK, (c + 1) * COMBINE_CHUNK)
        lpos = lpos_ref[rs, :]
        gates = gate_ref[rs, :]
        sel = jnp.zeros((COMBINE_CHUNK, lrows), F32)
        for k in range(TOP_K):
            sel = jnp.where(cols == lpos[:, k:k + 1], gates[:, k:k + 1], sel)
        out = x_ref[rs, :] + gt_ref[0] * _dot(sel.astype(BF16), y)
        if final_norm:
            ms = jnp.mean(out * out, axis=-1, keepdims=True)
            out = out * lax.rsqrt(ms + EPS) * fg_ref[...]
        o_ref[rs, :] = out


def _combine(seg_src, seg_dst, seg_len, y_buf, lpos_t, gates_t, x2d, gt, final_g, s, final_norm,
             ts, lrows, n_exp):
    t, d = x2d.shape
    per_b = s // ts
    tile_map = lambda i, *_: (i, 0)
    return pl.pallas_call(
        functools.partial(_combine_kernel, ts=ts, lrows=lrows, n_exp=n_exp, n_tiles=t // ts,
                          final_norm=final_norm),
        out_shape=jax.ShapeDtypeStruct((t, d), F32),
        grid_spec=pltpu.PrefetchScalarGridSpec(
            num_scalar_prefetch=3,
            grid=(t // ts,),
            in_specs=[
                pl.BlockSpec(memory_space=pl.ANY),
                pl.BlockSpec((ts, TOP_K), tile_map),
                pl.BlockSpec((ts, TOP_K), tile_map),
                pl.BlockSpec((ts, d), tile_map),
                pl.BlockSpec((1, 1, d), lambda i, *_: (i // per_b, 0, 0)),
                pl.BlockSpec((1, d), lambda i, *_: (0, 0)),
            ],
            out_specs=pl.BlockSpec((ts, d), tile_map),
            scratch_shapes=[
                pltpu.VMEM((2, lrows, d), F32),
                pltpu.SemaphoreType.DMA((2,)),
            ],
        ),
        compiler_params=_cparams(("arbitrary",)),
        name="combine",
    )(seg_src, seg_dst, seg_len, y_buf, lpos_t, gates_t, x2d, gt, final_g.reshape(1, d))


def _moe_layer(x, sh, sc, gt, norm_g, router_w, router_b, w1, b1, w2, b2, layer, final_g, final_norm,
               mix=None):
    bsz, s, d = x.shape
    t = bsz * s
    n_exp = router_w.shape[1]
    tm = EXPERT_ROWS
    ts = min(MOE_TILE, s)
    n_tiles = t // ts
    lrows = -(-(ts * TOP_K + n_exp * (SEG_ALIGN - 1)) // MXU_ROWS) * MXU_ROWS
    if mix is None:
        h, lpos, gates, tab, tot = _route(x, sh, sc, norm_g, router_w, router_b, ts)
    else:
        x, h, lpos, gates, tab, tot = _route(x, sh, sc, norm_g, router_w, router_b, ts, mix)

    rows = tot[:, 0].astype(jnp.int32)
    padded = (rows + tm - 1) // tm * tm
    pends = jnp.cumsum(padded).astype(jnp.int32)
    pstart = pends - padded
    n_blocks = -(-(t * TOP_K + n_tiles * n_exp * (SEG_ALIGN - 1) + n_exp * (tm - 1)) // tm)
    n_used = pends[-1] // tm
    blk = jnp.minimum(jnp.arange(n_blocks, dtype=jnp.int32), n_used - 1)
    block_expert = jnp.minimum(
        jnp.sum((pends[None, :] <= (blk * tm)[:, None]).astype(jnp.int32), axis=1), n_exp - 1)
    tab = tab[:, :, :3].astype(jnp.int32)
    seg_src = tab[:, :, 0].reshape(-1)
    seg_dst = (pstart[None, :] + tab[:, :, 1]).reshape(-1)
    seg_len = tab[:, :, 2].reshape(-1)

    buf = _dispatch(pstart, pends, seg_src, seg_dst, seg_len, lpos, h.reshape(t, d),
                    n_blocks * tm, ts, lrows)
    y_buf = _experts(block_expert, n_used.reshape(1), buf, w1, b1, w2, b2, layer)
    out = _combine(seg_src, seg_dst, seg_len, y_buf, lpos.T, gates.T, x.reshape(t, d), gt, final_g,
                   s, final_norm, ts, lrows, n_exp)
    return out.reshape(bsz, s, d)


def _head_rms(x, seg, seg_t, inv_dh):
    hi, lo = _split2(x * x)
    ssum = _dot(hi, seg) + _dot(lo, seg)
    inv = lax.rsqrt(ssum * inv_dh + EPS)
    ihi, ilo = _split2(inv)
    return _dot(ihi, seg_t) + _dot(ilo, seg_t)


def _kvq_kernel(x_ref, ksh_ref, ksc_ref, msh_ref, msc_ref, kg_ref, mg_ref,
                wk_ref, wvt_ref, wf_ref, fb_ref, kng_ref, qng_ref, wq_ref, wg_ref,
                place_ref, cst_ref,
                k_ref, kx_ref, vt_ref, q_ref, qx_ref, sg_ref, ccol_ref, *, ts, d):
    @pl.when(pl.program_id(1) == 0)
    def _():
        ccol_ref[...] = jnp.zeros_like(ccol_ref)

    x = x_ref[0]
    ms = jnp.mean(x * x, axis=-1, keepdims=True)
    xn = x * lax.rsqrt(ms + EPS)
    hk = ((xn * kg_ref[...]) * (1.0 + ksc_ref[0]) + ksh_ref[0]).astype(BF16)
    hq = ((xn * mg_ref[...]) * (1.0 + msc_ref[0]) + msh_ref[0]).astype(BF16)

    di = lax.broadcasted_iota(jnp.int32, (d, LANES), 0) // HEAD_DIM
    hi_ = lax.broadcasted_iota(jnp.int32, (d, LANES), 1)
    seg = jnp.where(di == hi_, 1.0, 0.0).astype(BF16)
    dt = lax.broadcasted_iota(jnp.int32, (LANES, d), 1) // HEAD_DIM
    ht = lax.broadcasted_iota(jnp.int32, (LANES, d), 0)
    seg_t = jnp.where(dt == ht, 1.0, 0.0).astype(BF16)
    inv_dh = 1.0 / HEAD_DIM

    k = _dot(hk, wk_ref[...])
    k_ref[0] = (k * _head_rms(k, seg, seg_t, inv_dh) * kng_ref[...]).astype(BF16)
    vt_ref[0] = _dot_nt(wvt_ref[...], hk).astype(BF16)

    q = _dot(hq, wq_ref[...])
    qn = q * _head_rms(q, seg, seg_t, inv_dh) * qng_ref[...]
    q_ref[0] = (qn * (HEAD_DIM ** -0.5 * LOG2E)).astype(BF16)
    sg_ref[0] = jax.nn.sigmoid(_dot(hq, wg_ref[...])).astype(BF16)

    lf = jax.nn.log_sigmoid(_dot(hk, wf_ref[...]) + fb_ref[...])
    r = lax.broadcasted_iota(jnp.int32, (ts, ts), 0)
    c = lax.broadcasted_iota(jnp.int32, (ts, ts), 1)
    lower = jnp.where(c <= r, 1.0, 0.0).astype(BF16)
    a, b, e = _split3(lf)
    cum = ccol_ref[...] + (_dot(lower, a) + _dot(lower, b) + _dot(lower, e))
    ccol_ref[...] = cum[ts - 1:ts, :]
    a, b, e = _split3(cum * LOG2E)
    extra = (_dot(a, place_ref[0]) + _dot(b, place_ref[1]) + _dot(e, place_ref[2])
             + cst_ref[...]).astype(BF16)
    kx_ref[0] = extra[:, :LANES]
    qx_ref[0] = extra[:, LANES:]


def _bias_lane_tables(n_heads):
    assert 2 * N_PARTS * n_heads <= LANES
    place = np.zeros((N_PARTS, LANES, 2 * LANES), np.float32)
    cst = np.zeros((1, 2 * LANES), np.float32)
    for h in range(n_heads):
        base = 2 * N_PARTS * h
        for part in range(N_PARTS):
            place[part, h, base + part] = 1.0
            place[part, h, LANES + base + N_PARTS + part] = 1.0
        cst[0, base + N_PARTS:base + 2 * N_PARTS] = 1.0
        cst[0, LANES + base:LANES + base + N_PARTS] = -1.0
    return jnp.asarray(place, BF16), jnp.asarray(cst, F32)


def _kvq(x, ksh, ksc, msh, msc, kv_norm_g, mix_norm_g, kv_w, k_norm_g, fgate_b,
         w_qg, q_norm_g, ts=256):
    bsz, s, d = x.shape
    n_heads = d // HEAD_DIM
    ts = min(ts, s)
    wk = kv_w[:, :d].astype(BF16)
    wvt = kv_w[:, d:2 * d].T.astype(BF16)
    wf = kv_w[:, 2 * d:]
    wf_pad = jnp.pad(wf, ((0, 0), (0, LANES - n_heads))).astype(BF16)
    fb_pad = jnp.pad(fgate_b, (0, LANES - n_heads)).reshape(1, LANES)
    place, cst = _bias_lane_tables(n_heads)
    vec = pl.BlockSpec((1, 1, d), lambda b, i: (b, 0, 0))
    c2 = lambda shape: pl.BlockSpec(shape, lambda b, i: (0, 0))
    tile = pl.BlockSpec((1, ts, d), lambda b, i: (b, i, 0))
    tile_x = pl.BlockSpec((1, ts, LANES), lambda b, i: (b, i, 0))
    return pl.pallas_call(
        functools.partial(_kvq_kernel, ts=ts, d=d),
        out_shape=(
            jax.ShapeDtypeStruct((bsz, s, d), BF16),
            jax.ShapeDtypeStruct((bsz, s, LANES), BF16),
            jax.ShapeDtypeStruct((bsz, d, s), BF16),
            jax.ShapeDtypeStruct((bsz, s, d), BF16),
            jax.ShapeDtypeStruct((bsz, s, LANES), BF16),
            jax.ShapeDtypeStruct((bsz, s, d), BF16),
        ),
        grid=(bsz, s // ts),
        in_specs=[
            tile, vec, vec, vec, vec,
            c2((1, d)), c2((1, d)),
            c2((d, d)), c2((d, d)), c2((d, LANES)), c2((1, LANES)),
            c2((1, d)), c2((1, d)),
            c2((d, d)), c2((d, d)),
            pl.BlockSpec((N_PARTS, LANES, 2 * LANES), lambda b, i: (0, 0, 0)),
            c2((1, 2 * LANES)),
        ],
        out_specs=(
            tile, tile_x,
            pl.BlockSpec((1, d, ts), lambda b, i: (b, 0, i)),
            tile, tile_x,
            tile,
        ),
        scratch_shapes=[pltpu.VMEM((1, LANES), F32)],
        compiler_params=_cparams(("arbitrary", "arbitrary")),
        name="kvq",
    )(x, ksh, ksc, msh, msc, kv_norm_g.reshape(1, d), mix_norm_g.reshape(1, d),
      wk, wvt, wf_pad, fb_pad,
      jnp.tile(k_norm_g, n_heads).reshape(1, d), jnp.tile(q_norm_g, n_heads).reshape(1, d),
      w_qg[:, :d].astype(BF16), w_qg[:, d:].astype(BF16), place, cst)


def _attn_kernel(q_ref, qx_ref, k_ref, kx_ref, vt_ref, o_ref, qh_ref, s0_ref, s1_ref, acc_ref,
                 *, tq, tk, npp):
    nh = 2 * npp
    first_pair = pl.program_id(1) * npp
    qi = pl.program_id(2)
    lane = lax.broadcasted_iota(jnp.int32, (tq, 2 * LANES), 1)
    for pp in range(npp):
        q = jnp.concatenate([q_ref[0, :, pp * LANES:(pp + 1) * LANES], qx_ref[0]], axis=1)
        zero = jnp.zeros_like(q)
        for hh in range(2):
            feat = jnp.logical_and(lane >= hh * HEAD_DIM, lane < (hh + 1) * HEAD_DIM)
            first_bias = LANES + 2 * N_PARTS * (2 * (first_pair + pp) + hh)
            bias = jnp.logical_and(lane >= first_bias, lane < first_bias + 2 * N_PARTS)
            qh_ref[2 * pp + hh] = jnp.where(feat | bias, q, zero)
    acc_ref[...] = jnp.zeros_like(acc_ref)
    ones = jnp.ones((DENOM_ROWS, tk), BF16)

    def scores_into(kb, s_ref):
        k0 = pl.multiple_of(kb * tk, tk)
        kx = kx_ref[0, pl.ds(k0, tk), :]
        for pp in range(npp):
            ks = jnp.concatenate([k_ref[0, pl.ds(k0, tk), pp * LANES:(pp + 1) * LANES], kx], axis=1)
            for hh in range(2):
                s_ref[2 * pp + hh] = _dot_nt(ks, qh_ref[2 * pp + hh])

    def softmax_pv(kb, s_ref, ms, masked):
        k0 = pl.multiple_of(kb * tk, tk)
        vt = vt_ref[0, :, pl.ds(k0, tk)]
        if masked:
            keep = (lax.broadcasted_iota(jnp.int32, (tk, tq), 0)
                    <= lax.broadcasted_iota(jnp.int32, (tk, tq), 1))
        out = []
        for h in range(nh):
            st = s_ref[h]
            if masked:
                st = jnp.where(keep, st, NEG_INF)
            m_new = jnp.maximum(ms[h], jnp.max(st, axis=0, keepdims=True))
            p = jnp.exp2(st - m_new).astype(BF16)
            alpha = jnp.exp2(ms[h] - m_new)
            vth = jnp.concatenate([vt[h * HEAD_DIM:(h + 1) * HEAD_DIM, :], ones], axis=0)
            acc_ref[h] = alpha * acc_ref[h] + _dot(vth, p)
            out.append(m_new)
        return tuple(out)

    def two_blocks(j, ms):
        scores_into(2 * j + 1, s1_ref)
        ms = softmax_pv(2 * j, s0_ref, ms, False)
        scores_into(2 * j + 2, s0_ref)
        return softmax_pv(2 * j + 1, s1_ref, ms, False)

    scores_into(0, s0_ref)
    ms = lax.fori_loop(0, qi // 2, two_blocks,
                       tuple(jnp.full((1, tq), NEG_INF, F32) for _ in range(nh)))

    @pl.when(qi % 2 == 1)
    def _():
        scores_into(qi, s1_ref)
        softmax_pv(qi, s1_ref, softmax_pv(qi - 1, s0_ref, ms, False), True)

    @pl.when(qi % 2 == 0)
    def _():
        softmax_pv(qi, s0_ref, ms, True)

    ot = jnp.concatenate(
        [acc_ref[h, :HEAD_DIM, :] / acc_ref[h, HEAD_DIM:HEAD_DIM + 1, :] for h in range(nh)],
        axis=0)
    o_ref[0] = ot.T.astype(BF16)


def _attention(q, qx, k, kx, vt, tq=512, npp=4):
    bsz, s, d = q.shape
    n_pairs = d // LANES
    npp = min(npp, n_pairs)
    tq = min(tq, s)
    tk = tq
    w = npp * LANES
    return pl.pallas_call(
        functools.partial(_attn_kernel, tq=tq, tk=tk, npp=npp),
        out_shape=jax.ShapeDtypeStruct((bsz, s, d), BF16),
        grid=(bsz, n_pairs // npp, s // tq),
        in_specs=[
            pl.BlockSpec((1, tq, w), lambda b, j, i: (b, i, j)),
            pl.BlockSpec((1, tq, LANES), lambda b, j, i: (b, i, 0)),
            pl.BlockSpec((1, s, w), lambda b, j, i: (b, 0, j)),
            pl.BlockSpec((1, s, LANES), lambda b, j, i: (b, 0, 0)),
            pl.BlockSpec((1, w, s), lambda b, j, i: (b, j, 0)),
        ],
        out_specs=pl.BlockSpec((1, tq, w), lambda b, j, i: (b, i, j)),
        scratch_shapes=[
            pltpu.VMEM((2 * npp, tq, 2 * LANES), BF16),
            pltpu.VMEM((2 * npp, tk, tq), F32),
            pltpu.VMEM((2 * npp, tk, tq), F32),
            pltpu.VMEM((2 * npp, HEAD_DIM + DENOM_ROWS, tq), F32),
        ],
        compiler_params=_cparams(("parallel", "parallel", "arbitrary")),
        name="attn",
    )(q, qx, k, kx, vt)


def _chunks(mod, n):
    return [m[:, None, :] for m in jnp.split(mod, n, axis=-1)]


def kernel(x, c, ada_w, ada_b, norm_mix_g, norm_ffn_g, gmlp_w_in, gmlp_v_g, gmlp_v_b, gmlp_w_s,
           gmlp_b_s, gmlp_w_out, kv_ada_w, kv_ada_b, kv_norm_g, kv_w, k_norm_g, fgate_b, fox_w_qg,
           q_norm_g, fox_w_o, router_w, router_b, exp_w1, exp_b1, exp_w2, exp_b2, final_g):
    mods = _ada(c, ada_w, ada_b)
    kv_mod = _ada(c, kv_ada_w[None], kv_ada_b[None])[0]

    sh_m, sc_m, gt_m, sh_f, sc_f, gt_f = _chunks(mods[0], 6)
    x = _gmlp(x, sh_m, sc_m, gt_m, norm_mix_g[0], gmlp_w_in[0], gmlp_v_g[0], gmlp_v_b[0],
              gmlp_w_s[0], gmlp_b_s[0], gmlp_w_out[0])
    x = _moe_layer(x, sh_f, sc_f, gt_f, norm_ffn_g[0], router_w[0], router_b[0],
                   exp_w1, exp_b1, exp_w2, exp_b2, 0, final_g, False)

    sh_m, sc_m, gt_m, sh_f, sc_f, gt_f = _chunks(mods[1], 6)
    ksh, ksc = _chunks(kv_mod, 2)
    k, kx, vt, q, qx, sg = _kvq(x, ksh, ksc, sh_m, sc_m, kv_norm_g, norm_mix_g[1], kv_w,
                                k_norm_g, fgate_b, fox_w_qg[0], q_norm_g[0])
    o = _attention(q, qx, k, kx, vt)
    return _moe_layer(x, sh_f, sc_f, gt_f, norm_ffn_g[1], router_w[1], router_b[1],
                      exp_w1, exp_b1, exp_w2, exp_b2, 1, final_g, True,
                      mix=(o, sg, gt_m, fox_w_o[0]))
```

```python
import functools

import numpy as np
import jax
import jax.numpy as jnp
from jax import lax
from jax.experimental import pallas as pl
from jax.experimental.pallas import tpu as pltpu

F32 = jnp.float32
BF16 = jnp.bfloat16

CHUNK = 128
GMLP_GROUPS = 8
HEAD_DIM = 64
TOP_K = 4
SWIGLU_LIMIT = 7.0
SWIGLU_ALPHA = 1.702
EPS = 1e-6
NEG_INF = -1e30
LOG2E = 1.4426950408889634
N_PARTS = 3
DENOM_ROWS = 16

LANES = 128
MXU_ROWS = 256
EXPERT_ROWS = 2 * MXU_ROWS
SEG_ALIGN = 8
MOE_TILE = 512
COMBINE_CHUNK = 128
VMEM_LIMIT = 56 * 1024 * 1024


def _cparams(sem, vmem=VMEM_LIMIT):
    return pltpu.CompilerParams(dimension_semantics=sem, vmem_limit_bytes=vmem)


def _divisor_tile(n, pref):
    if n <= pref:
        return n
    return max(t for t in range(LANES, pref + 1, LANES) if n % t == 0)


def _dot(a, b):
    return jnp.dot(a, b, preferred_element_type=F32)


def _dot_nt(a, b):
    return lax.dot_general(a, b, (((1,), (1,)), ((), ())), preferred_element_type=F32)


def _split3(x):
    hi = x.astype(BF16)
    r = x - hi.astype(F32)
    mid = r.astype(BF16)
    lo = (r - mid.astype(F32)).astype(BF16)
    return hi, mid, lo


def _split2(x):
    hi = x.astype(BF16)
    lo = (x - hi.astype(F32)).astype(BF16)
    return hi, lo


def _gelu_exact(x):
    return 0.5 * x * (1.0 + lax.erf(x * (0.5 ** 0.5)))


def _ada_norm(x, g, shift, scale):
    ms = jnp.mean(x * x, axis=-1, keepdims=True)
    y = x * lax.rsqrt(ms + EPS) * g
    return y * (1.0 + scale) + shift


def _ada_kernel(c_ref, w_ref, b_ref, o_ref):
    c = c_ref[...]
    sc = c * jax.nn.sigmoid(c)
    o_ref[0] = _dot(sc.astype(BF16), w_ref[0].astype(BF16)) + b_ref[0]


def _ada(c, w, b):
    nl, d, n = w.shape
    bsz = c.shape[0]
    tn = _divisor_tile(n, 2048)
    return pl.pallas_call(
        _ada_kernel,
        out_shape=jax.ShapeDtypeStruct((nl, bsz, n), F32),
        grid=(nl, n // tn),
        in_specs=[
            pl.BlockSpec((bsz, d), lambda l, j: (0, 0)),
            pl.BlockSpec((1, d, tn), lambda l, j: (l, 0, j)),
            pl.BlockSpec((1, 1, tn), lambda l, j: (l, 0, j)),
        ],
        out_specs=pl.BlockSpec((1, bsz, tn), lambda l, j: (l, 0, j)),
        compiler_params=_cparams(("arbitrary", "arbitrary")),
        name="ada",
    )(c, w, b.reshape(nl, 1, n))


def _gmlproute_kernel(x_ref, sh_ref, sc_ref, gt_ref, g_ref, win_ref, vg_ref, vb_ref,
                      ws_ref, bst_ref, wout_ref, fsh_ref, fsc_ref, fg_ref, rwt_ref, rb_ref,
                      xo_ref, h_ref, lpos_ref, gate_ref, tab_ref, tot_ref, gated_ref, carry_ref,
                      *, ts, width, n_exp):
    x = _gmlp_tile(x_ref[0], sh_ref, sc_ref, gt_ref, g_ref, win_ref, vg_ref, vb_ref,
                   ws_ref, bst_ref, wout_ref, gated_ref, ts, width)
    xo_ref[0] = x
    _route_tile(x, fsh_ref, fsc_ref, fg_ref, rwt_ref, rb_ref,
                h_ref, lpos_ref, gate_ref, tab_ref, tot_ref, carry_ref, ts, n_exp)


def _gmlp_tile(x, sh_ref, sc_ref, gt_ref, g_ref, win_ref, vg_ref, vb_ref,
               ws_ref, bst_ref, wout_ref, gated_ref, ts, width):
    h = _ada_norm(x, g_ref[...], sh_ref[0], sc_ref[0]).astype(BF16)
    z = _gelu_exact(_dot(h, win_ref[...]))
    u = z[:, :width]
    v = z[:, width:]
    mu = jnp.mean(v, axis=-1, keepdims=True)
    vc = v - mu
    var = jnp.mean(vc * vc, axis=-1, keepdims=True)
    vn = (vc * lax.rsqrt(var + EPS) * vg_ref[...] + vb_ref[...]).astype(BF16)

    gd = width // GMLP_GROUPS
    row = lax.broadcasted_iota(jnp.int32, (CHUNK, CHUNK), 0)
    col = lax.broadcasted_iota(jnp.int32, (CHUNK, CHUNK), 1)
    causal = col <= row
    bst = bst_ref[...]
    for g in range(GMLP_GROUPS):
        wc = jnp.where(causal, ws_ref[g], 0.0).astype(BF16)
        bias = bst[:, g:g + 1]
        for ci in range(ts // CHUNK):
            rs = slice(ci * CHUNK, (ci + 1) * CHUNK)
            cs = slice(g * gd, (g + 1) * gd)
            mixed = _dot(wc, vn[rs, cs]) + bias
            gated_ref[rs, cs] = (u[rs, cs] * mixed).astype(BF16)
    y = _dot(gated_ref[...], wout_ref[...])
    return x + gt_ref[0] * y


def _route_specs(bsz, s, d, ts, n_exp):
    nt = s // ts
    t = bsz * s
    vec = pl.BlockSpec((1, 1, d), lambda b, i: (b, 0, 0))
    tile = pl.BlockSpec((1, ts, d), lambda b, i: (b, i, 0))
    slot = pl.BlockSpec((TOP_K, ts), lambda b, i: (0, b * nt + i))
    out_shape = (
        jax.ShapeDtypeStruct((bsz, s, d), F32),
        jax.ShapeDtypeStruct((bsz, s, d), BF16),
        jax.ShapeDtypeStruct((TOP_K, t), jnp.int32),
        jax.ShapeDtypeStruct((TOP_K, t), F32),
        jax.ShapeDtypeStruct((bsz * nt, n_exp, LANES), F32),
        jax.ShapeDtypeStruct((n_exp, LANES), F32),
    )
    out_specs = (
        tile, tile, slot, slot,
        pl.BlockSpec((1, n_exp, LANES), lambda b, i: (b * nt + i, 0, 0)),
        pl.BlockSpec((n_exp, LANES), lambda b, i: (0, 0)),
    )
    in_specs = [
        vec, vec,
        pl.BlockSpec((1, d), lambda b, i: (0, 0)),
        pl.BlockSpec((n_exp, d), lambda b, i: (0, 0)),
        pl.BlockSpec((n_exp, 1), lambda b, i: (0, 0)),
    ]
    return out_shape, out_specs, in_specs


def _gmlp_route(x, sh, sc, gt, g, w_in, v_g, v_b, w_s, b_s, w_out,
                fsh, fsc, fg, router_w, router_b, ts):
    bsz, s, d = x.shape
    width = w_out.shape[0]
    n_exp = router_w.shape[1]
    vec = pl.BlockSpec((1, 1, d), lambda b, i: (b, 0, 0))
    const2 = lambda shape: pl.BlockSpec(shape, lambda b, i: (0, 0))
    out_shape, out_specs, route_in = _route_specs(bsz, s, d, ts, n_exp)
    return pl.pallas_call(
        functools.partial(_gmlproute_kernel, ts=ts, width=width, n_exp=n_exp),
        out_shape=out_shape,
        grid=(bsz, s // ts),
        in_specs=[
            pl.BlockSpec((1, ts, d), lambda b, i: (b, i, 0)),
            vec, vec, vec,
            const2((1, d)),
            const2((d, 2 * width)),
            const2((1, width)),
            const2((1, width)),
            pl.BlockSpec((GMLP_GROUPS, CHUNK, CHUNK), lambda b, i: (0, 0, 0)),
            const2((CHUNK, GMLP_GROUPS)),
            const2((width, d)),
        ] + route_in,
        out_specs=out_specs,
        scratch_shapes=[pltpu.VMEM((ts, width), BF16), pltpu.VMEM((n_exp, 1), F32)],
        compiler_params=_cparams(("arbitrary", "arbitrary")),
        name="gmlproute",
    )(x, sh, sc, gt, g.reshape(1, d), w_in.astype(BF16), v_g.reshape(1, width),
      v_b.reshape(1, width), w_s, b_s.T, w_out.astype(BF16),
      fsh, fsc, fg.reshape(1, d), router_w.T.astype(BF16), router_b.reshape(n_exp, 1))


def _mixroute_kernel(x_ref, o_ref, sg_ref, gt_ref, wo_ref, sh_ref, sc_ref, g_ref, rwt_ref, rb_ref,
                     xo_ref, h_ref, lpos_ref, gate_ref, tab_ref, tot_ref, carry_ref, *, ts, n_exp):
    gated = o_ref[0].astype(F32) * sg_ref[0].astype(F32)
    x = x_ref[0] + gt_ref[0] * _dot(gated.astype(BF16), wo_ref[...])
    xo_ref[0] = x
    _route_tile(x, sh_ref, sc_ref, g_ref, rwt_ref, rb_ref,
                h_ref, lpos_ref, gate_ref, tab_ref, tot_ref, carry_ref, ts, n_exp)


def _route_tile(x, sh_ref, sc_ref, g_ref, rwt_ref, rb_ref,
                h_ref, lpos_ref, gate_ref, tab_ref, tot_ref, carry_ref, ts, n_exp):
    first = jnp.logical_and(pl.program_id(0) == 0, pl.program_id(1) == 0)

    @pl.when(first)
    def _():
        carry_ref[...] = jnp.zeros_like(carry_ref)

    h = _ada_norm(x, g_ref[...], sh_ref[0], sc_ref[0]).astype(BF16)
    h_ref[0] = h
    logits = _dot_nt(rwt_ref[...], h) + rb_ref[...]

    eio = lax.broadcasted_iota(jnp.int32, (n_exp, ts), 0)
    work = logits
    vals, hots = [], []
    for _ in range(TOP_K):
        m = jnp.max(work, axis=0, keepdims=True)
        sel = jnp.min(jnp.where(work == m, eio, n_exp), axis=0, keepdims=True)
        hot = eio == sel
        vals.append(m)
        hots.append(hot)
        work = jnp.where(hot, -jnp.inf, work)

    exps = [jnp.exp(v - vals[0]) for v in vals]
    denom = exps[0]
    for e in exps[1:]:
        denom = denom + e
    gates = [e / denom for e in exps]

    chosen = hots[0]
    for hot in hots[1:]:
        chosen = jnp.logical_or(chosen, hot)
    chosen_f = jnp.where(chosen, 1.0, 0.0)
    r = lax.broadcasted_iota(jnp.int32, (ts, ts), 0)
    c = lax.broadcasted_iota(jnp.int32, (ts, ts), 1)
    before = jnp.where(r < c, 1.0, 0.0).astype(BF16)
    prefix = _dot(chosen_f.astype(BF16), before)
    count = jnp.sum(chosen_f, axis=1, keepdims=True)
    seg_len = jnp.floor((count + (SEG_ALIGN - 1)) * (1.0 / SEG_ALIGN)) * SEG_ALIGN
    er = lax.broadcasted_iota(jnp.int32, (n_exp, n_exp), 0)
    ec = lax.broadcasted_iota(jnp.int32, (n_exp, n_exp), 1)
    lower = jnp.where(ec < er, 1.0, 0.0).astype(BF16)
    seg_start = _dot(lower, jnp.broadcast_to(seg_len, (n_exp, LANES)).astype(BF16))[:, 0:1]
    base = seg_start + prefix
    lpos = [jnp.sum(jnp.where(hot, base, 0.0), axis=0, keepdims=True) for hot in hots]

    lpos_ref[...] = jnp.concatenate(lpos, axis=0).astype(jnp.int32)
    gate_ref[...] = jnp.concatenate(gates, axis=0)
    used = carry_ref[...]
    lane = lax.broadcasted_iota(jnp.int32, (n_exp, LANES), 1)
    tab_ref[0] = jnp.where(lane == 0, seg_start,
                           jnp.where(lane == 1, used, jnp.where(lane == 2, seg_len, 0.0)))
    carry_ref[...] = used + seg_len
    tot_ref[...] = jnp.broadcast_to(used + seg_len, tot_ref.shape)


def _mix_route(x, o, sg, gt, w_o, sh, sc, g, router_w, router_b, ts):
    bsz, s, d = x.shape
    n_exp = router_w.shape[1]
    vec = pl.BlockSpec((1, 1, d), lambda b, i: (b, 0, 0))
    tile = pl.BlockSpec((1, ts, d), lambda b, i: (b, i, 0))
    out_shape, out_specs, route_in = _route_specs(bsz, s, d, ts, n_exp)
    return pl.pallas_call(
        functools.partial(_mixroute_kernel, ts=ts, n_exp=n_exp),
        out_shape=out_shape,
        grid=(bsz, s // ts),
        in_specs=[tile, tile, tile, vec, pl.BlockSpec((d, d), lambda b, i: (0, 0))] + route_in,
        out_specs=out_specs,
        scratch_shapes=[pltpu.VMEM((n_exp, 1), F32)],
        compiler_params=_cparams(("arbitrary", "arbitrary")),
        name="mixroute",
    )(x, o, sg, gt, w_o.astype(BF16),
      sh, sc, g.reshape(1, d), router_w.T.astype(BF16), router_b.reshape(n_exp, 1))


def _wait_rows(src, dst, sem):
    pltpu.make_async_copy(src, dst, sem).wait()


def _chunk_sizes(max_rows):
    sizes = []
    size = SEG_ALIGN
    while size <= max_rows:
        sizes.append(size)
        size *= 2
    return sizes


def _for_each_chunk(length, max_rows, fn):
    for size in reversed(_chunk_sizes(max_rows)):
        shift = size.bit_length() - 1
        @pl.when(((length >> shift) & 1) == 1)
        def _():
            fn(pl.multiple_of((length >> (shift + 1)) << (shift + 1), SEG_ALIGN), size)


def _segment_copies(tile, n_exp, max_rows, src_ref, dst_ref, len_ref, make_copy):
    for e in range(n_exp):
        s = tile * n_exp + e
        src, dst = src_ref[s], dst_ref[s]
        _for_each_chunk(len_ref[s], max_rows,
                        lambda off, size: make_copy(pl.multiple_of(src + off, SEG_ALIGN),
                                                    pl.multiple_of(dst + off, SEG_ALIGN),
                                                    size).start())


def _segment_wait(tile, n_exp, lrows, src_ref, len_ref, make_copy):
    last = tile * n_exp + n_exp - 1
    total = src_ref[last] + len_ref[last]
    _for_each_chunk(total, lrows, lambda off, size: make_copy(0, 0, size).wait())


def _dispatch_kernel(pstart_ref, pend_ref, src_ref, dst_ref, len_ref, lpos_ref, h_ref, buf_hbm,
                     comp_ref, zero_ref, sem_seg, sem_zero,
                     *, ts, lrows, tm, n_exp, n_blocks, n_tiles):
    i = pl.program_id(0)
    slot = i % 2

    def seg_copy_from(buf_slot):
        def seg_copy(local_row, global_row, size):
            return pltpu.make_async_copy(comp_ref.at[buf_slot, pl.ds(local_row, size)],
                                         buf_hbm.at[pl.ds(global_row, size)], sem_seg.at[buf_slot])
        return seg_copy

    @pl.when(i == 0)
    def _():
        zero_ref[...] = jnp.zeros_like(zero_ref)
        for e in range(n_exp):
            @pl.when(pend_ref[e] > pstart_ref[e])
            def _():
                last = pl.multiple_of(pend_ref[e] - tm, tm)
                pltpu.make_async_copy(zero_ref, buf_hbm.at[pl.ds(last, tm)], sem_zero).start()
        n_used = pend_ref[n_exp - 1] // tm

        def zero_tail(b, carry):
            off = pl.multiple_of(b * tm, tm)
            pltpu.make_async_copy(zero_ref, buf_hbm.at[pl.ds(off, tm)], sem_zero).start()
            return carry

        def wait_tail(b, carry):
            _wait_rows(zero_ref, buf_hbm.at[pl.ds(0, tm)], sem_zero)
            return carry

        lax.fori_loop(n_used, n_blocks, zero_tail, 0)
        for e in range(n_exp):
            @pl.when(pend_ref[e] > pstart_ref[e])
            def _():
                _wait_rows(zero_ref, buf_hbm.at[pl.ds(0, tm)], sem_zero)
        lax.fori_loop(n_used, n_blocks, wait_tail, 0)

    @pl.when(i > 1)
    def _():
        _segment_wait(i - 2, n_exp, lrows, src_ref, len_ref, seg_copy_from(slot))

    lpos = lpos_ref[...]
    rows = lax.broadcasted_iota(jnp.int32, (MXU_ROWS, ts), 0)
    h = h_ref[...]
    for c in range(lrows // MXU_ROWS):
        rel = lpos - c * MXU_ROWS
        hit = rows == rel[0:1, :]
        for k in range(1, TOP_K):
            hit = jnp.logical_or(hit, rows == rel[k:k + 1, :])
        onehot = jnp.where(hit, 1.0, 0.0).astype(BF16)
        comp_ref[slot, c * MXU_ROWS:(c + 1) * MXU_ROWS, :] = _dot(onehot, h)
    _segment_copies(i, n_exp, ts, src_ref, dst_ref, len_ref, seg_copy_from(slot))

    @pl.when(i == n_tiles - 1)
    def _():
        @pl.when(i > 0)
        def _():
            _segment_wait(i - 1, n_exp, lrows, src_ref, len_ref, seg_copy_from(1 - slot))
        _segment_wait(i, n_exp, lrows, src_ref, len_ref, seg_copy_from(slot))


def _dispatch(pstart, pends, seg_src, seg_dst, seg_len, lpos, h, n_rows, ts, lrows):
    t, d = h.shape
    tm = EXPERT_ROWS
    n_tiles = t // ts
    tile_map = lambda i, *_: (i, 0)
    return pl.pallas_call(
        functools.partial(_dispatch_kernel, ts=ts, lrows=lrows, tm=tm, n_exp=pstart.shape[0],
                          n_blocks=n_rows // tm, n_tiles=n_tiles),
        out_shape=jax.ShapeDtypeStruct((n_rows, d), F32),
        grid_spec=pltpu.PrefetchScalarGridSpec(
            num_scalar_prefetch=5,
            grid=(n_tiles,),
            in_specs=[
                pl.BlockSpec((TOP_K, ts), lambda i, *_: (0, i)),
                pl.BlockSpec((ts, d), tile_map),
            ],
            out_specs=pl.BlockSpec(memory_space=pl.ANY),
            scratch_shapes=[
                pltpu.VMEM((2, lrows, d), F32),
                pltpu.VMEM((tm, d), F32),
                pltpu.SemaphoreType.DMA((2,)),
                pltpu.SemaphoreType.DMA,
            ],
        ),
        compiler_params=_cparams(("arbitrary",)),
        name="dispatch",
    )(pstart, pends, seg_src, seg_dst, seg_len, lpos, h)


def _expert_kernel(be_ref, nu_ref, x_ref, w1_ref, b1_ref, w2_ref, b2_ref, y_ref,
                   w1b_ref, w2b_ref, *, d_exp):
    i = pl.program_id(0)
    changed = jnp.logical_or(i == 0, be_ref[i] != be_ref[jnp.maximum(i - 1, 0)])

    @pl.when(changed)
    def _():
        w1b_ref[...] = w1_ref[0, 0].astype(BF16)
        w2b_ref[...] = w2_ref[0, 0].astype(BF16)

    @pl.when(i < nu_ref[0])
    def _():
        gu = _dot(x_ref[...].astype(BF16), w1b_ref[...]) + b1_ref[0, 0]
        glu = jnp.minimum(gu[:, :d_exp], SWIGLU_LIMIT)
        lin = jnp.clip(gu[:, d_exp:], -SWIGLU_LIMIT, SWIGLU_LIMIT)
        act = (lin + 1.0) * (glu * jax.nn.sigmoid(SWIGLU_ALPHA * glu))
        y_ref[...] = _dot(act.astype(BF16), w2b_ref[...]) + b2_ref[0, 0]

    @pl.when(i >= nu_ref[0])
    def _():
        y_ref[...] = jnp.zeros_like(y_ref)


def _experts(block_expert, n_used, buf, w1, b1, w2, b2, layer):
    n_rows, d = buf.shape
    _, n_exp, _, d_exp2 = w1.shape
    d_exp = d_exp2 // 2
    tm = EXPERT_ROWS
    n_blocks = n_rows // tm

    def in_map(i, be, nu):
        return (jnp.minimum(i, nu[0] - 1), 0)

    def out_map(i, be, nu):
        return (i, 0)

    def w_map(i, be, nu):
        return (layer, be[i], 0, 0)

    return pl.pallas_call(
        functools.partial(_expert_kernel, d_exp=d_exp),
        out_shape=jax.ShapeDtypeStruct((n_rows, d), F32),
        grid_spec=pltpu.PrefetchScalarGridSpec(
            num_scalar_prefetch=2,
            grid=(n_blocks,),
            in_specs=[
                pl.BlockSpec((tm, d), in_map),
                pl.BlockSpec((1, 1, d, d_exp2), w_map),
                pl.BlockSpec((1, 1, 1, d_exp2), w_map),
                pl.BlockSpec((1, 1, d_exp, d), w_map),
                pl.BlockSpec((1, 1, 1, d), w_map),
            ],
            out_specs=pl.BlockSpec((tm, d), out_map),
            scratch_shapes=[pltpu.VMEM((d, d_exp2), BF16), pltpu.VMEM((d_exp, d), BF16)],
        ),
        compiler_params=_cparams(("arbitrary",)),
        name="experts",
    )(block_expert, n_used, buf, w1, b1.reshape(b1.shape[0], n_exp, 1, d_exp2),
      w2, b2.reshape(b2.shape[0], n_exp, 1, d))


def _combine_kernel(src_ref, dst_ref, len_ref, y_hbm, lpos_ref, gate_ref, x_ref, gt_ref, fg_ref,
                    o_ref, ycomp_ref, sem_seg, *, ts, lrows, n_exp, n_tiles, final_norm):
    i = pl.program_id(0)
    slot = i % 2

    def seg_copy_into(buf_slot):
        def seg_copy(local_row, global_row, size):
            return pltpu.make_async_copy(y_hbm.at[pl.ds(global_row, size)],
                                         ycomp_ref.at[buf_slot, pl.ds(local_row, size)],
                                         sem_seg.at[buf_slot])
        return seg_copy

    @pl.when(i == 0)
    def _():
        ycomp_ref[...] = jnp.zeros_like(ycomp_ref)
        _segment_copies(0, n_exp, ts, src_ref, dst_ref, len_ref, seg_copy_into(0))

    @pl.when(i + 1 < n_tiles)
    def _():
        _segment_copies(i + 1, n_exp, ts, src_ref, dst_ref, len_ref, seg_copy_into(1 - slot))

    _segment_wait(i, n_exp, lrows, src_ref, len_ref, seg_copy_into(slot))
    y = ycomp_ref[slot].astype(BF16)

    cols = lax.broadcasted_iota(jnp.int32, (COMBINE_CHUNK, lrows), 1)
    for c in range(ts // COMBINE_CHUNK):
        rs = slice(c * COMBINE_CHUNK, (c + 1) * COMBINE_CHUNK)
        lpos = lpos_ref[rs, :]
        gates = gate_ref[rs, :]
        sel = jnp.zeros((COMBINE_CHUNK, lrows), F32)
        for k in range(TOP_K):
            sel = jnp.where(cols == lpos[:, k:k + 1], gates[:, k:k + 1], sel)
        out = x_ref[rs, :] + gt_ref[0] * _dot(sel.astype(BF16), y)
        if final_norm:
            ms = jnp.mean(out * out, axis=-1, keepdims=True)
            out = out * lax.rsqrt(ms + EPS) * fg_ref[...]
        o_ref[rs, :] = out


def _combine(seg_src, seg_dst, seg_len, y_buf, lpos_t, gates_t, x2d, gt, final_g, s, final_norm,
             ts, lrows, n_exp):
    t, d = x2d.shape
    per_b = s // ts
    tile_map = lambda i, *_: (i, 0)
    return pl.pallas_call(
        functools.partial(_combine_kernel, ts=ts, lrows=lrows, n_exp=n_exp, n_tiles=t // ts,
                          final_norm=final_norm),
        out_shape=jax.ShapeDtypeStruct((t, d), F32),
        grid_spec=pltpu.PrefetchScalarGridSpec(
            num_scalar_prefetch=3,
            grid=(t // ts,),
            in_specs=[
                pl.BlockSpec(memory_space=pl.ANY),
                pl.BlockSpec((ts, TOP_K), tile_map),
                pl.BlockSpec((ts, TOP_K), tile_map),
                pl.BlockSpec((ts, d), tile_map),
                pl.BlockSpec((1, 1, d), lambda i, *_: (i // per_b, 0, 0)),
                pl.BlockSpec((1, d), lambda i, *_: (0, 0)),
            ],
            out_specs=pl.BlockSpec((ts, d), tile_map),
            scratch_shapes=[
                pltpu.VMEM((2, lrows, d), F32),
                pltpu.SemaphoreType.DMA((2,)),
            ],
        ),
        compiler_params=_cparams(("arbitrary",)),
        name="combine",
    )(seg_src, seg_dst, seg_len, y_buf, lpos_t, gates_t, x2d, gt, final_g.reshape(1, d))


def _moe_tile(s):
    return min(MOE_TILE, s)


def _moe_layer(routed, gt, w1, b1, w2, b2, layer, final_g, final_norm):
    x, h, lpos, gates, tab, tot = routed
    bsz, s, d = x.shape
    t = bsz * s
    n_exp = tot.shape[0]
    tm = EXPERT_ROWS
    ts = _moe_tile(s)
    n_tiles = t // ts
    lrows = -(-(ts * TOP_K + n_exp * (SEG_ALIGN - 1)) // MXU_ROWS) * MXU_ROWS

    rows = tot[:, 0].astype(jnp.int32)
    padded = (rows + tm - 1) // tm * tm
    pends = jnp.cumsum(padded).astype(jnp.int32)
    pstart = pends - padded
    n_blocks = -(-(t * TOP_K + n_tiles * n_exp * (SEG_ALIGN - 1) + n_exp * (tm - 1)) // tm)
    n_used = pends[-1] // tm
    blk = jnp.minimum(jnp.arange(n_blocks, dtype=jnp.int32), n_used - 1)
    block_expert = jnp.minimum(
        jnp.sum((pends[None, :] <= (blk * tm)[:, None]).astype(jnp.int32), axis=1), n_exp - 1)
    tab = tab[:, :, :3].astype(jnp.int32)
    seg_src = tab[:, :, 0].reshape(-1)
    seg_dst = (pstart[None, :] + tab[:, :, 1]).reshape(-1)
    seg_len = tab[:, :, 2].reshape(-1)

    buf = _dispatch(pstart, pends, seg_src, seg_dst, seg_len, lpos, h.reshape(t, d),
                    n_blocks * tm, ts, lrows)
    y_buf = _experts(block_expert, n_used.reshape(1), buf, w1, b1, w2, b2, layer)
    out = _combine(seg_src, seg_dst, seg_len, y_buf, lpos.T, gates.T, x.reshape(t, d), gt, final_g,
                   s, final_norm, ts, lrows, n_exp)
    return out.reshape(bsz, s, d)


def _head_rms(x, seg, seg_t, inv_dh):
    hi, lo = _split2(x * x)
    ssum = _dot(hi, seg) + _dot(lo, seg)
    inv = lax.rsqrt(ssum * inv_dh + EPS)
    ihi, ilo = _split2(inv)
    return _dot(ihi, seg_t) + _dot(ilo, seg_t)


def _kvq_kernel(x_ref, ksh_ref, ksc_ref, msh_ref, msc_ref, kg_ref, mg_ref,
                wk_ref, wvt_ref, wf_ref, fb_ref, kng_ref, qng_ref, wq_ref, wg_ref,
                place_ref, cst_ref,
                k_ref, kx_ref, vt_ref, q_ref, qx_ref, sg_ref, ccol_ref, *, ts, d):
    @pl.when(pl.program_id(1) == 0)
    def _():
        ccol_ref[...] = jnp.zeros_like(ccol_ref)

    x = x_ref[0]
    ms = jnp.mean(x * x, axis=-1, keepdims=True)
    xn = x * lax.rsqrt(ms + EPS)
    hk = ((xn * kg_ref[...]) * (1.0 + ksc_ref[0]) + ksh_ref[0]).astype(BF16)
    hq = ((xn * mg_ref[...]) * (1.0 + msc_ref[0]) + msh_ref[0]).astype(BF16)

    di = lax.broadcasted_iota(jnp.int32, (d, LANES), 0) // HEAD_DIM
    hi_ = lax.broadcasted_iota(jnp.int32, (d, LANES), 1)
    seg = jnp.where(di == hi_, 1.0, 0.0).astype(BF16)
    dt = lax.broadcasted_iota(jnp.int32, (LANES, d), 1) // HEAD_DIM
    ht = lax.broadcasted_iota(jnp.int32, (LANES, d), 0)
    seg_t = jnp.where(dt == ht, 1.0, 0.0).astype(BF16)
    inv_dh = 1.0 / HEAD_DIM

    k = _dot(hk, wk_ref[...])
    k_ref[0] = (k * _head_rms(k, seg, seg_t, inv_dh) * kng_ref[...]).astype(BF16)
    vt_ref[0] = _dot_nt(wvt_ref[...], hk).astype(BF16)

    q = _dot(hq, wq_ref[...])
    qn = q * _head_rms(q, seg, seg_t, inv_dh) * qng_ref[...]
    q_ref[0] = (qn * (HEAD_DIM ** -0.5 * LOG2E)).astype(BF16)
    sg_ref[0] = jax.nn.sigmoid(_dot(hq, wg_ref[...])).astype(BF16)

    lf = jax.nn.log_sigmoid(_dot(hk, wf_ref[...]) + fb_ref[...])
    r = lax.broadcasted_iota(jnp.int32, (ts, ts), 0)
    c = lax.broadcasted_iota(jnp.int32, (ts, ts), 1)
    lower = jnp.where(c <= r, 1.0, 0.0).astype(BF16)
    a, b, e = _split3(lf)
    cum = ccol_ref[...] + (_dot(lower, a) + _dot(lower, b) + _dot(lower, e))
    ccol_ref[...] = cum[ts - 1:ts, :]
    a, b, e = _split3(cum * LOG2E)
    extra = (_dot(a, place_ref[0]) + _dot(b, place_ref[1]) + _dot(e, place_ref[2])
             + cst_ref[...]).astype(BF16)
    kx_ref[0] = extra[:, :LANES]
    qx_ref[0] = extra[:, LANES:]


def _bias_lane_tables(n_heads):
    assert 2 * N_PARTS * n_heads <= LANES
    place = np.zeros((N_PARTS, LANES, 2 * LANES), np.float32)
    cst = np.zeros((1, 2 * LANES), np.float32)
    for h in range(n_heads):
        base = 2 * N_PARTS * h
        for part in range(N_PARTS):
            place[part, h, base + part] = 1.0
            place[part, h, LANES + base + N_PARTS + part] = 1.0
        cst[0, base + N_PARTS:base + 2 * N_PARTS] = 1.0
        cst[0, LANES + base:LANES + base + N_PARTS] = -1.0
    return jnp.asarray(place, BF16), jnp.asarray(cst, F32)


def _kvq(x, ksh, ksc, msh, msc, kv_norm_g, mix_norm_g, kv_w, k_norm_g, fgate_b,
         w_qg, q_norm_g, ts=256):
    bsz, s, d = x.shape
    n_heads = d // HEAD_DIM
    ts = min(ts, s)
    wk = kv_w[:, :d].astype(BF16)
    wvt = kv_w[:, d:2 * d].T.astype(BF16)
    wf = kv_w[:, 2 * d:]
    wf_pad = jnp.pad(wf, ((0, 0), (0, LANES - n_heads))).astype(BF16)
    fb_pad = jnp.pad(fgate_b, (0, LANES - n_heads)).reshape(1, LANES)
    place, cst = _bias_lane_tables(n_heads)
    vec = pl.BlockSpec((1, 1, d), lambda b, i: (b, 0, 0))
    c2 = lambda shape: pl.BlockSpec(shape, lambda b, i: (0, 0))
    tile = pl.BlockSpec((1, ts, d), lambda b, i: (b, i, 0))
    tile_x = pl.BlockSpec((1, ts, LANES), lambda b, i: (b, i, 0))
    return pl.pallas_call(
        functools.partial(_kvq_kernel, ts=ts, d=d),
        out_shape=(
            jax.ShapeDtypeStruct((bsz, s, d), BF16),
            jax.ShapeDtypeStruct((bsz, s, LANES), BF16),
            jax.ShapeDtypeStruct((bsz, d, s), BF16),
            jax.ShapeDtypeStruct((bsz, s, d), BF16),
            jax.ShapeDtypeStruct((bsz, s, LANES), BF16),
            jax.ShapeDtypeStruct((bsz, s, d), BF16),
        ),
        grid=(bsz, s // ts),
        in_specs=[
            tile, vec, vec, vec, vec,
            c2((1, d)), c2((1, d)),
            c2((d, d)), c2((d, d)), c2((d, LANES)), c2((1, LANES)),
            c2((1, d)), c2((1, d)),
            c2((d, d)), c2((d, d)),
            pl.BlockSpec((N_PARTS, LANES, 2 * LANES), lambda b, i: (0, 0, 0)),
            c2((1, 2 * LANES)),
        ],
        out_specs=(
            tile, tile_x,
            pl.BlockSpec((1, d, ts), lambda b, i: (b, 0, i)),
            tile, tile_x,
            tile,
        ),
        scratch_shapes=[pltpu.VMEM((1, LANES), F32)],
        compiler_params=_cparams(("arbitrary", "arbitrary")),
        name="kvq",
    )(x, ksh, ksc, msh, msc, kv_norm_g.reshape(1, d), mix_norm_g.reshape(1, d),
      wk, wvt, wf_pad, fb_pad,
      jnp.tile(k_norm_g, n_heads).reshape(1, d), jnp.tile(q_norm_g, n_heads).reshape(1, d),
      w_qg[:, :d].astype(BF16), w_qg[:, d:].astype(BF16), place, cst)


def _attn_kernel(q_ref, qx_ref, k_ref, kx_ref, vt_ref, o_ref, qh_ref, s0_ref, s1_ref, acc_ref,
                 *, tq, tk, npp):
    nh = 2 * npp
    first_pair = pl.program_id(1) * npp
    qi = pl.program_id(2)
    lane = lax.broadcasted_iota(jnp.int32, (tq, 2 * LANES), 1)
    for pp in range(npp):
        q = jnp.concatenate([q_ref[0, :, pp * LANES:(pp + 1) * LANES], qx_ref[0]], axis=1)
        zero = jnp.zeros_like(q)
        for hh in range(2):
            feat = jnp.logical_and(lane >= hh * HEAD_DIM, lane < (hh + 1) * HEAD_DIM)
            first_bias = LANES + 2 * N_PARTS * (2 * (first_pair + pp) + hh)
            bias = jnp.logical_and(lane >= first_bias, lane < first_bias + 2 * N_PARTS)
            qh_ref[2 * pp + hh] = jnp.where(feat | bias, q, zero)
    acc_ref[...] = jnp.zeros_like(acc_ref)
    ones = jnp.ones((DENOM_ROWS, tk), BF16)

    def scores_into(kb, s_ref):
        k0 = pl.multiple_of(kb * tk, tk)
        kx = kx_ref[0, pl.ds(k0, tk), :]
        for pp in range(npp):
            ks = jnp.concatenate([k_ref[0, pl.ds(k0, tk), pp * LANES:(pp + 1) * LANES], kx], axis=1)
            for hh in range(2):
                s_ref[2 * pp + hh] = _dot_nt(ks, qh_ref[2 * pp + hh])

    def softmax_pv(kb, s_ref, ms, masked):
        k0 = pl.multiple_of(kb * tk, tk)
        vt = vt_ref[0, :, pl.ds(k0, tk)]
        if masked:
            keep = (lax.broadcasted_iota(jnp.int32, (tk, tq), 0)
                    <= lax.broadcasted_iota(jnp.int32, (tk, tq), 1))
        out = []
        for h in range(nh):
            st = s_ref[h]
            if masked:
                st = jnp.where(keep, st, NEG_INF)
            m_new = jnp.maximum(ms[h], jnp.max(st, axis=0, keepdims=True))
            p = jnp.exp2(st - m_new).astype(BF16)
            alpha = jnp.exp2(ms[h] - m_new)
            vth = jnp.concatenate([vt[h * HEAD_DIM:(h + 1) * HEAD_DIM, :], ones], axis=0)
            acc_ref[h] = alpha * acc_ref[h] + _dot(vth, p)
            out.append(m_new)
        return tuple(out)

    def two_blocks(j, ms):
        scores_into(2 * j + 1, s1_ref)
        ms = softmax_pv(2 * j, s0_ref, ms, False)
        scores_into(2 * j + 2, s0_ref)
        return softmax_pv(2 * j + 1, s1_ref, ms, False)

    scores_into(0, s0_ref)
    ms = lax.fori_loop(0, qi // 2, two_blocks,
                       tuple(jnp.full((1, tq), NEG_INF, F32) for _ in range(nh)))

    @pl.when(qi % 2 == 1)
    def _():
        scores_into(qi, s1_ref)
        softmax_pv(qi, s1_ref, softmax_pv(qi - 1, s0_ref, ms, False), True)

    @pl.when(qi % 2 == 0)
    def _():
        softmax_pv(qi, s0_ref, ms, True)

    ot = jnp.concatenate(
        [acc_ref[h, :HEAD_DIM, :] / acc_ref[h, HEAD_DIM:HEAD_DIM + 1, :] for h in range(nh)],
        axis=0)
    o_ref[0] = ot.T.astype(BF16)


def _attention(q, qx, k, kx, vt, tq=512, npp=4):
    bsz, s, d = q.shape
    n_pairs = d // LANES
    npp = min(npp, n_pairs)
    tq = min(tq, s)
    tk = tq
    w = npp * LANES
    return pl.pallas_call(
        functools.partial(_attn_kernel, tq=tq, tk=tk, npp=npp),
        out_shape=jax.ShapeDtypeStruct((bsz, s, d), BF16),
        grid=(bsz, n_pairs // npp, s // tq),
        in_specs=[
            pl.BlockSpec((1, tq, w), lambda b, j, i: (b, i, j)),
            pl.BlockSpec((1, tq, LANES), lambda b, j, i: (b, i, 0)),
            pl.BlockSpec((1, s, w), lambda b, j, i: (b, 0, j)),
            pl.BlockSpec((1, s, LANES), lambda b, j, i: (b, 0, 0)),
            pl.BlockSpec((1, w, s), lambda b, j, i: (b, j, 0)),
        ],
        out_specs=pl.BlockSpec((1, tq, w), lambda b, j, i: (b, i, j)),
        scratch_shapes=[
            pltpu.VMEM((2 * npp, tq, 2 * LANES), BF16),
            pltpu.VMEM((2 * npp, tk, tq), F32),
            pltpu.VMEM((2 * npp, tk, tq), F32),
            pltpu.VMEM((2 * npp, HEAD_DIM + DENOM_ROWS, tq), F32),
        ],
        compiler_params=_cparams(("parallel", "parallel", "arbitrary")),
        name="attn",
    )(q, qx, k, kx, vt)


def _chunks(mod, n):
    return [m[:, None, :] for m in jnp.split(mod, n, axis=-1)]


def kernel(x, c, ada_w, ada_b, norm_mix_g, norm_ffn_g, gmlp_w_in, gmlp_v_g, gmlp_v_b, gmlp_w_s,
           gmlp_b_s, gmlp_w_out, kv_ada_w, kv_ada_b, kv_norm_g, kv_w, k_norm_g, fgate_b, fox_w_qg,
           q_norm_g, fox_w_o, router_w, router_b, exp_w1, exp_b1, exp_w2, exp_b2, final_g):
    mods = _ada(c, ada_w, ada_b)
    kv_mod = _ada(c, kv_ada_w[None], kv_ada_b[None])[0]

    sh_m, sc_m, gt_m, sh_f, sc_f, gt_f = _chunks(mods[0], 6)
    ts = _moe_tile(x.shape[1])
    routed = _gmlp_route(x, sh_m, sc_m, gt_m, norm_mix_g[0], gmlp_w_in[0], gmlp_v_g[0],
                         gmlp_v_b[0], gmlp_w_s[0], gmlp_b_s[0], gmlp_w_out[0],
                         sh_f, sc_f, norm_ffn_g[0], router_w[0], router_b[0], ts)
    x = _moe_layer(routed, gt_f, exp_w1, exp_b1, exp_w2, exp_b2, 0, final_g, False)

    sh_m, sc_m, gt_m, sh_f, sc_f, gt_f = _chunks(mods[1], 6)
    ksh, ksc = _chunks(kv_mod, 2)
    k, kx, vt, q, qx, sg = _kvq(x, ksh, ksc, sh_m, sc_m, kv_norm_g, norm_mix_g[1], kv_w,
                                k_norm_g, fgate_b, fox_w_qg[0], q_norm_g[0])
    o = _attention(q, qx, k, kx, vt)
    routed = _mix_route(x, o, sg, gt_m, fox_w_o[0], sh_f, sc_f, norm_ffn_g[1],
                        router_w[1], router_b[1], ts)
    return _moe_layer(routed, gt_f, exp_w1, exp_b1, exp_w2, exp_b2, 1, final_g, True)
```

```python
import functools

import numpy as np
import jax
import jax.numpy as jnp
from jax import lax
from jax.experimental import pallas as pl
from jax.experimental.pallas import tpu as pltpu

F32 = jnp.float32
BF16 = jnp.bfloat16

CHUNK = 128
GMLP_GROUPS = 8
HEAD_DIM = 64
TOP_K = 4
SWIGLU_LIMIT = 7.0
SWIGLU_ALPHA = 1.702
EPS = 1e-6
NEG_INF = -1e30
LOG2E = 1.4426950408889634
N_PARTS = 3
DENOM_ROWS = 16

LANES = 128
MXU_ROWS = 256
EXPERT_ROWS = 2 * MXU_ROWS
SEG_ALIGN = 8
MOE_TILE = 512
COMBINE_CHUNK = 128
VMEM_LIMIT = 56 * 1024 * 1024


def _cparams(sem, vmem=VMEM_LIMIT):
    return pltpu.CompilerParams(dimension_semantics=sem, vmem_limit_bytes=vmem)


def _divisor_tile(n, pref):
    if n <= pref:
        return n
    return max(t for t in range(LANES, pref + 1, LANES) if n % t == 0)


def _dot(a, b):
    return jnp.dot(a, b, preferred_element_type=F32)


def _dot_nt(a, b):
    return lax.dot_general(a, b, (((1,), (1,)), ((), ())), preferred_element_type=F32)


def _split3(x):
    hi = x.astype(BF16)
    r = x - hi.astype(F32)
    mid = r.astype(BF16)
    lo = (r - mid.astype(F32)).astype(BF16)
    return hi, mid, lo


def _split2(x):
    hi = x.astype(BF16)
    lo = (x - hi.astype(F32)).astype(BF16)
    return hi, lo


def _gelu_exact(x):
    return 0.5 * x * (1.0 + lax.erf(x * (0.5 ** 0.5)))


def _ada_norm(x, g, shift, scale):
    ms = jnp.mean(x * x, axis=-1, keepdims=True)
    y = x * lax.rsqrt(ms + EPS) * g
    return y * (1.0 + scale) + shift


def _ada_kernel(c_ref, w_ref, b_ref, o_ref):
    c = c_ref[...]
    sc = c * jax.nn.sigmoid(c)
    o_ref[0] = _dot(sc.astype(BF16), w_ref[0].astype(BF16)) + b_ref[0]


def _ada(c, w, b):
    nl, d, n = w.shape
    bsz = c.shape[0]
    tn = _divisor_tile(n, 2048)
    return pl.pallas_call(
        _ada_kernel,
        out_shape=jax.ShapeDtypeStruct((nl, bsz, n), F32),
        grid=(nl, n // tn),
        in_specs=[
            pl.BlockSpec((bsz, d), lambda l, j: (0, 0)),
            pl.BlockSpec((1, d, tn), lambda l, j: (l, 0, j)),
            pl.BlockSpec((1, 1, tn), lambda l, j: (l, 0, j)),
        ],
        out_specs=pl.BlockSpec((1, bsz, tn), lambda l, j: (l, 0, j)),
        compiler_params=_cparams(("arbitrary", "arbitrary")),
        name="ada",
    )(c, w, b.reshape(nl, 1, n))


def _gmlproute_kernel(x_ref, sh_ref, sc_ref, gt_ref, g_ref, win_ref, vg_ref, vb_ref,
                      ws_ref, bst_ref, wout_ref, fsh_ref, fsc_ref, fg_ref, rwt_ref, rb_ref,
                      xo_ref, h_ref, lpos_ref, gate_ref, tab_ref, tot_ref, gated_ref, carry_ref,
                      *, ts, width, n_exp):
    x = _gmlp_tile(x_ref[0], sh_ref, sc_ref, gt_ref, g_ref, win_ref, vg_ref, vb_ref,
                   ws_ref, bst_ref, wout_ref, gated_ref, ts, width)
    xo_ref[0] = x
    _route_tile(x, fsh_ref, fsc_ref, fg_ref, rwt_ref, rb_ref,
                h_ref, lpos_ref, gate_ref, tab_ref, tot_ref, carry_ref, ts, n_exp)


def _gmlp_tile(x, sh_ref, sc_ref, gt_ref, g_ref, win_ref, vg_ref, vb_ref,
               ws_ref, bst_ref, wout_ref, gated_ref, ts, width):
    h = _ada_norm(x, g_ref[...], sh_ref[0], sc_ref[0]).astype(BF16)
    z = _gelu_exact(_dot(h, win_ref[...]))
    u = z[:, :width]
    v = z[:, width:]
    mu = jnp.mean(v, axis=-1, keepdims=True)
    vc = v - mu
    var = jnp.mean(vc * vc, axis=-1, keepdims=True)
    vn = (vc * lax.rsqrt(var + EPS) * vg_ref[...] + vb_ref[...]).astype(BF16)

    gd = width // GMLP_GROUPS
    row = lax.broadcasted_iota(jnp.int32, (CHUNK, CHUNK), 0)
    col = lax.broadcasted_iota(jnp.int32, (CHUNK, CHUNK), 1)
    causal = col <= row
    bst = bst_ref[...]
    for g in range(GMLP_GROUPS):
        wc = jnp.where(causal, ws_ref[g], 0.0).astype(BF16)
        bias = bst[:, g:g + 1]
        for ci in range(ts // CHUNK):
            rs = slice(ci * CHUNK, (ci + 1) * CHUNK)
            cs = slice(g * gd, (g + 1) * gd)
            mixed = _dot(wc, vn[rs, cs]) + bias
            gated_ref[rs, cs] = (u[rs, cs] * mixed).astype(BF16)
    y = _dot(gated_ref[...], wout_ref[...])
    return x + gt_ref[0] * y


def _route_specs(bsz, s, d, ts, n_exp):
    nt = s // ts
    t = bsz * s
    vec = pl.BlockSpec((1, 1, d), lambda b, i: (b, 0, 0))
    tile = pl.BlockSpec((1, ts, d), lambda b, i: (b, i, 0))
    slot = pl.BlockSpec((TOP_K, ts), lambda b, i: (0, b * nt + i))
    out_shape = (
        jax.ShapeDtypeStruct((bsz, s, d), F32),
        jax.ShapeDtypeStruct((bsz, s, d), BF16),
        jax.ShapeDtypeStruct((TOP_K, t), jnp.int32),
        jax.ShapeDtypeStruct((TOP_K, t), F32),
        jax.ShapeDtypeStruct((bsz * nt, n_exp, LANES), F32),
        jax.ShapeDtypeStruct((n_exp, LANES), F32),
    )
    out_specs = (
        tile, tile, slot, slot,
        pl.BlockSpec((1, n_exp, LANES), lambda b, i: (b * nt + i, 0, 0)),
        pl.BlockSpec((n_exp, LANES), lambda b, i: (0, 0)),
    )
    in_specs = [
        vec, vec,
        pl.BlockSpec((1, d), lambda b, i: (0, 0)),
        pl.BlockSpec((n_exp, d), lambda b, i: (0, 0)),
        pl.BlockSpec((n_exp, 1), lambda b, i: (0, 0)),
    ]
    return out_shape, out_specs, in_specs


def _gmlp_route(x, sh, sc, gt, g, w_in, v_g, v_b, w_s, b_s, w_out,
                fsh, fsc, fg, router_w, router_b, ts):
    bsz, s, d = x.shape
    width = w_out.shape[0]
    n_exp = router_w.shape[1]
    vec = pl.BlockSpec((1, 1, d), lambda b, i: (b, 0, 0))
    const2 = lambda shape: pl.BlockSpec(shape, lambda b, i: (0, 0))
    out_shape, out_specs, route_in = _route_specs(bsz, s, d, ts, n_exp)
    return pl.pallas_call(
        functools.partial(_gmlproute_kernel, ts=ts, width=width, n_exp=n_exp),
        out_shape=out_shape,
        grid=(bsz, s // ts),
        in_specs=[
            pl.BlockSpec((1, ts, d), lambda b, i: (b, i, 0)),
            vec, vec, vec,
            const2((1, d)),
            const2((d, 2 * width)),
            const2((1, width)),
            const2((1, width)),
            pl.BlockSpec((GMLP_GROUPS, CHUNK, CHUNK), lambda b, i: (0, 0, 0)),
            const2((CHUNK, GMLP_GROUPS)),
            const2((width, d)),
        ] + route_in,
        out_specs=out_specs,
        scratch_shapes=[pltpu.VMEM((ts, width), BF16), pltpu.VMEM((n_exp, 1), F32)],
        compiler_params=_cparams(("arbitrary", "arbitrary")),
        name="gmlproute",
    )(x, sh, sc, gt, g.reshape(1, d), w_in.astype(BF16), v_g.reshape(1, width),
      v_b.reshape(1, width), w_s, b_s.T, w_out.astype(BF16),
      fsh, fsc, fg.reshape(1, d), router_w.T.astype(BF16), router_b.reshape(n_exp, 1))


def _mixroute_kernel(x_ref, o_ref, sg_ref, gt_ref, wo_ref, sh_ref, sc_ref, g_ref, rwt_ref, rb_ref,
                     xo_ref, h_ref, lpos_ref, gate_ref, tab_ref, tot_ref, carry_ref, *, ts, n_exp):
    gated = o_ref[0].astype(F32) * sg_ref[0].astype(F32)
    x = x_ref[0] + gt_ref[0] * _dot(gated.astype(BF16), wo_ref[...])
    xo_ref[0] = x
    _route_tile(x, sh_ref, sc_ref, g_ref, rwt_ref, rb_ref,
                h_ref, lpos_ref, gate_ref, tab_ref, tot_ref, carry_ref, ts, n_exp)


def _route_tile(x, sh_ref, sc_ref, g_ref, rwt_ref, rb_ref,
                h_ref, lpos_ref, gate_ref, tab_ref, tot_ref, carry_ref, ts, n_exp):
    first = jnp.logical_and(pl.program_id(0) == 0, pl.program_id(1) == 0)

    @pl.when(first)
    def _():
        carry_ref[...] = jnp.zeros_like(carry_ref)

    h = _ada_norm(x, g_ref[...], sh_ref[0], sc_ref[0]).astype(BF16)
    h_ref[0] = h
    logits = _dot_nt(rwt_ref[...], h) + rb_ref[...]

    eio = lax.broadcasted_iota(jnp.int32, (n_exp, ts), 0)
    work = logits
    vals, hots = [], []
    for _ in range(TOP_K):
        m = jnp.max(work, axis=0, keepdims=True)
        sel = jnp.min(jnp.where(work == m, eio, n_exp), axis=0, keepdims=True)
        hot = eio == sel
        vals.append(m)
        hots.append(hot)
        work = jnp.where(hot, -jnp.inf, work)

    exps = [jnp.exp(v - vals[0]) for v in vals]
    denom = exps[0]
    for e in exps[1:]:
        denom = denom + e
    gates = [e / denom for e in exps]

    chosen = hots[0]
    for hot in hots[1:]:
        chosen = jnp.logical_or(chosen, hot)
    chosen_f = jnp.where(chosen, 1.0, 0.0)
    r = lax.broadcasted_iota(jnp.int32, (ts, ts), 0)
    c = lax.broadcasted_iota(jnp.int32, (ts, ts), 1)
    before = jnp.where(r < c, 1.0, 0.0).astype(BF16)
    prefix = _dot(chosen_f.astype(BF16), before)
    count = jnp.sum(chosen_f, axis=1, keepdims=True)
    seg_len = jnp.floor((count + (SEG_ALIGN - 1)) * (1.0 / SEG_ALIGN)) * SEG_ALIGN
    er = lax.broadcasted_iota(jnp.int32, (n_exp, n_exp), 0)
    ec = lax.broadcasted_iota(jnp.int32, (n_exp, n_exp), 1)
    lower = jnp.where(ec < er, 1.0, 0.0).astype(BF16)
    seg_start = _dot(lower, jnp.broadcast_to(seg_len, (n_exp, LANES)).astype(BF16))[:, 0:1]
    base = seg_start + prefix
    lpos = [jnp.sum(jnp.where(hot, base, 0.0), axis=0, keepdims=True) for hot in hots]

    lpos_ref[...] = jnp.concatenate(lpos, axis=0).astype(jnp.int32)
    gate_ref[...] = jnp.concatenate(gates, axis=0)
    used = carry_ref[...]
    lane = lax.broadcasted_iota(jnp.int32, (n_exp, LANES), 1)
    tab_ref[0] = jnp.where(lane == 0, seg_start,
                           jnp.where(lane == 1, used, jnp.where(lane == 2, seg_len, 0.0)))
    carry_ref[...] = used + seg_len
    tot_ref[...] = jnp.broadcast_to(used + seg_len, tot_ref.shape)


def _mix_route(x, o, sg, gt, w_o, sh, sc, g, router_w, router_b, ts):
    bsz, s, d = x.shape
    n_exp = router_w.shape[1]
    vec = pl.BlockSpec((1, 1, d), lambda b, i: (b, 0, 0))
    tile = pl.BlockSpec((1, ts, d), lambda b, i: (b, i, 0))
    out_shape, out_specs, route_in = _route_specs(bsz, s, d, ts, n_exp)
    return pl.pallas_call(
        functools.partial(_mixroute_kernel, ts=ts, n_exp=n_exp),
        out_shape=out_shape,
        grid=(bsz, s // ts),
        in_specs=[tile, tile, tile, vec, pl.BlockSpec((d, d), lambda b, i: (0, 0))] + route_in,
        out_specs=out_specs,
        scratch_shapes=[pltpu.VMEM((n_exp, 1), F32)],
        compiler_params=_cparams(("arbitrary", "arbitrary")),
        name="mixroute",
    )(x, o, sg, gt, w_o.astype(BF16),
      sh, sc, g.reshape(1, d), router_w.T.astype(BF16), router_b.reshape(n_exp, 1))


def _wait_rows(src, dst, sem):
    pltpu.make_async_copy(src, dst, sem).wait()


def _chunk_sizes(max_rows):
    sizes = []
    size = SEG_ALIGN
    while size <= max_rows:
        sizes.append(size)
        size *= 2
    return sizes


def _for_each_chunk(length, max_rows, fn):
    for size in reversed(_chunk_sizes(max_rows)):
        shift = size.bit_length() - 1
        @pl.when(((length >> shift) & 1) == 1)
        def _():
            fn(pl.multiple_of((length >> (shift + 1)) << (shift + 1), SEG_ALIGN), size)


def _segment_copies(tile, n_exp, max_rows, src_ref, dst_ref, len_ref, make_copy):
    for e in range(n_exp):
        s = tile * n_exp + e
        src, dst = src_ref[s], dst_ref[s]
        _for_each_chunk(len_ref[s], max_rows,
                        lambda off, size: make_copy(pl.multiple_of(src + off, SEG_ALIGN),
                                                    pl.multiple_of(dst + off, SEG_ALIGN),
                                                    size).start())


def _segment_wait(tile, n_exp, lrows, src_ref, len_ref, make_copy):
    last = tile * n_exp + n_exp - 1
    total = src_ref[last] + len_ref[last]
    _for_each_chunk(total, lrows, lambda off, size: make_copy(0, 0, size).wait())


def _dispatch_kernel(pstart_ref, pend_ref, src_ref, dst_ref, len_ref, lpos_ref, h_ref, buf_hbm,
                     comp_ref, zero_ref, sem_seg, sem_zero,
                     *, ts, lrows, tm, n_exp, n_blocks, n_tiles):
    i = pl.program_id(0)
    slot = i % 2

    def seg_copy_from(buf_slot):
        def seg_copy(local_row, global_row, size):
            return pltpu.make_async_copy(comp_ref.at[buf_slot, pl.ds(local_row, size)],
                                         buf_hbm.at[pl.ds(global_row, size)], sem_seg.at[buf_slot])
        return seg_copy

    @pl.when(i == 0)
    def _():
        zero_ref[...] = jnp.zeros_like(zero_ref)
        for e in range(n_exp):
            @pl.when(pend_ref[e] > pstart_ref[e])
            def _():
                last = pl.multiple_of(pend_ref[e] - tm, tm)
                pltpu.make_async_copy(zero_ref, buf_hbm.at[pl.ds(last, tm)], sem_zero).start()
        n_used = pend_ref[n_exp - 1] // tm

        def zero_tail(b, carry):
            off = pl.multiple_of(b * tm, tm)
            pltpu.make_async_copy(zero_ref, buf_hbm.at[pl.ds(off, tm)], sem_zero).start()
            return carry

        def wait_tail(b, carry):
            _wait_rows(zero_ref, buf_hbm.at[pl.ds(0, tm)], sem_zero)
            return carry

        lax.fori_loop(n_used, n_blocks, zero_tail, 0)
        for e in range(n_exp):
            @pl.when(pend_ref[e] > pstart_ref[e])
            def _():
                _wait_rows(zero_ref, buf_hbm.at[pl.ds(0, tm)], sem_zero)
        lax.fori_loop(n_used, n_blocks, wait_tail, 0)

    @pl.when(i > 1)
    def _():
        _segment_wait(i - 2, n_exp, lrows, src_ref, len_ref, seg_copy_from(slot))

    lpos = lpos_ref[...]
    rows = lax.broadcasted_iota(jnp.int32, (MXU_ROWS, ts), 0)
    h = h_ref[...]
    for c in range(lrows // MXU_ROWS):
        rel = lpos - c * MXU_ROWS
        hit = rows == rel[0:1, :]
        for k in range(1, TOP_K):
            hit = jnp.logical_or(hit, rows == rel[k:k + 1, :])
        onehot = jnp.where(hit, 1.0, 0.0).astype(BF16)
        comp_ref[slot, c * MXU_ROWS:(c + 1) * MXU_ROWS, :] = _dot(onehot, h)
    _segment_copies(i, n_exp, ts, src_ref, dst_ref, len_ref, seg_copy_from(slot))

    @pl.when(i == n_tiles - 1)
    def _():
        @pl.when(i > 0)
        def _():
            _segment_wait(i - 1, n_exp, lrows, src_ref, len_ref, seg_copy_from(1 - slot))
        _segment_wait(i, n_exp, lrows, src_ref, len_ref, seg_copy_from(slot))


def _dispatch(pstart, pends, seg_src, seg_dst, seg_len, lpos, h, n_rows, ts, lrows):
    t, d = h.shape
    tm = EXPERT_ROWS
    n_tiles = t // ts
    tile_map = lambda i, *_: (i, 0)
    return pl.pallas_call(
        functools.partial(_dispatch_kernel, ts=ts, lrows=lrows, tm=tm, n_exp=pstart.shape[0],
                          n_blocks=n_rows // tm, n_tiles=n_tiles),
        out_shape=jax.ShapeDtypeStruct((n_rows, d), F32),
        grid_spec=pltpu.PrefetchScalarGridSpec(
            num_scalar_prefetch=5,
            grid=(n_tiles,),
            in_specs=[
                pl.BlockSpec((TOP_K, ts), lambda i, *_: (0, i)),
                pl.BlockSpec((ts, d), tile_map),
            ],
            out_specs=pl.BlockSpec(memory_space=pl.ANY),
            scratch_shapes=[
                pltpu.VMEM((2, lrows, d), F32),
                pltpu.VMEM((tm, d), F32),
                pltpu.SemaphoreType.DMA((2,)),
                pltpu.SemaphoreType.DMA,
            ],
        ),
        compiler_params=_cparams(("arbitrary",)),
        name="dispatch",
    )(pstart, pends, seg_src, seg_dst, seg_len, lpos, h)


def _expert_kernel(be_ref, nu_ref, nxt_ref, eslot_ref, x_ref, w1_hbm, b1_ref, w2_hbm, b2_ref, y_ref,
                   w1f_ref, w2f_ref, w1b_ref, w2b_ref, sem_w, *, d_exp, layer):
    i = pl.program_id(0)
    e = be_ref[i]
    changed = jnp.logical_or(i == 0, e != be_ref[jnp.maximum(i - 1, 0)])

    def weight_copies(expert, slot):
        return (pltpu.make_async_copy(w1_hbm.at[layer, expert], w1f_ref.at[slot], sem_w.at[slot]),
                pltpu.make_async_copy(w2_hbm.at[layer, expert], w2f_ref.at[slot], sem_w.at[slot]))

    @pl.when(i == 0)
    def _():
        for cp in weight_copies(e, eslot_ref[e]):
            cp.start()

    @pl.when(changed)
    def _():
        slot = eslot_ref[e]
        for cp in weight_copies(e, slot):
            cp.wait()
        w1b_ref[...] = w1f_ref[slot].astype(BF16)
        w2b_ref[...] = w2f_ref[slot].astype(BF16)
        nxt = nxt_ref[e]

        @pl.when(nxt >= 0)
        def _():
            for cp in weight_copies(nxt, 1 - slot):
                cp.start()

    @pl.when(i < nu_ref[0])
    def _():
        gu = _dot(x_ref[...].astype(BF16), w1b_ref[...]) + b1_ref[0, 0]
        glu = jnp.minimum(gu[:, :d_exp], SWIGLU_LIMIT)
        lin = jnp.clip(gu[:, d_exp:], -SWIGLU_LIMIT, SWIGLU_LIMIT)
        act = (lin + 1.0) * (glu * jax.nn.sigmoid(SWIGLU_ALPHA * glu))
        y_ref[...] = _dot(act.astype(BF16), w2b_ref[...]) + b2_ref[0, 0]

    @pl.when(i >= nu_ref[0])
    def _():
        y_ref[...] = jnp.zeros_like(y_ref)


def _experts(block_expert, n_used, next_expert, expert_slot, buf, w1, b1, w2, b2, layer):
    n_rows, d = buf.shape
    _, n_exp, _, d_exp2 = w1.shape
    d_exp = d_exp2 // 2
    tm = EXPERT_ROWS
    n_blocks = n_rows // tm

    def in_map(i, be, nu, *_):
        return (jnp.minimum(i, nu[0] - 1), 0)

    def out_map(i, *_):
        return (i, 0)

    def b_map(i, be, *_):
        return (layer, be[i], 0, 0)

    return pl.pallas_call(
        functools.partial(_expert_kernel, d_exp=d_exp, layer=layer),
        out_shape=jax.ShapeDtypeStruct((n_rows, d), F32),
        grid_spec=pltpu.PrefetchScalarGridSpec(
            num_scalar_prefetch=4,
            grid=(n_blocks,),
            in_specs=[
                pl.BlockSpec((tm, d), in_map),
                pl.BlockSpec(memory_space=pl.ANY),
                pl.BlockSpec((1, 1, 1, d_exp2), b_map),
                pl.BlockSpec(memory_space=pl.ANY),
                pl.BlockSpec((1, 1, 1, d), b_map),
            ],
            out_specs=pl.BlockSpec((tm, d), out_map),
            scratch_shapes=[
                pltpu.VMEM((2, d, d_exp2), F32),
                pltpu.VMEM((2, d_exp, d), F32),
                pltpu.VMEM((d, d_exp2), BF16),
                pltpu.VMEM((d_exp, d), BF16),
                pltpu.SemaphoreType.DMA((2,)),
            ],
        ),
        compiler_params=_cparams(("arbitrary",)),
        name="experts",
    )(block_expert, n_used, next_expert, expert_slot, buf, w1,
      b1.reshape(b1.shape[0], n_exp, 1, d_exp2), w2, b2.reshape(b2.shape[0], n_exp, 1, d))


def _combine_kernel(src_ref, dst_ref, len_ref, y_hbm, lpos_ref, gate_ref, x_ref, gt_ref, fg_ref,
                    o_ref, ycomp_ref, sem_seg, *, ts, lrows, n_exp, n_tiles, final_norm):
    i = pl.program_id(0)
    slot = i % 2

    def seg_copy_into(buf_slot):
        def seg_copy(local_row, global_row, size):
            return pltpu.make_async_copy(y_hbm.at[pl.ds(global_row, size)],
                                         ycomp_ref.at[buf_slot, pl.ds(local_row, size)],
                                         sem_seg.at[buf_slot])
        return seg_copy

    @pl.when(i == 0)
    def _():
        ycomp_ref[...] = jnp.zeros_like(ycomp_ref)
        _segment_copies(0, n_exp, ts, src_ref, dst_ref, len_ref, seg_copy_into(0))

    @pl.when(i + 1 < n_tiles)
    def _():
        _segment_copies(i + 1, n_exp, ts, src_ref, dst_ref, len_ref, seg_copy_into(1 - slot))

    _segment_wait(i, n_exp, lrows, src_ref, len_ref, seg_copy_into(slot))
    y = ycomp_ref[slot].astype(BF16)

    cols = lax.broadcasted_iota(jnp.int32, (COMBINE_CHUNK, lrows), 1)
    for c in range(ts // COMBINE_CHUNK):
        rs = slice(c * COMBINE_CHUNK, (c + 1) * COMBINE_CHUNK)
        lpos = lpos_ref[rs, :]
        gates = gate_ref[rs, :]
        sel = jnp.zeros((COMBINE_CHUNK, lrows), F32)
        for k in range(TOP_K):
            sel = jnp.where(cols == lpos[:, k:k + 1], gates[:, k:k + 1], sel)
        out = x_ref[rs, :] + gt_ref[0] * _dot(sel.astype(BF16), y)
        if final_norm:
            ms = jnp.mean(out * out, axis=-1, keepdims=True)
            out = out * lax.rsqrt(ms + EPS) * fg_ref[...]
        o_ref[rs, :] = out


def _combine(seg_src, seg_dst, seg_len, y_buf, lpos_t, gates_t, x2d, gt, final_g, s, final_norm,
             ts, lrows, n_exp):
    t, d = x2d.shape
    per_b = s // ts
    tile_map = lambda i, *_: (i, 0)
    return pl.pallas_call(
        functools.partial(_combine_kernel, ts=ts, lrows=lrows, n_exp=n_exp, n_tiles=t // ts,
                          final_norm=final_norm),
        out_shape=jax.ShapeDtypeStruct((t, d), F32),
        grid_spec=pltpu.PrefetchScalarGridSpec(
            num_scalar_prefetch=3,
            grid=(t // ts,),
            in_specs=[
                pl.BlockSpec(memory_space=pl.ANY),
                pl.BlockSpec((ts, TOP_K), tile_map),
                pl.BlockSpec((ts, TOP_K), tile_map),
                pl.BlockSpec((ts, d), tile_map),
                pl.BlockSpec((1, 1, d), lambda i, *_: (i // per_b, 0, 0)),
                pl.BlockSpec((1, d), lambda i, *_: (0, 0)),
            ],
            out_specs=pl.BlockSpec((ts, d), tile_map),
            scratch_shapes=[
                pltpu.VMEM((2, lrows, d), F32),
                pltpu.SemaphoreType.DMA((2,)),
            ],
        ),
        compiler_params=_cparams(("arbitrary",)),
        name="combine",
    )(seg_src, seg_dst, seg_len, y_buf, lpos_t, gates_t, x2d, gt, final_g.reshape(1, d))


def _moe_tile(s):
    return min(MOE_TILE, s)


def _moe_layer(routed, gt, w1, b1, w2, b2, layer, final_g, final_norm):
    x, h, lpos, gates, tab, tot = routed
    bsz, s, d = x.shape
    t = bsz * s
    n_exp = tot.shape[0]
    tm = EXPERT_ROWS
    ts = _moe_tile(s)
    n_tiles = t // ts
    lrows = -(-(ts * TOP_K + n_exp * (SEG_ALIGN - 1)) // MXU_ROWS) * MXU_ROWS

    rows = tot[:, 0].astype(jnp.int32)
    padded = (rows + tm - 1) // tm * tm
    pends = jnp.cumsum(padded).astype(jnp.int32)
    pstart = pends - padded
    n_blocks = -(-(t * TOP_K + n_tiles * n_exp * (SEG_ALIGN - 1) + n_exp * (tm - 1)) // tm)
    n_used = pends[-1] // tm
    blk = jnp.minimum(jnp.arange(n_blocks, dtype=jnp.int32), n_used - 1)
    block_expert = jnp.minimum(
        jnp.sum((pends[None, :] <= (blk * tm)[:, None]).astype(jnp.int32), axis=1), n_exp - 1)
    tab = tab[:, :, :3].astype(jnp.int32)
    seg_src = tab[:, :, 0].reshape(-1)
    seg_dst = (pstart[None, :] + tab[:, :, 1]).reshape(-1)
    seg_len = tab[:, :, 2].reshape(-1)

    buf = _dispatch(pstart, pends, seg_src, seg_dst, seg_len, lpos, h.reshape(t, d),
                    n_blocks * tm, ts, lrows)
    eid = jnp.arange(n_exp, dtype=jnp.int32)
    nonempty = rows > 0
    later = jnp.where(jnp.logical_and(nonempty[None, :], eid[None, :] > eid[:, None]),
                      eid[None, :], n_exp)
    next_expert = jnp.min(later, axis=1)
    next_expert = jnp.where(next_expert == n_exp, -1, next_expert).astype(jnp.int32)
    expert_slot = ((jnp.cumsum(nonempty) - nonempty) % 2).astype(jnp.int32)
    y_buf = _experts(block_expert, n_used.reshape(1), next_expert, expert_slot, buf,
                     w1, b1, w2, b2, layer)
    out = _combine(seg_src, seg_dst, seg_len, y_buf, lpos.T, gates.T, x.reshape(t, d), gt, final_g,
                   s, final_norm, ts, lrows, n_exp)
    return out.reshape(bsz, s, d)


def _head_rms(x, seg, seg_t, inv_dh):
    hi, lo = _split2(x * x)
    ssum = _dot(hi, seg) + _dot(lo, seg)
    inv = lax.rsqrt(ssum * inv_dh + EPS)
    ihi, ilo = _split2(inv)
    return _dot(ihi, seg_t) + _dot(ilo, seg_t)


def _kvq_kernel(x_ref, ksh_ref, ksc_ref, msh_ref, msc_ref, kg_ref, mg_ref,
                wk_ref, wvt_ref, wf_ref, fb_ref, kng_ref, qng_ref, wq_ref, wg_ref,
                place_ref, cst_ref,
                k_ref, kx_ref, vt_ref, q_ref, qx_ref, sg_ref, ccol_ref, *, ts, d):
    @pl.when(pl.program_id(1) == 0)
    def _():
        ccol_ref[...] = jnp.zeros_like(ccol_ref)

    x = x_ref[0]
    ms = jnp.mean(x * x, axis=-1, keepdims=True)
    xn = x * lax.rsqrt(ms + EPS)
    hk = ((xn * kg_ref[...]) * (1.0 + ksc_ref[0]) + ksh_ref[0]).astype(BF16)
    hq = ((xn * mg_ref[...]) * (1.0 + msc_ref[0]) + msh_ref[0]).astype(BF16)

    di = lax.broadcasted_iota(jnp.int32, (d, LANES), 0) // HEAD_DIM
    hi_ = lax.broadcasted_iota(jnp.int32, (d, LANES), 1)
    seg = jnp.where(di == hi_, 1.0, 0.0).astype(BF16)
    dt = lax.broadcasted_iota(jnp.int32, (LANES, d), 1) // HEAD_DIM
    ht = lax.broadcasted_iota(jnp.int32, (LANES, d), 0)
    seg_t = jnp.where(dt == ht, 1.0, 0.0).astype(BF16)
    inv_dh = 1.0 / HEAD_DIM

    k = _dot(hk, wk_ref[...])
    k_ref[0] = (k * _head_rms(k, seg, seg_t, inv_dh) * kng_ref[...]).astype(BF16)
    vt_ref[0] = _dot_nt(wvt_ref[...], hk).astype(BF16)

    q = _dot(hq, wq_ref[...])
    qn = q * _head_rms(q, seg, seg_t, inv_dh) * qng_ref[...]
    q_ref[0] = (qn * (HEAD_DIM ** -0.5 * LOG2E)).astype(BF16)
    sg_ref[0] = jax.nn.sigmoid(_dot(hq, wg_ref[...])).astype(BF16)

    lf = jax.nn.log_sigmoid(_dot(hk, wf_ref[...]) + fb_ref[...])
    r = lax.broadcasted_iota(jnp.int32, (ts, ts), 0)
    c = lax.broadcasted_iota(jnp.int32, (ts, ts), 1)
    lower = jnp.where(c <= r, 1.0, 0.0).astype(BF16)
    a, b, e = _split3(lf)
    cum = ccol_ref[...] + (_dot(lower, a) + _dot(lower, b) + _dot(lower, e))
    ccol_ref[...] = cum[ts - 1:ts, :]
    a, b, e = _split3(cum * LOG2E)
    extra = (_dot(a, place_ref[0]) + _dot(b, place_ref[1]) + _dot(e, place_ref[2])
             + cst_ref[...]).astype(BF16)
    kx_ref[0] = extra[:, :LANES]
    qx_ref[0] = extra[:, LANES:]


def _bias_lane_tables(n_heads):
    assert 2 * N_PARTS * n_heads <= LANES
    place = np.zeros((N_PARTS, LANES, 2 * LANES), np.float32)
    cst = np.zeros((1, 2 * LANES), np.float32)
    for h in range(n_heads):
        base = 2 * N_PARTS * h
        for part in range(N_PARTS):
            place[part, h, base + part] = 1.0
            place[part, h, LANES + base + N_PARTS + part] = 1.0
        cst[0, base + N_PARTS:base + 2 * N_PARTS] = 1.0
        cst[0, LANES + base:LANES + base + N_PARTS] = -1.0
    return jnp.asarray(place, BF16), jnp.asarray(cst, F32)


def _kvq(x, ksh, ksc, msh, msc, kv_norm_g, mix_norm_g, kv_w, k_norm_g, fgate_b,
         w_qg, q_norm_g, ts=256):
    bsz, s, d = x.shape
    n_heads = d // HEAD_DIM
    ts = min(ts, s)
    wk = kv_w[:, :d].astype(BF16)
    wvt = kv_w[:, d:2 * d].T.astype(BF16)
    wf = kv_w[:, 2 * d:]
    wf_pad = jnp.pad(wf, ((0, 0), (0, LANES - n_heads))).astype(BF16)
    fb_pad = jnp.pad(fgate_b, (0, LANES - n_heads)).reshape(1, LANES)
    place, cst = _bias_lane_tables(n_heads)
    vec = pl.BlockSpec((1, 1, d), lambda b, i: (b, 0, 0))
    c2 = lambda shape: pl.BlockSpec(shape, lambda b, i: (0, 0))
    tile = pl.BlockSpec((1, ts, d), lambda b, i: (b, i, 0))
    tile_x = pl.BlockSpec((1, ts, LANES), lambda b, i: (b, i, 0))
    return pl.pallas_call(
        functools.partial(_kvq_kernel, ts=ts, d=d),
        out_shape=(
            jax.ShapeDtypeStruct((bsz, s, d), BF16),
            jax.ShapeDtypeStruct((bsz, s, LANES), BF16),
            jax.ShapeDtypeStruct((bsz, d, s), BF16),
            jax.ShapeDtypeStruct((bsz, s, d), BF16),
            jax.ShapeDtypeStruct((bsz, s, LANES), BF16),
            jax.ShapeDtypeStruct((bsz, s, d), BF16),
        ),
        grid=(bsz, s // ts),
        in_specs=[
            tile, vec, vec, vec, vec,
            c2((1, d)), c2((1, d)),
            c2((d, d)), c2((d, d)), c2((d, LANES)), c2((1, LANES)),
            c2((1, d)), c2((1, d)),
            c2((d, d)), c2((d, d)),
            pl.BlockSpec((N_PARTS, LANES, 2 * LANES), lambda b, i: (0, 0, 0)),
            c2((1, 2 * LANES)),
        ],
        out_specs=(
            tile, tile_x,
            pl.BlockSpec((1, d, ts), lambda b, i: (b, 0, i)),
            tile, tile_x,
            tile,
        ),
        scratch_shapes=[pltpu.VMEM((1, LANES), F32)],
        compiler_params=_cparams(("arbitrary", "arbitrary")),
        name="kvq",
    )(x, ksh, ksc, msh, msc, kv_norm_g.reshape(1, d), mix_norm_g.reshape(1, d),
      wk, wvt, wf_pad, fb_pad,
      jnp.tile(k_norm_g, n_heads).reshape(1, d), jnp.tile(q_norm_g, n_heads).reshape(1, d),
      w_qg[:, :d].astype(BF16), w_qg[:, d:].astype(BF16), place, cst)


def _attn_kernel(q_ref, qx_ref, k_ref, kx_ref, vt_ref, o_ref, qh_ref, s0_ref, s1_ref, acc_ref,
                 *, tq, tk, npp):
    nh = 2 * npp
    first_pair = pl.program_id(1) * npp
    qi = pl.program_id(2)
    lane = lax.broadcasted_iota(jnp.int32, (tq, 2 * LANES), 1)
    for pp in range(npp):
        q = jnp.concatenate([q_ref[0, :, pp * LANES:(pp + 1) * LANES], qx_ref[0]], axis=1)
        zero = jnp.zeros_like(q)
        for hh in range(2):
            feat = jnp.logical_and(lane >= hh * HEAD_DIM, lane < (hh + 1) * HEAD_DIM)
            first_bias = LANES + 2 * N_PARTS * (2 * (first_pair + pp) + hh)
            bias = jnp.logical_and(lane >= first_bias, lane < first_bias + 2 * N_PARTS)
            qh_ref[2 * pp + hh] = jnp.where(feat | bias, q, zero)
    acc_ref[...] = jnp.zeros_like(acc_ref)
    ones = jnp.ones((DENOM_ROWS, tk), BF16)

    def scores_into(kb, s_ref):
        k0 = pl.multiple_of(kb * tk, tk)
        kx = kx_ref[0, pl.ds(k0, tk), :]
        for pp in range(npp):
            ks = jnp.concatenate([k_ref[0, pl.ds(k0, tk), pp * LANES:(pp + 1) * LANES], kx], axis=1)
            for hh in range(2):
                s_ref[2 * pp + hh] = _dot_nt(ks, qh_ref[2 * pp + hh])

    def softmax_pv(kb, s_ref, ms, masked):
        k0 = pl.multiple_of(kb * tk, tk)
        vt = vt_ref[0, :, pl.ds(k0, tk)]
        if masked:
            keep = (lax.broadcasted_iota(jnp.int32, (tk, tq), 0)
                    <= lax.broadcasted_iota(jnp.int32, (tk, tq), 1))
        out = []
        for h in range(nh):
            st = s_ref[h]
            if masked:
                st = jnp.where(keep, st, NEG_INF)
            m_new = jnp.maximum(ms[h], jnp.max(st, axis=0, keepdims=True))
            p = jnp.exp2(st - m_new).astype(BF16)
            alpha = jnp.exp2(ms[h] - m_new)
            vth = jnp.concatenate([vt[h * HEAD_DIM:(h + 1) * HEAD_DIM, :], ones], axis=0)
            acc_ref[h] = alpha * acc_ref[h] + _dot(vth, p)
            out.append(m_new)
        return tuple(out)

    def two_blocks(j, ms):
        scores_into(2 * j + 1, s1_ref)
        ms = softmax_pv(2 * j, s0_ref, ms, False)
        scores_into(2 * j + 2, s0_ref)
        return softmax_pv(2 * j + 1, s1_ref, ms, False)

    scores_into(0, s0_ref)
    ms = lax.fori_loop(0, qi // 2, two_blocks,
                       tuple(jnp.full((1, tq), NEG_INF, F32) for _ in range(nh)))

    @pl.when(qi % 2 == 1)
    def _():
        scores_into(qi, s1_ref)
        softmax_pv(qi, s1_ref, softmax_pv(qi - 1, s0_ref, ms, False), True)

    @pl.when(qi % 2 == 0)
    def _():
        softmax_pv(qi, s0_ref, ms, True)

    ot = jnp.concatenate(
        [acc_ref[h, :HEAD_DIM, :] / acc_ref[h, HEAD_DIM:HEAD_DIM + 1, :] for h in range(nh)],
        axis=0)
    o_ref[0] = ot.T.astype(BF16)


def _attention(q, qx, k, kx, vt, tq=512, npp=4):
    bsz, s, d = q.shape
    n_pairs = d // LANES
    npp = min(npp, n_pairs)
    tq = min(tq, s)
    tk = tq
    w = npp * LANES
    return pl.pallas_call(
        functools.partial(_attn_kernel, tq=tq, tk=tk, npp=npp),
        out_shape=jax.ShapeDtypeStruct((bsz, s, d), BF16),
        grid=(bsz, n_pairs // npp, s // tq),
        in_specs=[
            pl.BlockSpec((1, tq, w), lambda b, j, i: (b, i, j)),
            pl.BlockSpec((1, tq, LANES), lambda b, j, i: (b, i, 0)),
            pl.BlockSpec((1, s, w), lambda b, j, i: (b, 0, j)),
            pl.BlockSpec((1, s, LANES), lambda b, j, i: (b, 0, 0)),
            pl.BlockSpec((1, w, s), lambda b, j, i: (b, j, 0)),
        ],
        out_specs=pl.BlockSpec((1, tq, w), lambda b, j, i: (b, i, j)),
        scratch_shapes=[
            pltpu.VMEM((2 * npp, tq, 2 * LANES), BF16),
            pltpu.VMEM((2 * npp, tk, tq), F32),
            pltpu.VMEM((2 * npp, tk, tq), F32),
            pltpu.VMEM((2 * npp, HEAD_DIM + DENOM_ROWS, tq), F32),
        ],
        compiler_params=_cparams(("parallel", "parallel", "arbitrary")),
        name="attn",
    )(q, qx, k, kx, vt)


def _chunks(mod, n):
    return [m[:, None, :] for m in jnp.split(mod, n, axis=-1)]


def kernel(x, c, ada_w, ada_b, norm_mix_g, norm_ffn_g, gmlp_w_in, gmlp_v_g, gmlp_v_b, gmlp_w_s,
           gmlp_b_s, gmlp_w_out, kv_ada_w, kv_ada_b, kv_norm_g, kv_w, k_norm_g, fgate_b, fox_w_qg,
           q_norm_g, fox_w_o, router_w, router_b, exp_w1, exp_b1, exp_w2, exp_b2, final_g):
    mods = _ada(c, ada_w, ada_b)
    kv_mod = _ada(c, kv_ada_w[None], kv_ada_b[None])[0]

    sh_m, sc_m, gt_m, sh_f, sc_f, gt_f = _chunks(mods[0], 6)
    ts = _moe_tile(x.shape[1])
    routed = _gmlp_route(x, sh_m, sc_m, gt_m, norm_mix_g[0], gmlp_w_in[0], gmlp_v_g[0],
                         gmlp_v_b[0], gmlp_w_s[0], gmlp_b_s[0], gmlp_w_out[0],
                         sh_f, sc_f, norm_ffn_g[0], router_w[0], router_b[0], ts)
    x = _moe_layer(routed, gt_f, exp_w1, exp_b1, exp_w2, exp_b2, 0, final_g, False)

    sh_m, sc_m, gt_m, sh_f, sc_f, gt_f = _chunks(mods[1], 6)
    ksh, ksc = _chunks(kv_mod, 2)
    k, kx, vt, q, qx, sg = _kvq(x, ksh, ksc, sh_m, sc_m, kv_norm_g, norm_mix_g[1], kv_w,
                                k_norm_g, fgate_b, fox_w_qg[0], q_norm_g[0])
    o = _attention(q, qx, k, kx, vt)
    routed = _mix_route(x, o, sg, gt_m, fox_w_o[0], sh_f, sc_f, norm_ffn_g[1],
                        router_w[1], router_b[1], ts)
    return _moe_layer(routed, gt_f, exp_w1, exp_b1, exp_w2, exp_b2, 1, final_g, True)
```

```python
import functools

import numpy as np
import jax
import jax.numpy as jnp
from jax import lax
from jax.experimental import pallas as pl
from jax.experimental.pallas import tpu as pltpu

F32 = jnp.float32
BF16 = jnp.bfloat16

CHUNK = 128
GMLP_GROUPS = 8
HEAD_DIM = 64
TOP_K = 4
SWIGLU_LIMIT = 7.0
SWIGLU_ALPHA = 1.702
EPS = 1e-6
NEG_INF = -1e30
LOG2E = 1.4426950408889634
N_PARTS = 3
DENOM_ROWS = 16

LANES = 128
MXU_ROWS = 256
EXPERT_ROWS = 2 * MXU_ROWS
SEG_ALIGN = 8
MOE_TILE = 512
COMBINE_CHUNK = 128
VMEM_LIMIT = 56 * 1024 * 1024


def _cparams(sem, vmem=VMEM_LIMIT):
    return pltpu.CompilerParams(dimension_semantics=sem, vmem_limit_bytes=vmem)


def _divisor_tile(n, pref):
    if n <= pref:
        return n
    return max(t for t in range(LANES, pref + 1, LANES) if n % t == 0)


def _dot(a, b):
    return jnp.dot(a, b, preferred_element_type=F32)


def _dot_nt(a, b):
    return lax.dot_general(a, b, (((1,), (1,)), ((), ())), preferred_element_type=F32)


def _split3(x):
    hi = x.astype(BF16)
    r = x - hi.astype(F32)
    mid = r.astype(BF16)
    lo = (r - mid.astype(F32)).astype(BF16)
    return hi, mid, lo


def _split2(x):
    hi = x.astype(BF16)
    lo = (x - hi.astype(F32)).astype(BF16)
    return hi, lo


def _gelu_exact(x):
    return 0.5 * x * (1.0 + lax.erf(x * (0.5 ** 0.5)))


def _ada_norm(x, g, shift, scale):
    ms = jnp.mean(x * x, axis=-1, keepdims=True)
    y = x * lax.rsqrt(ms + EPS) * g
    return y * (1.0 + scale) + shift


def _ada_kernel(c_ref, w_ref, b_ref, o_ref):
    c = c_ref[...]
    sc = c * jax.nn.sigmoid(c)
    o_ref[0] = _dot(sc.astype(BF16), w_ref[0].astype(BF16)) + b_ref[0]


def _ada(c, w, b):
    nl, d, n = w.shape
    bsz = c.shape[0]
    tn = _divisor_tile(n, 2048)
    return pl.pallas_call(
        _ada_kernel,
        out_shape=jax.ShapeDtypeStruct((nl, bsz, n), F32),
        grid=(nl, n // tn),
        in_specs=[
            pl.BlockSpec((bsz, d), lambda l, j: (0, 0)),
            pl.BlockSpec((1, d, tn), lambda l, j: (l, 0, j)),
            pl.BlockSpec((1, 1, tn), lambda l, j: (l, 0, j)),
        ],
        out_specs=pl.BlockSpec((1, bsz, tn), lambda l, j: (l, 0, j)),
        compiler_params=_cparams(("arbitrary", "arbitrary")),
        name="ada",
    )(c, w, b.reshape(nl, 1, n))


def _gmlproute_kernel(x_ref, sh_ref, sc_ref, gt_ref, g_ref, win_ref, vg_ref, vb_ref,
                      ws_ref, bst_ref, wout_ref, fsh_ref, fsc_ref, fg_ref, rwt_ref, rb_ref,
                      xo_ref, h_ref, lpos_ref, gate_ref, tab_ref, tot_ref, gated_ref, carry_ref,
                      *, ts, width, n_exp):
    x = _gmlp_tile(x_ref[0], sh_ref, sc_ref, gt_ref, g_ref, win_ref, vg_ref, vb_ref,
                   ws_ref, bst_ref, wout_ref, gated_ref, ts, width)
    xo_ref[0] = x
    _route_tile(x, fsh_ref, fsc_ref, fg_ref, rwt_ref, rb_ref,
                h_ref, lpos_ref, gate_ref, tab_ref, tot_ref, carry_ref, ts, n_exp)


def _gmlp_tile(x, sh_ref, sc_ref, gt_ref, g_ref, win_ref, vg_ref, vb_ref,
               ws_ref, bst_ref, wout_ref, gated_ref, ts, width):
    h = _ada_norm(x, g_ref[...], sh_ref[0], sc_ref[0]).astype(BF16)
    z = _gelu_exact(_dot(h, win_ref[...]))
    u = z[:, :width]
    v = z[:, width:]
    mu = jnp.mean(v, axis=-1, keepdims=True)
    vc = v - mu
    var = jnp.mean(vc * vc, axis=-1, keepdims=True)
    vn = (vc * lax.rsqrt(var + EPS) * vg_ref[...] + vb_ref[...]).astype(BF16)

    gd = width // GMLP_GROUPS
    row = lax.broadcasted_iota(jnp.int32, (CHUNK, CHUNK), 0)
    col = lax.broadcasted_iota(jnp.int32, (CHUNK, CHUNK), 1)
    causal = col <= row
    bst = bst_ref[...]
    for g in range(GMLP_GROUPS):
        wc = jnp.where(causal, ws_ref[g], 0.0).astype(BF16)
        bias = bst[:, g:g + 1]
        for ci in range(ts // CHUNK):
            rs = slice(ci * CHUNK, (ci + 1) * CHUNK)
            cs = slice(g * gd, (g + 1) * gd)
            mixed = _dot(wc, vn[rs, cs]) + bias
            gated_ref[rs, cs] = (u[rs, cs] * mixed).astype(BF16)
    y = _dot(gated_ref[...], wout_ref[...])
    return x + gt_ref[0] * y


def _route_specs(bsz, s, d, ts, n_exp):
    nt = s // ts
    t = bsz * s
    vec = pl.BlockSpec((1, 1, d), lambda b, i: (b, 0, 0))
    tile = pl.BlockSpec((1, ts, d), lambda b, i: (b, i, 0))
    slot = pl.BlockSpec((TOP_K, ts), lambda b, i: (0, b * nt + i))
    out_shape = (
        jax.ShapeDtypeStruct((bsz, s, d), F32),
        jax.ShapeDtypeStruct((bsz, s, d), BF16),
        jax.ShapeDtypeStruct((TOP_K, t), jnp.int32),
        jax.ShapeDtypeStruct((TOP_K, t), F32),
        jax.ShapeDtypeStruct((bsz * nt, n_exp, LANES), F32),
        jax.ShapeDtypeStruct((n_exp, LANES), F32),
    )
    out_specs = (
        tile, tile, slot, slot,
        pl.BlockSpec((1, n_exp, LANES), lambda b, i: (b * nt + i, 0, 0)),
        pl.BlockSpec((n_exp, LANES), lambda b, i: (0, 0)),
    )
    in_specs = [
        vec, vec,
        pl.BlockSpec((1, d), lambda b, i: (0, 0)),
        pl.BlockSpec((n_exp, d), lambda b, i: (0, 0)),
        pl.BlockSpec((n_exp, 1), lambda b, i: (0, 0)),
    ]
    return out_shape, out_specs, in_specs


def _gmlp_route(x, sh, sc, gt, g, w_in, v_g, v_b, w_s, b_s, w_out,
                fsh, fsc, fg, router_w, router_b, ts):
    bsz, s, d = x.shape
    width = w_out.shape[0]
    n_exp = router_w.shape[1]
    vec = pl.BlockSpec((1, 1, d), lambda b, i: (b, 0, 0))
    const2 = lambda shape: pl.BlockSpec(shape, lambda b, i: (0, 0))
    out_shape, out_specs, route_in = _route_specs(bsz, s, d, ts, n_exp)
    return pl.pallas_call(
        functools.partial(_gmlproute_kernel, ts=ts, width=width, n_exp=n_exp),
        out_shape=out_shape,
        grid=(bsz, s // ts),
        in_specs=[
            pl.BlockSpec((1, ts, d), lambda b, i: (b, i, 0)),
            vec, vec, vec,
            const2((1, d)),
            const2((d, 2 * width)),
            const2((1, width)),
            const2((1, width)),
            pl.BlockSpec((GMLP_GROUPS, CHUNK, CHUNK), lambda b, i: (0, 0, 0)),
            const2((CHUNK, GMLP_GROUPS)),
            const2((width, d)),
        ] + route_in,
        out_specs=out_specs,
        scratch_shapes=[pltpu.VMEM((ts, width), BF16), pltpu.VMEM((n_exp, 1), F32)],
        compiler_params=_cparams(("arbitrary", "arbitrary")),
        name="gmlproute",
    )(x, sh, sc, gt, g.reshape(1, d), w_in.astype(BF16), v_g.reshape(1, width),
      v_b.reshape(1, width), w_s, b_s.T, w_out.astype(BF16),
      fsh, fsc, fg.reshape(1, d), router_w.T.astype(BF16), router_b.reshape(n_exp, 1))


def _mixroute_kernel(x_ref, o_ref, sg_ref, gt_ref, wo_ref, sh_ref, sc_ref, g_ref, rwt_ref, rb_ref,
                     xo_ref, h_ref, lpos_ref, gate_ref, tab_ref, tot_ref, carry_ref, *, ts, n_exp):
    gated = o_ref[0].astype(F32) * sg_ref[0].astype(F32)
    x = x_ref[0] + gt_ref[0] * _dot(gated.astype(BF16), wo_ref[...])
    xo_ref[0] = x
    _route_tile(x, sh_ref, sc_ref, g_ref, rwt_ref, rb_ref,
                h_ref, lpos_ref, gate_ref, tab_ref, tot_ref, carry_ref, ts, n_exp)


def _route_tile(x, sh_ref, sc_ref, g_ref, rwt_ref, rb_ref,
                h_ref, lpos_ref, gate_ref, tab_ref, tot_ref, carry_ref, ts, n_exp):
    first = jnp.logical_and(pl.program_id(0) == 0, pl.program_id(1) == 0)

    @pl.when(first)
    def _():
        carry_ref[...] = jnp.zeros_like(carry_ref)

    h = _ada_norm(x, g_ref[...], sh_ref[0], sc_ref[0]).astype(BF16)
    h_ref[0] = h
    logits = _dot_nt(rwt_ref[...], h) + rb_ref[...]

    eio = lax.broadcasted_iota(jnp.int32, (n_exp, ts), 0)
    work = logits
    vals, hots = [], []
    for _ in range(TOP_K):
        m = jnp.max(work, axis=0, keepdims=True)
        sel = jnp.min(jnp.where(work == m, eio, n_exp), axis=0, keepdims=True)
        hot = eio == sel
        vals.append(m)
        hots.append(hot)
        work = jnp.where(hot, -jnp.inf, work)

    exps = [jnp.exp(v - vals[0]) for v in vals]
    denom = exps[0]
    for e in exps[1:]:
        denom = denom + e
    gates = [e / denom for e in exps]

    chosen = hots[0]
    for hot in hots[1:]:
        chosen = jnp.logical_or(chosen, hot)
    chosen_f = jnp.where(chosen, 1.0, 0.0)
    r = lax.broadcasted_iota(jnp.int32, (ts, ts), 0)
    c = lax.broadcasted_iota(jnp.int32, (ts, ts), 1)
    before = jnp.where(r < c, 1.0, 0.0).astype(BF16)
    prefix = _dot(chosen_f.astype(BF16), before)
    count = jnp.sum(chosen_f, axis=1, keepdims=True)
    seg_len = jnp.floor((count + (SEG_ALIGN - 1)) * (1.0 / SEG_ALIGN)) * SEG_ALIGN
    er = lax.broadcasted_iota(jnp.int32, (n_exp, n_exp), 0)
    ec = lax.broadcasted_iota(jnp.int32, (n_exp, n_exp), 1)
    lower = jnp.where(ec < er, 1.0, 0.0).astype(BF16)
    seg_start = _dot(lower, jnp.broadcast_to(seg_len, (n_exp, LANES)).astype(BF16))[:, 0:1]
    base = seg_start + prefix
    lpos = [jnp.sum(jnp.where(hot, base, 0.0), axis=0, keepdims=True) for hot in hots]

    lpos_ref[...] = jnp.concatenate(lpos, axis=0).astype(jnp.int32)
    gate_ref[...] = jnp.concatenate(gates, axis=0)
    used = carry_ref[...]
    lane = lax.broadcasted_iota(jnp.int32, (n_exp, LANES), 1)
    tab_ref[0] = jnp.where(lane == 0, seg_start,
                           jnp.where(lane == 1, used, jnp.where(lane == 2, seg_len, 0.0)))
    carry_ref[...] = used + seg_len
    tot_ref[...] = jnp.broadcast_to(used + seg_len, tot_ref.shape)


def _mix_route(x, o, sg, gt, w_o, sh, sc, g, router_w, router_b, ts):
    bsz, s, d = x.shape
    n_exp = router_w.shape[1]
    vec = pl.BlockSpec((1, 1, d), lambda b, i: (b, 0, 0))
    tile = pl.BlockSpec((1, ts, d), lambda b, i: (b, i, 0))
    out_shape, out_specs, route_in = _route_specs(bsz, s, d, ts, n_exp)
    return pl.pallas_call(
        functools.partial(_mixroute_kernel, ts=ts, n_exp=n_exp),
        out_shape=out_shape,
        grid=(bsz, s // ts),
        in_specs=[tile, tile, tile, vec, pl.BlockSpec((d, d), lambda b, i: (0, 0))] + route_in,
        out_specs=out_specs,
        scratch_shapes=[pltpu.VMEM((n_exp, 1), F32)],
        compiler_params=_cparams(("arbitrary", "arbitrary")),
        name="mixroute",
    )(x, o, sg, gt, w_o.astype(BF16),
      sh, sc, g.reshape(1, d), router_w.T.astype(BF16), router_b.reshape(n_exp, 1))


def _wait_rows(src, dst, sem):
    pltpu.make_async_copy(src, dst, sem).wait()


def _chunk_sizes(max_rows):
    sizes = []
    size = SEG_ALIGN
    while size <= max_rows:
        sizes.append(size)
        size *= 2
    return sizes


def _for_each_chunk(length, max_rows, fn):
    for size in reversed(_chunk_sizes(max_rows)):
        shift = size.bit_length() - 1
        @pl.when(((length >> shift) & 1) == 1)
        def _():
            fn(pl.multiple_of((length >> (shift + 1)) << (shift + 1), SEG_ALIGN), size)


def _segment_copies(tile, n_exp, max_rows, src_ref, dst_ref, len_ref, make_copy):
    for e in range(n_exp):
        s = tile * n_exp + e
        src, dst = src_ref[s], dst_ref[s]
        _for_each_chunk(len_ref[s], max_rows,
                        lambda off, size: make_copy(pl.multiple_of(src + off, SEG_ALIGN),
                                                    pl.multiple_of(dst + off, SEG_ALIGN),
                                                    size).start())


def _segment_wait(tile, n_exp, lrows, src_ref, len_ref, make_copy):
    last = tile * n_exp + n_exp - 1
    total = src_ref[last] + len_ref[last]
    _for_each_chunk(total, lrows, lambda off, size: make_copy(0, 0, size).wait())


def _dispatch_kernel(pstart_ref, pend_ref, src_ref, dst_ref, len_ref, lpos_ref, h_ref, buf_hbm,
                     comp_ref, zero_ref, sem_seg, sem_zero,
                     *, ts, lrows, tm, n_exp, n_blocks, n_tiles):
    i = pl.program_id(0)
    slot = i % 2

    def seg_copy_from(buf_slot):
        def seg_copy(local_row, global_row, size):
            return pltpu.make_async_copy(comp_ref.at[buf_slot, pl.ds(local_row, size)],
                                         buf_hbm.at[pl.ds(global_row, size)], sem_seg.at[buf_slot])
        return seg_copy

    @pl.when(i == 0)
    def _():
        zero_ref[...] = jnp.zeros_like(zero_ref)
        for e in range(n_exp):
            @pl.when(pend_ref[e] > pstart_ref[e])
            def _():
                last = pl.multiple_of(pend_ref[e] - tm, tm)
                pltpu.make_async_copy(zero_ref, buf_hbm.at[pl.ds(last, tm)], sem_zero).start()
        n_used = pend_ref[n_exp - 1] // tm

        def zero_tail(b, carry):
            off = pl.multiple_of(b * tm, tm)
            pltpu.make_async_copy(zero_ref, buf_hbm.at[pl.ds(off, tm)], sem_zero).start()
            return carry

        def wait_tail(b, carry):
            _wait_rows(zero_ref, buf_hbm.at[pl.ds(0, tm)], sem_zero)
            return carry

        lax.fori_loop(n_used, n_blocks, zero_tail, 0)
        for e in range(n_exp):
            @pl.when(pend_ref[e] > pstart_ref[e])
            def _():
                _wait_rows(zero_ref, buf_hbm.at[pl.ds(0, tm)], sem_zero)
        lax.fori_loop(n_used, n_blocks, wait_tail, 0)

    @pl.when(i > 1)
    def _():
        _segment_wait(i - 2, n_exp, lrows, src_ref, len_ref, seg_copy_from(slot))

    lpos = lpos_ref[...]
    rows = lax.broadcasted_iota(jnp.int32, (MXU_ROWS, ts), 0)
    h = h_ref[...]
    for c in range(lrows // MXU_ROWS):
        rel = lpos - c * MXU_ROWS
        hit = rows == rel[0:1, :]
        for k in range(1, TOP_K):
            hit = jnp.logical_or(hit, rows == rel[k:k + 1, :])
        onehot = jnp.where(hit, 1.0, 0.0).astype(BF16)
        comp_ref[slot, c * MXU_ROWS:(c + 1) * MXU_ROWS, :] = _dot(onehot, h)
    _segment_copies(i, n_exp, ts, src_ref, dst_ref, len_ref, seg_copy_from(slot))

    @pl.when(i == n_tiles - 1)
    def _():
        @pl.when(i > 0)
        def _():
            _segment_wait(i - 1, n_exp, lrows, src_ref, len_ref, seg_copy_from(1 - slot))
        _segment_wait(i, n_exp, lrows, src_ref, len_ref, seg_copy_from(slot))


def _dispatch(pstart, pends, seg_src, seg_dst, seg_len, lpos, h, n_rows, ts, lrows):
    t, d = h.shape
    tm = EXPERT_ROWS
    n_tiles = t // ts
    tile_map = lambda i, *_: (i, 0)
    return pl.pallas_call(
        functools.partial(_dispatch_kernel, ts=ts, lrows=lrows, tm=tm, n_exp=pstart.shape[0],
                          n_blocks=n_rows // tm, n_tiles=n_tiles),
        out_shape=jax.ShapeDtypeStruct((n_rows, d), F32),
        grid_spec=pltpu.PrefetchScalarGridSpec(
            num_scalar_prefetch=5,
            grid=(n_tiles,),
            in_specs=[
                pl.BlockSpec((TOP_K, ts), lambda i, *_: (0, i)),
                pl.BlockSpec((ts, d), tile_map),
            ],
            out_specs=pl.BlockSpec(memory_space=pl.ANY),
            scratch_shapes=[
                pltpu.VMEM((2, lrows, d), F32),
                pltpu.VMEM((tm, d), F32),
                pltpu.SemaphoreType.DMA((2,)),
                pltpu.SemaphoreType.DMA,
            ],
        ),
        compiler_params=_cparams(("arbitrary",)),
        name="dispatch",
    )(pstart, pends, seg_src, seg_dst, seg_len, lpos, h)


def _expert_kernel(be_ref, nu_ref, nxt_ref, eslot_ref, x_ref, w1_hbm, b1_ref, w2_hbm, b2_ref, y_ref,
                   w1f_ref, w2f_ref, w1b_ref, w2b_ref, sem_w, *, d_exp, layer):
    i = pl.program_id(0)
    e = be_ref[i]
    changed = jnp.logical_or(i == 0, e != be_ref[jnp.maximum(i - 1, 0)])

    def weight_copies(expert, slot):
        return (pltpu.make_async_copy(w1_hbm.at[layer, expert], w1f_ref.at[slot], sem_w.at[slot]),
                pltpu.make_async_copy(w2_hbm.at[layer, expert], w2f_ref.at[slot], sem_w.at[slot]))

    @pl.when(i == 0)
    def _():
        for cp in weight_copies(e, eslot_ref[e]):
            cp.start()

    @pl.when(changed)
    def _():
        slot = eslot_ref[e]
        for cp in weight_copies(e, slot):
            cp.wait()
        w1b_ref[...] = w1f_ref[slot].astype(BF16)
        w2b_ref[...] = w2f_ref[slot].astype(BF16)
        nxt = nxt_ref[e]

        @pl.when(nxt >= 0)
        def _():
            for cp in weight_copies(nxt, 1 - slot):
                cp.start()

    @pl.when(i < nu_ref[0])
    def _():
        gu = _dot(x_ref[...].astype(BF16), w1b_ref[...]) + b1_ref[0, 0]
        glu = jnp.minimum(gu[:, :d_exp], SWIGLU_LIMIT)
        lin = jnp.clip(gu[:, d_exp:], -SWIGLU_LIMIT, SWIGLU_LIMIT)
        act = (lin + 1.0) * (glu * jax.nn.sigmoid(SWIGLU_ALPHA * glu))
        y_ref[...] = _dot(act.astype(BF16), w2b_ref[...]) + b2_ref[0, 0]

    @pl.when(i >= nu_ref[0])
    def _():
        y_ref[...] = jnp.zeros_like(y_ref)


def _experts(block_expert, n_used, next_expert, expert_slot, buf, w1, b1, w2, b2, layer):
    n_rows, d = buf.shape
    _, n_exp, _, d_exp2 = w1.shape
    d_exp = d_exp2 // 2
    tm = EXPERT_ROWS
    n_blocks = n_rows // tm

    def in_map(i, be, nu, *_):
        return (jnp.minimum(i, nu[0] - 1), 0)

    def out_map(i, *_):
        return (i, 0)

    def b_map(i, be, *_):
        return (layer, be[i], 0, 0)

    return pl.pallas_call(
        functools.partial(_expert_kernel, d_exp=d_exp, layer=layer),
        out_shape=jax.ShapeDtypeStruct((n_rows, d), F32),
        grid_spec=pltpu.PrefetchScalarGridSpec(
            num_scalar_prefetch=4,
            grid=(n_blocks,),
            in_specs=[
                pl.BlockSpec((tm, d), in_map),
                pl.BlockSpec(memory_space=pl.ANY),
                pl.BlockSpec((1, 1, 1, d_exp2), b_map),
                pl.BlockSpec(memory_space=pl.ANY),
                pl.BlockSpec((1, 1, 1, d), b_map),
            ],
            out_specs=pl.BlockSpec((tm, d), out_map),
            scratch_shapes=[
                pltpu.VMEM((2, d, d_exp2), F32),
                pltpu.VMEM((2, d_exp, d), F32),
                pltpu.VMEM((d, d_exp2), BF16),
                pltpu.VMEM((d_exp, d), BF16),
                pltpu.SemaphoreType.DMA((2,)),
            ],
        ),
        compiler_params=_cparams(("arbitrary",)),
        name="experts",
    )(block_expert, n_used, next_expert, expert_slot, buf, w1,
      b1.reshape(b1.shape[0], n_exp, 1, d_exp2), w2, b2.reshape(b2.shape[0], n_exp, 1, d))


def _combine_kernel(src_ref, dst_ref, len_ref, y_hbm, lpos_ref, gate_ref, x_ref, gt_ref, fg_ref,
                    o_ref, ycomp_ref, sem_seg, *, ts, lrows, n_exp, n_tiles, final_norm):
    i = pl.program_id(0)
    slot = i % 2

    def seg_copy_into(buf_slot):
        def seg_copy(local_row, global_row, size):
            return pltpu.make_async_copy(y_hbm.at[pl.ds(global_row, size)],
                                         ycomp_ref.at[buf_slot, pl.ds(local_row, size)],
                                         sem_seg.at[buf_slot])
        return seg_copy

    @pl.when(i == 0)
    def _():
        ycomp_ref[...] = jnp.zeros_like(ycomp_ref)
        _segment_copies(0, n_exp, ts, src_ref, dst_ref, len_ref, seg_copy_into(0))

    @pl.when(i + 1 < n_tiles)
    def _():
        _segment_copies(i + 1, n_exp, ts, src_ref, dst_ref, len_ref, seg_copy_into(1 - slot))

    _segment_wait(i, n_exp, lrows, src_ref, len_ref, seg_copy_into(slot))
    y = ycomp_ref[slot].astype(BF16)

    cols = lax.broadcasted_iota(jnp.int32, (COMBINE_CHUNK, lrows), 1)
    for c in range(ts // COMBINE_CHUNK):
        rs = slice(c * COMBINE_CHUNK, (c + 1) * COMBINE_CHUNK)
        lpos = lpos_ref[rs, :]
        gates = gate_ref[rs, :]
        sel = jnp.zeros((COMBINE_CHUNK, lrows), F32)
        for k in range(TOP_K):
            sel = jnp.where(cols == lpos[:, k:k + 1], gates[:, k:k + 1], sel)
        out = x_ref[rs, :] + gt_ref[0] * _dot(sel.astype(BF16), y)
        if final_norm:
            ms = jnp.mean(out * out, axis=-1, keepdims=True)
            out = out * lax.rsqrt(ms + EPS) * fg_ref[...]
        o_ref[rs, :] = out


def _combine(seg_src, seg_dst, seg_len, y_buf, lpos_t, gates_t, x2d, gt, final_g, s, final_norm,
             ts, lrows, n_exp):
    t, d = x2d.shape
    per_b = s // ts
    tile_map = lambda i, *_: (i, 0)
    return pl.pallas_call(
        functools.partial(_combine_kernel, ts=ts, lrows=lrows, n_exp=n_exp, n_tiles=t // ts,
                          final_norm=final_norm),
        out_shape=jax.ShapeDtypeStruct((t, d), F32),
        grid_spec=pltpu.PrefetchScalarGridSpec(
            num_scalar_prefetch=3,
            grid=(t // ts,),
            in_specs=[
                pl.BlockSpec(memory_space=pl.ANY),
                pl.BlockSpec((ts, TOP_K), tile_map),
                pl.BlockSpec((ts, TOP_K), tile_map),
                pl.BlockSpec((ts, d), tile_map),
                pl.BlockSpec((1, 1, d), lambda i, *_: (i // per_b, 0, 0)),
                pl.BlockSpec((1, d), lambda i, *_: (0, 0)),
            ],
            out_specs=pl.BlockSpec((ts, d), tile_map),
            scratch_shapes=[
                pltpu.VMEM((2, lrows, d), F32),
                pltpu.SemaphoreType.DMA((2,)),
            ],
        ),
        compiler_params=_cparams(("arbitrary",)),
        name="combine",
    )(seg_src, seg_dst, seg_len, y_buf, lpos_t, gates_t, x2d, gt, final_g.reshape(1, d))


def _moe_tile(s):
    return min(MOE_TILE, s)


def _moe_layer(routed, gt, w1, b1, w2, b2, layer, final_g, final_norm):
    x, h, lpos, gates, tab, tot = routed
    bsz, s, d = x.shape
    t = bsz * s
    n_exp = tot.shape[0]
    tm = EXPERT_ROWS
    ts = _moe_tile(s)
    n_tiles = t // ts
    lrows = -(-(ts * TOP_K + n_exp * (SEG_ALIGN - 1)) // MXU_ROWS) * MXU_ROWS

    rows = tot[:, 0].astype(jnp.int32)
    padded = (rows + tm - 1) // tm * tm
    pends = jnp.cumsum(padded).astype(jnp.int32)
    pstart = pends - padded
    n_blocks = -(-(t * TOP_K + n_tiles * n_exp * (SEG_ALIGN - 1) + n_exp * (tm - 1)) // tm)
    n_used = pends[-1] // tm
    blk = jnp.minimum(jnp.arange(n_blocks, dtype=jnp.int32), n_used - 1)
    block_expert = jnp.minimum(
        jnp.sum((pends[None, :] <= (blk * tm)[:, None]).astype(jnp.int32), axis=1), n_exp - 1)
    tab = tab[:, :, :3].astype(jnp.int32)
    seg_src = tab[:, :, 0].reshape(-1)
    seg_dst = (pstart[None, :] + tab[:, :, 1]).reshape(-1)
    seg_len = tab[:, :, 2].reshape(-1)

    buf = _dispatch(pstart, pends, seg_src, seg_dst, seg_len, lpos, h.reshape(t, d),
                    n_blocks * tm, ts, lrows)
    eid = jnp.arange(n_exp, dtype=jnp.int32)
    nonempty = rows > 0
    later = jnp.where(jnp.logical_and(nonempty[None, :], eid[None, :] > eid[:, None]),
                      eid[None, :], n_exp)
    next_expert = jnp.min(later, axis=1)
    next_expert = jnp.where(next_expert == n_exp, -1, next_expert).astype(jnp.int32)
    expert_slot = ((jnp.cumsum(nonempty) - nonempty) % 2).astype(jnp.int32)
    y_buf = _experts(block_expert, n_used.reshape(1), next_expert, expert_slot, buf,
                     w1, b1, w2, b2, layer)
    out = _combine(seg_src, seg_dst, seg_len, y_buf, lpos.T, gates.T, x.reshape(t, d), gt, final_g,
                   s, final_norm, ts, lrows, n_exp)
    return out.reshape(bsz, s, d)


def _head_rms(x, seg, seg_t2, inv_dh):
    hi, lo = _split2(x * x)
    ssum = _dot(hi, seg) + _dot(lo, seg)
    inv = lax.rsqrt(ssum * inv_dh + EPS)
    return _dot(jnp.concatenate(_split2(inv), axis=1), seg_t2)


def _kvq_kernel(x_ref, ksh_ref, ksc_ref, msh_ref, msc_ref, kg_ref, mg_ref,
                wk_ref, wvt_ref, wf_ref, fb_ref, kng_ref, qng_ref, wq_ref, wg_ref,
                place_ref, cst_ref,
                k_ref, kx_ref, vt_ref, q_ref, qx_ref, sg_ref, ccol_ref, *, ts, d):
    @pl.when(pl.program_id(1) == 0)
    def _():
        ccol_ref[...] = jnp.zeros_like(ccol_ref)

    x = x_ref[0]
    ms = jnp.mean(x * x, axis=-1, keepdims=True)
    xn = x * lax.rsqrt(ms + EPS)
    hk = ((xn * kg_ref[...]) * (1.0 + ksc_ref[0]) + ksh_ref[0]).astype(BF16)
    hq = ((xn * mg_ref[...]) * (1.0 + msc_ref[0]) + msh_ref[0]).astype(BF16)

    di = lax.broadcasted_iota(jnp.int32, (d, LANES), 0) // HEAD_DIM
    hi_ = lax.broadcasted_iota(jnp.int32, (d, LANES), 1)
    seg = jnp.where(di == hi_, 1.0, 0.0).astype(BF16)
    dt = lax.broadcasted_iota(jnp.int32, (2 * LANES, d), 1) // HEAD_DIM
    ht = lax.broadcasted_iota(jnp.int32, (2 * LANES, d), 0) % LANES
    seg_t2 = jnp.where(dt == ht, 1.0, 0.0).astype(BF16)
    inv_dh = 1.0 / HEAD_DIM

    k = _dot(hk, wk_ref[...])
    k_ref[0] = (k * _head_rms(k, seg, seg_t2, inv_dh) * kng_ref[...]).astype(BF16)
    vt_ref[0] = _dot_nt(wvt_ref[...], hk).astype(BF16)

    q = _dot(hq, wq_ref[...])
    qn = q * _head_rms(q, seg, seg_t2, inv_dh) * qng_ref[...]
    q_ref[0] = (qn * (HEAD_DIM ** -0.5 * LOG2E)).astype(BF16)
    sg_ref[0] = jax.nn.sigmoid(_dot(hq, wg_ref[...])).astype(BF16)

    lf = jax.nn.log_sigmoid(_dot(hk, wf_ref[...]) + fb_ref[...])
    r = lax.broadcasted_iota(jnp.int32, (ts, ts), 0)
    c = lax.broadcasted_iota(jnp.int32, (ts, ts), 1)
    lower = jnp.where(c <= r, 1.0, 0.0).astype(BF16)
    sums = _dot(lower, jnp.concatenate(_split3(lf), axis=1))
    cum = ccol_ref[...] + (sums[:, :LANES] + sums[:, LANES:2 * LANES] + sums[:, 2 * LANES:])
    ccol_ref[...] = cum[ts - 1:ts, :]
    parts = jnp.concatenate(_split3(cum * LOG2E), axis=1)
    place = place_ref[...].reshape(N_PARTS * LANES, 2 * LANES)
    extra = (_dot(parts, place) + cst_ref[...]).astype(BF16)
    kx_ref[0] = extra[:, :LANES]
    qx_ref[0] = extra[:, LANES:]


def _bias_lane_tables(n_heads):
    assert 2 * N_PARTS * n_heads <= LANES
    place = np.zeros((N_PARTS, LANES, 2 * LANES), np.float32)
    cst = np.zeros((1, 2 * LANES), np.float32)
    for h in range(n_heads):
        base = 2 * N_PARTS * h
        for part in range(N_PARTS):
            place[part, h, base + part] = 1.0
            place[part, h, LANES + base + N_PARTS + part] = 1.0
        cst[0, base + N_PARTS:base + 2 * N_PARTS] = 1.0
        cst[0, LANES + base:LANES + base + N_PARTS] = -1.0
    return jnp.asarray(place, BF16), jnp.asarray(cst, F32)


def _kvq(x, ksh, ksc, msh, msc, kv_norm_g, mix_norm_g, kv_w, k_norm_g, fgate_b,
         w_qg, q_norm_g, ts=256):
    bsz, s, d = x.shape
    n_heads = d // HEAD_DIM
    ts = min(ts, s)
    wk = kv_w[:, :d].astype(BF16)
    wvt = kv_w[:, d:2 * d].T.astype(BF16)
    wf = kv_w[:, 2 * d:]
    wf_pad = jnp.pad(wf, ((0, 0), (0, LANES - n_heads))).astype(BF16)
    fb_pad = jnp.pad(fgate_b, (0, LANES - n_heads)).reshape(1, LANES)
    place, cst = _bias_lane_tables(n_heads)
    vec = pl.BlockSpec((1, 1, d), lambda b, i: (b, 0, 0))
    c2 = lambda shape: pl.BlockSpec(shape, lambda b, i: (0, 0))
    tile = pl.BlockSpec((1, ts, d), lambda b, i: (b, i, 0))
    tile_x = pl.BlockSpec((1, ts, LANES), lambda b, i: (b, i, 0))
    return pl.pallas_call(
        functools.partial(_kvq_kernel, ts=ts, d=d),
        out_shape=(
            jax.ShapeDtypeStruct((bsz, s, d), BF16),
            jax.ShapeDtypeStruct((bsz, s, LANES), BF16),
            jax.ShapeDtypeStruct((bsz, d, s), BF16),
            jax.ShapeDtypeStruct((bsz, s, d), BF16),
            jax.ShapeDtypeStruct((bsz, s, LANES), BF16),
            jax.ShapeDtypeStruct((bsz, s, d), BF16),
        ),
        grid=(bsz, s // ts),
        in_specs=[
            tile, vec, vec, vec, vec,
            c2((1, d)), c2((1, d)),
            c2((d, d)), c2((d, d)), c2((d, LANES)), c2((1, LANES)),
            c2((1, d)), c2((1, d)),
            c2((d, d)), c2((d, d)),
            pl.BlockSpec((N_PARTS, LANES, 2 * LANES), lambda b, i: (0, 0, 0)),
            c2((1, 2 * LANES)),
        ],
        out_specs=(
            tile, tile_x,
            pl.BlockSpec((1, d, ts), lambda b, i: (b, 0, i)),
            tile, tile_x,
            tile,
        ),
        scratch_shapes=[pltpu.VMEM((1, LANES), F32)],
        compiler_params=_cparams(("arbitrary", "arbitrary")),
        name="kvq",
    )(x, ksh, ksc, msh, msc, kv_norm_g.reshape(1, d), mix_norm_g.reshape(1, d),
      wk, wvt, wf_pad, fb_pad,
      jnp.tile(k_norm_g, n_heads).reshape(1, d), jnp.tile(q_norm_g, n_heads).reshape(1, d),
      w_qg[:, :d].astype(BF16), w_qg[:, d:].astype(BF16), place, cst)


def _attn_kernel(q_ref, qx_ref, k_ref, kx_ref, vt_ref, o_ref, qh_ref, s0_ref, s1_ref, acc_ref,
                 *, tq, tk, npp):
    nh = 2 * npp
    first_pair = pl.program_id(1) * npp
    qi = pl.program_id(2)
    lane = lax.broadcasted_iota(jnp.int32, (tq, 2 * LANES), 1)
    for pp in range(npp):
        q = jnp.concatenate([q_ref[0, :, pp * LANES:(pp + 1) * LANES], qx_ref[0]], axis=1)
        zero = jnp.zeros_like(q)
        for hh in range(2):
            feat = jnp.logical_and(lane >= hh * HEAD_DIM, lane < (hh + 1) * HEAD_DIM)
            first_bias = LANES + 2 * N_PARTS * (2 * (first_pair + pp) + hh)
            bias = jnp.logical_and(lane >= first_bias, lane < first_bias + 2 * N_PARTS)
            qh_ref[2 * pp + hh] = jnp.where(feat | bias, q, zero)
    acc_ref[...] = jnp.zeros_like(acc_ref)
    ones = jnp.ones((DENOM_ROWS, tk), BF16)

    def scores_into(kb, s_ref):
        k0 = pl.multiple_of(kb * tk, tk)
        kx = kx_ref[0, pl.ds(k0, tk), :]
        for pp in range(npp):
            ks = jnp.concatenate([k_ref[0, pl.ds(k0, tk), pp * LANES:(pp + 1) * LANES], kx], axis=1)
            for hh in range(2):
                s_ref[2 * pp + hh] = _dot_nt(ks, qh_ref[2 * pp + hh])

    def softmax_pv(kb, s_ref, ms, masked):
        k0 = pl.multiple_of(kb * tk, tk)
        vt = vt_ref[0, :, pl.ds(k0, tk)]
        if masked:
            keep = (lax.broadcasted_iota(jnp.int32, (tk, tq), 0)
                    <= lax.broadcasted_iota(jnp.int32, (tk, tq), 1))
        out = []
        for h in range(nh):
            st = s_ref[h]
            if masked:
                st = jnp.where(keep, st, NEG_INF)
            m_new = jnp.maximum(ms[h], jnp.max(st, axis=0, keepdims=True))
            p = jnp.exp2(st - m_new).astype(BF16)
            alpha = jnp.exp2(ms[h] - m_new)
            vth = jnp.concatenate([vt[h * HEAD_DIM:(h + 1) * HEAD_DIM, :], ones], axis=0)
            acc_ref[h] = alpha * acc_ref[h] + _dot(vth, p)
            out.append(m_new)
        return tuple(out)

    def two_blocks(j, ms):
        scores_into(2 * j + 1, s1_ref)
        ms = softmax_pv(2 * j, s0_ref, ms, False)
        scores_into(2 * j + 2, s0_ref)
        return softmax_pv(2 * j + 1, s1_ref, ms, False)

    scores_into(0, s0_ref)
    ms = lax.fori_loop(0, qi // 2, two_blocks,
                       tuple(jnp.full((1, tq), NEG_INF, F32) for _ in range(nh)))

    @pl.when(qi % 2 == 1)
    def _():
        scores_into(qi, s1_ref)
        softmax_pv(qi, s1_ref, softmax_pv(qi - 1, s0_ref, ms, False), True)

    @pl.when(qi % 2 == 0)
    def _():
        softmax_pv(qi, s0_ref, ms, True)

    ot = jnp.concatenate(
        [acc_ref[h, :HEAD_DIM, :] / acc_ref[h, HEAD_DIM:HEAD_DIM + 1, :] for h in range(nh)],
        axis=0)
    o_ref[0] = ot.T.astype(BF16)


def _attention(q, qx, k, kx, vt, tq=512, npp=4):
    bsz, s, d = q.shape
    n_pairs = d // LANES
    npp = min(npp, n_pairs)
    tq = min(tq, s)
    tk = tq
    w = npp * LANES
    return pl.pallas_call(
        functools.partial(_attn_kernel, tq=tq, tk=tk, npp=npp),
        out_shape=jax.ShapeDtypeStruct((bsz, s, d), BF16),
        grid=(bsz, n_pairs // npp, s // tq),
        in_specs=[
            pl.BlockSpec((1, tq, w), lambda b, j, i: (b, i, j)),
            pl.BlockSpec((1, tq, LANES), lambda b, j, i: (b, i, 0)),
            pl.BlockSpec((1, s, w), lambda b, j, i: (b, 0, j)),
            pl.BlockSpec((1, s, LANES), lambda b, j, i: (b, 0, 0)),
            pl.BlockSpec((1, w, s), lambda b, j, i: (b, j, 0)),
        ],
        out_specs=pl.BlockSpec((1, tq, w), lambda b, j, i: (b, i, j)),
        scratch_shapes=[
            pltpu.VMEM((2 * npp, tq, 2 * LANES), BF16),
            pltpu.VMEM((2 * npp, tk, tq), F32),
            pltpu.VMEM((2 * npp, tk, tq), F32),
            pltpu.VMEM((2 * npp, HEAD_DIM + DENOM_ROWS, tq), F32),
        ],
        compiler_params=_cparams(("parallel", "parallel", "arbitrary")),
        name="attn",
    )(q, qx, k, kx, vt)


def _chunks(mod, n):
    return [m[:, None, :] for m in jnp.split(mod, n, axis=-1)]


def kernel(x, c, ada_w, ada_b, norm_mix_g, norm_ffn_g, gmlp_w_in, gmlp_v_g, gmlp_v_b, gmlp_w_s,
           gmlp_b_s, gmlp_w_out, kv_ada_w, kv_ada_b, kv_norm_g, kv_w, k_norm_g, fgate_b, fox_w_qg,
           q_norm_g, fox_w_o, router_w, router_b, exp_w1, exp_b1, exp_w2, exp_b2, final_g):
    mods = _ada(c, ada_w, ada_b)
    kv_mod = _ada(c, kv_ada_w[None], kv_ada_b[None])[0]

    sh_m, sc_m, gt_m, sh_f, sc_f, gt_f = _chunks(mods[0], 6)
    ts = _moe_tile(x.shape[1])
    routed = _gmlp_route(x, sh_m, sc_m, gt_m, norm_mix_g[0], gmlp_w_in[0], gmlp_v_g[0],
                         gmlp_v_b[0], gmlp_w_s[0], gmlp_b_s[0], gmlp_w_out[0],
                         sh_f, sc_f, norm_ffn_g[0], router_w[0], router_b[0], ts)
    x = _moe_layer(routed, gt_f, exp_w1, exp_b1, exp_w2, exp_b2, 0, final_g, False)

    sh_m, sc_m, gt_m, sh_f, sc_f, gt_f = _chunks(mods[1], 6)
    ksh, ksc = _chunks(kv_mod, 2)
    k, kx, vt, q, qx, sg = _kvq(x, ksh, ksc, sh_m, sc_m, kv_norm_g, norm_mix_g[1], kv_w,
                                k_norm_g, fgate_b, fox_w_qg[0], q_norm_g[0])
    o = _attention(q, qx, k, kx, vt)
    routed = _mix_route(x, o, sg, gt_m, fox_w_o[0], sh_f, sc_f, norm_ffn_g[1],
                        router_w[1], router_b[1], ts)
    return _moe_layer(routed, gt_f, exp_w1, exp_b1, exp_w2, exp_b2, 1, final_g, True)
```
